```python
import math
import jax, jax.numpy as jnp
from jax import lax
import numpy as np

D_MODEL = 1024
BATCH = 1
SEQ = 16384
DEPTH = 4

N_MIXERS = 4
EPS = 1e-6
NEG = -1e30
MLSTM_HEADS = 4
MLSTM_DQK = 128
MLSTM_DV = D_MODEL // MLSTM_HEADS
MLSTM_CHUNK = 128
MLSTM_IN = 2 * MLSTM_HEADS * MLSTM_DQK + 2 * MLSTM_HEADS * MLSTM_DV + 2 * MLSTM_HEADS
ATTN_HEADS = 16
ATTN_DH = D_MODEL // ATTN_HEADS
ROPE_DIMS = ATTN_DH // 4
ROPE_THETA = 500000.0
MOBA_BLOCK = 256
MOBA_TOPK = 3
Q_BLOCK = 128
CONV_WIDTH = 3
FFN_HIDDEN = -(-(8 * D_MODEL) // (3 * 256)) * 256

kernel_name = "hybrid_mlstm_moba_shortconv_stickbreak"


def rms_norm(x, g):
    xf = x.astype(jnp.float32)
    y = xf * lax.rsqrt(jnp.mean(xf * xf, axis=-1, keepdims=True) + EPS)
    return (y * g.astype(jnp.float32)).astype(x.dtype)


def partial_rope(x, positions):
    half = ROPE_DIMS // 2
    inv = ROPE_THETA ** (-jnp.arange(half, dtype=jnp.float32) / half)
    ang = positions.astype(jnp.float32)[..., None] * inv
    cos = jnp.cos(ang)[:, :, None, :]
    sin = jnp.sin(ang)[:, :, None, :]
    xr = x[..., :ROPE_DIMS].astype(jnp.float32)
    x1, x2 = xr[..., :half], xr[..., half:]
    rot = jnp.concatenate([x1 * cos - x2 * sin, x2 * cos + x1 * sin], axis=-1).astype(x.dtype)
    return jnp.concatenate([rot, x[..., ROPE_DIMS:]], axis=-1)


def mlstm_mixer(h, w_in, b_gate, head_gain, w_out):
    B, S, _ = h.shape
    NH, DK, DV, L = MLSTM_HEADS, MLSTM_DQK, MLSTM_DV, MLSTM_CHUNK
    nc = S // L
    proj = h @ w_in
    q, k, v, o_pre, g_pre = jnp.split(
        proj, [NH * DK, 2 * NH * DK, 2 * NH * DK + NH * DV, 2 * NH * DK + 2 * NH * DV], axis=-1)
    g = g_pre.astype(jnp.float32) + b_gate.astype(jnp.float32)
    i_log = g[..., :NH]
    f_log = jax.nn.log_sigmoid(g[..., NH:])

    def to_chunks(t, d):
        return t.astype(jnp.float32).reshape(B, nc, L, NH, d).transpose(1, 0, 3, 2, 4)

    def gate_chunks(t):
        return t.reshape(B, nc, L, NH).transpose(1, 0, 3, 2)

    qc = to_chunks(q, DK) * (DK ** -0.5)
    kc = to_chunks(k, DK)
    vc = to_chunks(v, DV)
    ic = gate_chunks(i_log)
    fc = gate_chunks(f_log)
    causal = jnp.tril(jnp.ones((L, L), dtype=bool))

    def step(carry, xs):
        C, n, m = carry
        q_, k_, v_, i_, f_ = xs
        b = jnp.cumsum(f_, axis=-1)
        dmat = jnp.where(causal, b[..., :, None] - b[..., None, :] + i_[..., None, :], -jnp.inf)
        inter = b + m[..., None]
        m_t = jnp.maximum(inter, jnp.max(dmat, axis=-1))
        w_intra = jnp.exp(dmat - m_t[..., None])
        w_inter = jnp.exp(inter - m_t)
        s = jnp.einsum('bhtd,bhsd->bhts', q_, k_) * w_intra
        num = jnp.einsum('bhts,bhsv->bhtv', s, v_) + w_inter[..., None] * jnp.einsum('bhtd,bhdv->bhtv', q_, C)
        den = jnp.sum(s, axis=-1) + w_inter * jnp.einsum('bhtd,bhd->bht', q_, n)
        den = jnp.maximum(jnp.abs(den), jnp.exp(-m_t))
        h_out = num / den[..., None]
        g_tot = b[..., -1]
        a = g_tot[..., None] - b + i_
        m_new = jnp.maximum(g_tot + m, jnp.max(a, axis=-1))
        decay = jnp.exp(g_tot + m - m_new)
        wa = jnp.exp(a - m_new[..., None])
        C_new = decay[..., None, None] * C + jnp.einsum('bhs,bhsd,bhsv->bhdv', wa, k_, v_)
        n_new = decay[..., None] * n + jnp.einsum('bhs,bhsd->bhd', wa, k_)
        return (C_new, n_new, m_new), h_out

    init = (jnp.zeros((B, NH, DK, DV), jnp.float32),
            jnp.zeros((B, NH, DK), jnp.float32),
            jnp.zeros((B, NH), jnp.float32))
    _, hs = lax.scan(step, init, (qc, kc, vc, ic, fc))
    hs = hs.transpose(1, 0, 3, 2, 4).reshape(B, S, NH, DV)
    hs = hs * lax.rsqrt(jnp.mean(hs * hs, axis=-1, keepdims=True) + EPS)
    hs = hs * head_gain.astype(jnp.float32).reshape(NH, DV)
    out = hs.reshape(B, S, NH * DV).astype(h.dtype) * jax.nn.sigmoid(o_pre)
    return out @ w_out


def moba_mixer(h, positions, w_qkv, w_out):
    B, S, _ = h.shape
    H, Dh, BLK, QB = ATTN_HEADS, ATTN_DH, MOBA_BLOCK, Q_BLOCK
    q, k, v = jnp.split(h @ w_qkv, 3, axis=-1)
    q = partial_rope(q.reshape(B, S, H, Dh), positions)
    k = partial_rope(k.reshape(B, S, H, Dh), positions)
    v = v.reshape(B, S, H, Dh)
    nb = -(-S // BLK)
    pad = nb * BLK - S
    k = jnp.pad(k, ((0, 0), (0, pad), (0, 0), (0, 0))).astype(jnp.float32)
    v = jnp.pad(v, ((0, 0), (0, pad), (0, 0), (0, 0))).astype(jnp.float32)
    kb = k.transpose(0, 2, 1, 3).reshape(B, H, nb, BLK, Dh)
    vb = v.transpose(0, 2, 1, 3).reshape(B, H, nb, BLK, Dh)
    counts = jnp.minimum(BLK, S - jnp.arange(nb) * BLK).astype(jnp.float32)
    kmean = jnp.sum(kb, axis=3) / counts[:, None]
    topk = min(MOBA_TOPK, nb)
    nqb = S // QB
    qblocks = q.astype(jnp.float32).transpose(0, 2, 1, 3).reshape(B, H, nqb, QB, Dh).transpose(2, 0, 1, 3, 4)
    bi = jnp.arange(B)[:, None, None, None]
    hi = jnp.arange(H)[None, :, None, None]
    scale = Dh ** -0.5
    blk_ids = jnp.arange(nb)

    def attend(args):
        qi, qf = args
        cur = (qi * QB) // BLK
        gate = jnp.einsum('bhqd,bhnd->bhqn', qf, kmean)
        gate = jnp.where(blk_ids < cur, gate, NEG)
        _, idx = lax.top_k(gate, topk)
        sel_ok = idx < cur
        ksel = kb[bi, hi, idx]
        vsel = vb[bi, hi, idx]
        s_sel = jnp.einsum('bhqd,bhqjkd->bhqjk', qf, ksel) * scale
        s_sel = jnp.where(sel_ok[..., None], s_sel, NEG).reshape(B, H, QB, topk * BLK)
        k_own = lax.dynamic_index_in_dim(kb, cur, axis=2, keepdims=False)
        v_own = lax.dynamic_index_in_dim(vb, cur, axis=2, keepdims=False)
        s_own = jnp.einsum('bhqd,bhkd->bhqk', qf, k_own) * scale
        qpos = qi * QB + jnp.arange(QB)
        kpos = cur * BLK + jnp.arange(BLK)
        s_own = jnp.where(kpos[None, :] <= qpos[:, None], s_own, NEG)
        p = jax.nn.softmax(jnp.concatenate([s_sel, s_own], axis=-1), axis=-1)
        p_sel = p[..., :topk * BLK].reshape(B, H, QB, topk, BLK)
        p_own = p[..., topk * BLK:]
        return (jnp.einsum('bhqjk,bhqjkd->bhqd', p_sel, vsel)
                + jnp.einsum('bhqk,bhkd->bhqd', p_own, v_own))

    outs = lax.map(attend, (jnp.arange(nqb), qblocks))
    o = outs.transpose(1, 0, 3, 2, 4).reshape(B, S, H * Dh).astype(h.dtype)
    return o @ w_out


def short_conv_mixer(h, w_in, conv_w, w_out):
    b_gate, c_gate, u = jnp.split(h @ w_in, 3, axis=-1)
    z = c_gate * u
    y = lax.conv_general_dilated(
        z, conv_w[:, None, :].astype(z.dtype), window_strides=(1,), padding=[(CONV_WIDTH - 1, 0)],
        dimension_numbers=('NWC', 'WIO', 'NWC'), feature_group_count=D_MODEL)
    return (b_gate * y) @ w_out


def stick_breaking_mixer(h, w_qkv, w_out):
    B, S, _ = h.shape
    H, Dh, QB = ATTN_HEADS, ATTN_DH, Q_BLOCK
    q, k, v = jnp.split(h @ w_qkv, 3, axis=-1)
    q = q.astype(jnp.float32).reshape(B, S, H, Dh).transpose(0, 2, 1, 3)
    k = k.astype(jnp.float32).reshape(B, S, H, Dh).transpose(0, 2, 1, 3)
    v = v.astype(jnp.float32).reshape(B, S, H, Dh).transpose(0, 2, 1, 3)
    nqb = S // QB
    qblocks = q.reshape(B, H, nqb, QB, Dh).transpose(2, 0, 1, 3, 4)
    kpos = jnp.arange(S)
    scale = Dh ** -0.5

    def attend(args):
        qi, qf = args
        z = jnp.einsum('bhqd,bhkd->bhqk', qf, k) * scale
        qpos = qi * QB + jnp.arange(QB)
        strict = kpos[None, :] < qpos[:, None]
        log_fail = jnp.where(strict, jax.nn.log_sigmoid(-z), 0.0)
        after = lax.cumsum(log_fail, axis=3, reverse=True) - log_fail
        a = jnp.exp(jnp.where(strict, jax.nn.log_sigmoid(z) + after, -jnp.inf))
        return jnp.einsum('bhqk,bhkd->bhqd', a, v)

    outs = lax.map(attend, (jnp.arange(nqb), qblocks))
    o = outs.transpose(1, 0, 3, 2, 4).reshape(B, S, H * Dh).astype(h.dtype)
    return o @ w_out


def swiglu(h, w_gate, w_up, w_down):
    return (jax.nn.silu(h @ w_gate) * (h @ w_up)) @ w_down


def setup_inputs(seed: int = 0) -> dict:
    key = jax.random.key(seed)
    ks = jax.random.split(key, 20)
    D, F = D_MODEL, FFN_HIDDEN
    n_a = (DEPTH + 3) // 4
    n_b = (DEPTH + 2) // 4
    n_c = (DEPTH + 1) // 4
    n_d = DEPTH // 4
    nrm = lambda k, shape, fan: jax.random.normal(k, shape, jnp.float32) * fan ** -0.5
    x = jax.random.normal(ks[0], (BATCH, SEQ, D), jnp.float32)
    positions = jnp.broadcast_to(jnp.arange(SEQ, dtype=jnp.int32), (BATCH, SEQ))
    norm_gains = 1.0 + 0.1 * jax.random.normal(ks[1], (DEPTH, 2, D), jnp.float32)
    mlstm_w_in = nrm(ks[2], (n_a, D, MLSTM_IN), D)
    ib = 0.1 * jax.random.normal(ks[3], (n_a, MLSTM_HEADS), jnp.float32)
    fb = 3.0 + 0.1 * jax.random.normal(ks[4], (n_a, MLSTM_HEADS), jnp.float32)
    mlstm_b_gate = jnp.concatenate([ib, fb], axis=-1)
    mlstm_head_gain = 1.0 + 0.1 * jax.random.normal(ks[5], (n_a, MLSTM_HEADS * MLSTM_DV), jnp.float32)
    mlstm_w_out = nrm(ks[6], (n_a, MLSTM_HEADS * MLSTM_DV, D), MLSTM_HEADS * MLSTM_DV)
    moba_w_qkv = nrm(ks[7], (n_b, D, 3 * D), D)
    moba_w_out = nrm(ks[8], (n_b, D, D), D)
    conv_w_in = nrm(ks[9], (n_c, D, 3 * D), D)
    conv_w = nrm(ks[10], (n_c, CONV_WIDTH, D), CONV_WIDTH)
    conv_w_out = nrm(ks[11], (n_c, D, D), D)
    sb_w_qkv = nrm(ks[12], (n_d, D, 3 * D), D)
    sb_w_out = nrm(ks[13], (n_d, D, D), D)
    ffn_w_gate = nrm(ks[14], (DEPTH, D, F), D)
    ffn_w_up = nrm(ks[15], (DEPTH, D, F), D)
    ffn_w_down = nrm(ks[16], (DEPTH, F, D), F)
    final_gain = 1.0 + 0.1 * jax.random.normal(ks[17], (D,), jnp.float32)
    return {"x": x, "positions": positions, "norm_gains": norm_gains,
            "mlstm_w_in": mlstm_w_in, "mlstm_b_gate": mlstm_b_gate,
            "mlstm_head_gain": mlstm_head_gain, "mlstm_w_out": mlstm_w_out,
            "moba_w_qkv": moba_w_qkv, "moba_w_out": moba_w_out,
            "conv_w_in": conv_w_in, "conv_w": conv_w, "conv_w_out": conv_w_out,
            "sb_w_qkv": sb_w_qkv, "sb_w_out": sb_w_out,
            "ffn_w_gate": ffn_w_gate, "ffn_w_up": ffn_w_up, "ffn_w_down": ffn_w_down,
            "final_gain": final_gain}


def reference(x, positions, norm_gains, mlstm_w_in, mlstm_b_gate, mlstm_head_gain, mlstm_w_out,
              moba_w_qkv, moba_w_out, conv_w_in, conv_w, conv_w_out, sb_w_qkv, sb_w_out,
              ffn_w_gate, ffn_w_up, ffn_w_down, final_gain):
    h = x
    for layer in range(DEPTH):
        kind = layer % N_MIXERS
        j = layer // N_MIXERS
        a = rms_norm(h, norm_gains[layer, 0])
        if kind == 0:
            mix = mlstm_mixer(a, mlstm_w_in[j], mlstm_b_gate[j], mlstm_head_gain[j], mlstm_w_out[j])
        elif kind == 1:
            mix = moba_mixer(a, positions, moba_w_qkv[j], moba_w_out[j])
        elif kind == 2:
            mix = short_conv_mixer(a, conv_w_in[j], conv_w[j], conv_w_out[j])
        else:
            mix = stick_breaking_mixer(a, sb_w_qkv[j], sb_w_out[j])
        h = h + mix
        h = h + swiglu(rms_norm(h, norm_gains[layer, 1]), ffn_w_gate[layer], ffn_w_up[layer], ffn_w_down[layer])
    return rms_norm(h, final_gain)
```

```python
import functools

import jax
import jax.numpy as jnp
from jax import lax
from jax.experimental import pallas as pl
from jax.experimental.pallas import tpu as pltpu

F32 = jnp.float32
BF16 = jnp.bfloat16

EPS = 1e-6
NEG = -1e30
D_MODEL = 1024
MLSTM_HEADS = 4
MLSTM_DQK = 128
MLSTM_DV = 256
MLSTM_CHUNK = 128
ATTN_HEADS = 16
ATTN_DH = 64
ROPE_DIMS = 16
ROPE_THETA = 500000.0
MOBA_BLOCK = 256
MOBA_TOPK = 3
Q_BLOCK = 128
SB_KEY_TILE = 256
CONV_WIDTH = 3

LANES = 128
ROW_TILE = 512
FFN_CHUNKS = 2
VMEM_LIMIT = 56 * 1024 * 1024


def _cparams(*sem):
    return pltpu.CompilerParams(dimension_semantics=sem, vmem_limit_bytes=VMEM_LIMIT)


def _const_spec(shape):
    nd = len(shape)
    return pl.BlockSpec(shape, lambda *_: (0,) * nd, pipeline_mode=pl.Buffered(1))


def _rms(x, g):
    return x * lax.rsqrt(jnp.mean(x * x, axis=-1, keepdims=True) + EPS) * g


def _dot(a, b):
    return jnp.dot(a, b, preferred_element_type=F32)


def _dot_nt(a, b):
    return lax.dot_general(a, b, (((1,), (1,)), ((), ())), preferred_element_type=F32)


def _dot_tn(a, b):
    return lax.dot_general(a, b, (((0,), (0,)), ((), ())), preferred_element_type=F32)


def _mlstm_proj_kernel(h_ref, g_ref, wq_ref, wk_ref, wv_ref, wo_ref, wg_ref,
                       q_ref, k_ref, v_ref, o_ref, gate_ref):
    a = _rms(h_ref[...], g_ref[...]).astype(BF16)
    q_ref[...] = (_dot(a, wq_ref[...]) * (MLSTM_DQK ** -0.5)).astype(BF16)
    k_ref[...] = _dot(a, wk_ref[...]).astype(BF16)
    v_ref[...] = _dot(a, wv_ref[...]).astype(BF16)
    o_ref[...] = _dot(a, wo_ref[...])
    gate_ref[...] = _dot(a, wg_ref[...])


def _mlstm_proj(h, gain, w_in):
    S, D = h.shape
    NH, DK, DV = MLSTM_HEADS, MLSTM_DQK, MLSTM_DV
    nq, nv = NH * DK, NH * DV
    wq = w_in[:, :nq].astype(BF16)
    wk = w_in[:, nq:2 * nq].astype(BF16)
    wv = w_in[:, 2 * nq:2 * nq + nv].astype(BF16)
    wo = w_in[:, 2 * nq + nv:2 * nq + 2 * nv].astype(BF16)
    wg = jnp.pad(w_in[:, 2 * nq + 2 * nv:], ((0, 0), (0, LANES - 2 * NH))).astype(BF16)
    tm = min(ROW_TILE, S)
    row = lambda n: pl.BlockSpec((tm, n), lambda i: (i, 0))
    return pl.pallas_call(
        _mlstm_proj_kernel,
        grid=(S // tm,),
        in_specs=[row(D), _const_spec((1, D)), _const_spec((D, nq)), _const_spec((D, nq)),
                  _const_spec((D, nv)), _const_spec((D, nv)), _const_spec((D, LANES))],
        out_specs=[row(nq), row(nq), row(nv), row(nv), row(LANES)],
        out_shape=[jax.ShapeDtypeStruct((S, nq), BF16), jax.ShapeDtypeStruct((S, nq), BF16),
                   jax.ShapeDtypeStruct((S, nv), BF16), jax.ShapeDtypeStruct((S, nv), F32),
                   jax.ShapeDtypeStruct((S, LANES), F32)],
        compiler_params=_cparams("arbitrary"),
        name="mlstm_proj",
    )(h, gain.reshape(1, D), wq, wk, wv, wo, wg)


def _mlstm_core_kernel(q_ref, k_ref, v_ref, o_ref, gate_ref, bg_ref, hg_ref, out_ref,
                       c_ref, n_ref, m_ref):
    NH, DK, DV, L = MLSTM_HEADS, MLSTM_DQK, MLSTM_DV, MLSTM_CHUNK

    @pl.when(pl.program_id(0) == 0)
    def _():
        c_ref[...] = jnp.zeros_like(c_ref)
        n_ref[...] = jnp.zeros_like(n_ref)
        m_ref[...] = jnp.zeros_like(m_ref)

    g = gate_ref[...] + bg_ref[...]
    lane = lax.broadcasted_iota(jnp.int32, (L, LANES), 1)
    log_sig = jnp.minimum(g, 0.0) - jnp.log(1.0 + jnp.exp(-jnp.abs(g)))
    gl = jnp.where(lane >= NH, log_sig, g)
    gl_t = gl.T
    t_idx = lax.broadcasted_iota(jnp.int32, (L, L), 0)
    s_idx = lax.broadcasted_iota(jnp.int32, (L, L), 1)
    causal = s_idx <= t_idx

    for hd in range(NH):
        i_col = gl[:, hd:hd + 1]
        f_col = gl[:, NH + hd:NH + hd + 1]
        i_row = gl_t[hd:hd + 1, :]
        f_row = gl_t[NH + hd:NH + hd + 1, :]
        b_col = jnp.sum(jnp.where(causal, f_row, 0.0), axis=1, keepdims=True)
        b_row = jnp.sum(jnp.where(t_idx <= s_idx, f_col, 0.0), axis=0, keepdims=True)
        m_prev = m_ref[hd]
        dmat = jnp.where(causal, b_col - b_row + i_row, -jnp.inf)
        inter = b_col + m_prev
        m_t = jnp.maximum(inter, jnp.max(dmat, axis=1, keepdims=True))
        w_intra = jnp.exp(dmat - m_t)
        w_inter = jnp.exp(inter - m_t)

        q = q_ref[:, hd * DK:(hd + 1) * DK]
        k = k_ref[:, hd * DK:(hd + 1) * DK]
        v = v_ref[:, hd * DV:(hd + 1) * DV]
        c_old = c_ref[hd]
        n_old = n_ref[hd]
        s = _dot_nt(q, k) * w_intra
        num = _dot(s.astype(BF16), v) + w_inter * _dot(q, c_old.astype(BF16))
        qn = jnp.sum(q.astype(F32) * n_old, axis=1, keepdims=True)
        den = jnp.sum(s, axis=1, keepdims=True) + w_inter * qn
        den = jnp.maximum(jnp.abs(den), jnp.exp(-m_t))
        h_out = num / den

        g_tot = b_col[L - 1:L, :]
        a_col = g_tot - b_col + i_col
        m_new = jnp.maximum(g_tot + m_prev, jnp.max(a_col, axis=0, keepdims=True))
        decay = jnp.exp(g_tot + m_prev - m_new)
        wa = jnp.exp(a_col - m_new)
        kw = k.astype(F32) * wa
        c_ref[hd] = decay * c_old + _dot_tn(kw.astype(BF16), v)
        n_ref[hd] = decay * n_old + jnp.sum(kw, axis=0, keepdims=True)
        m_ref[hd] = m_new

        hn = h_out * lax.rsqrt(jnp.mean(h_out * h_out, axis=1, keepdims=True) + EPS)
        hn = hn * hg_ref[:, hd * DV:(hd + 1) * DV]
        out = hn * jax.nn.sigmoid(o_ref[:, hd * DV:(hd + 1) * DV])
        out_ref[:, hd * DV:(hd + 1) * DV] = out.astype(BF16)


def _mlstm_core(q, k, v, o, gates, b_gate, head_gain):
    S = q.shape[0]
    NH, DK, DV, L = MLSTM_HEADS, MLSTM_DQK, MLSTM_DV, MLSTM_CHUNK
    bg = jnp.pad(b_gate.astype(F32), (0, LANES - 2 * NH)).reshape(1, LANES)
    row = lambda n: pl.BlockSpec((L, n), lambda c: (c, 0))
    return pl.pallas_call(
        _mlstm_core_kernel,
        grid=(S // L,),
        in_specs=[row(NH * DK), row(NH * DK), row(NH * DV), row(NH * DV), row(LANES),
                  _const_spec((1, LANES)), _const_spec((1, NH * DV))],
        out_specs=row(NH * DV),
        out_shape=jax.ShapeDtypeStruct((S, NH * DV), BF16),
        scratch_shapes=[pltpu.VMEM((NH, DK, DV), F32), pltpu.VMEM((NH, 1, DK), F32),
                        pltpu.VMEM((NH, 1, 1), F32)],
        compiler_params=_cparams("arbitrary"),
        name="mlstm_core",
    )(q, k, v, o, gates, bg, head_gain.astype(F32).reshape(1, NH * DV))


def _rope_tile(x, cos, sin_lo, sin_hi):
    half = ROPE_DIMS // 2
    cols = []
    for c in range(x.shape[1] // LANES):
        xc = x[:, c * LANES:(c + 1) * LANES]
        up = pltpu.roll(xc, LANES - half, axis=1)
        down = pltpu.roll(xc, half, axis=1)
        cols.append(xc * cos + up * sin_lo + down * sin_hi)
    return jnp.concatenate(cols, axis=1)


def _qkv_proj_kernel(*refs, rope):
    if rope:
        (h_ref, g_ref, wq_ref, wk_ref, wv_ref, pos_ref, inv_ref,
         q_ref, k_ref, v_ref, kmean_ref) = refs
    else:
        h_ref, g_ref, wq_ref, wk_ref, wv_ref, q_ref, k_ref, v_ref = refs
    a = _rms(h_ref[...], g_ref[...]).astype(BF16)
    q = _dot(a, wq_ref[...])
    k = _dot(a, wk_ref[...])
    v_ref[...] = _dot(a, wv_ref[...]).astype(BF16)
    if rope:
        tm = q.shape[0]
        half = ROPE_DIMS // 2
        ang = pos_ref[...].astype(F32) * inv_ref[...]
        cos = jnp.cos(ang)
        sin = jnp.sin(ang)
        dim = lax.broadcasted_iota(jnp.int32, (tm, LANES), 1) % ATTN_DH
        sin_lo = jnp.where(dim < half, -sin, 0.0)
        sin_hi = jnp.where((dim >= half) & (dim < ROPE_DIMS), sin, 0.0)
        q = _rope_tile(q, cos, sin_lo, sin_hi)
        k = _rope_tile(k, cos, sin_lo, sin_hi)
        nblk = tm // MOBA_BLOCK
        kmean_ref[0] = jnp.sum(k.reshape(nblk, MOBA_BLOCK, k.shape[1]), axis=1) * (1.0 / MOBA_BLOCK)
    q_ref[...] = q.astype(BF16)
    k_ref[...] = k.astype(BF16)


def _qkv_proj(h, gain, w_qkv, positions=None):
    S, D = h.shape
    rope = positions is not None
    wq = w_qkv[:, :D].astype(BF16)
    wk = w_qkv[:, D:2 * D].astype(BF16)
    wv = w_qkv[:, 2 * D:].astype(BF16)
    tm = min(ROW_TILE, S)
    row = lambda n: pl.BlockSpec((tm, n), lambda i: (i, 0))
    in_specs = [row(D), _const_spec((1, D)), _const_spec((D, D)), _const_spec((D, D)), _const_spec((D, D))]
    args = [h, gain.reshape(1, D), wq, wk, wv]
    out_specs = [row(D), row(D), row(D)]
    out_shape = [jax.ShapeDtypeStruct((S, D), BF16)] * 3
    if rope:
        assert S % MOBA_BLOCK == 0 and tm % MOBA_BLOCK == 0
        half = ROPE_DIMS // 2
        inv = ROPE_THETA ** (-jnp.arange(half, dtype=F32) / half)
        dim = jnp.arange(LANES) % ATTN_DH
        inv_lane = jnp.where(dim < ROPE_DIMS, inv[dim % half], 0.0).astype(F32).reshape(1, LANES)
        in_specs += [row(1), _const_spec((1, LANES))]
        args += [positions.reshape(S, 1), inv_lane]
        nblk = tm // MOBA_BLOCK
        out_specs.append(pl.BlockSpec((1, nblk, D), lambda i: (i, 0, 0)))
        out_shape.append(jax.ShapeDtypeStruct((S // tm, nblk, D), F32))
    outs = pl.pallas_call(
        functools.partial(_qkv_proj_kernel, rope=rope),
        grid=(S // tm,),
        in_specs=in_specs,
        out_specs=out_specs,
        out_shape=out_shape,
        compiler_params=_cparams("arbitrary"),
        name="moba_proj" if rope else "sb_proj",
    )(*args)
    if rope:
        q, k, v, kmean = outs
        return q, k, v, kmean.reshape(S // MOBA_BLOCK, D)
    return outs


def _split_pair(q):
    lane = lax.broadcasted_iota(jnp.int32, q.shape, 1)
    zero = jnp.zeros_like(q)
    return jnp.where(lane < ATTN_DH, q, zero), jnp.where(lane >= ATTN_DH, q, zero)


def _merge_pair(acc_a, acc_b):
    lane = lax.broadcasted_iota(jnp.int32, acc_a.shape, 1)
    return jnp.where(lane < ATTN_DH, acc_a, acc_b)


def _moba_select(gate, cur):
    nb = gate.shape[1]
    blk = lax.broadcasted_iota(jnp.int32, gate.shape, 1)
    valid = blk < cur
    g = jnp.where(valid, gate, -jnp.inf)
    sel = jnp.zeros(gate.shape, F32)
    for _ in range(MOBA_TOPK):
        mx = jnp.max(g, axis=1, keepdims=True)
        first = jnp.min(jnp.where(g == mx, blk, nb), axis=1, keepdims=True)
        hit = (blk == first) & valid
        sel = jnp.where(hit, 1.0, sel)
        g = jnp.where(blk == first, -jnp.inf, g)
    return sel


def _moba_attn_kernel(q_ref, k_ref, v_ref, kmean_ref, out_ref):
    QB, BLK = Q_BLOCK, MOBA_BLOCK
    scale = ATTN_DH ** -0.5
    i = pl.program_id(1)
    cur = (i * QB) // BLK
    qs = _split_pair(q_ref[...])
    km = kmean_ref[...].astype(BF16)
    sels = [_moba_select(_dot_nt(qx, km), cur) for qx in qs]
    blk_lane = lax.broadcasted_iota(jnp.int32, sels[0].shape, 1)

    start = pl.multiple_of(cur * BLK, BLK)
    k_own = k_ref[pl.ds(start, BLK), :]
    v_own = v_ref[pl.ds(start, BLK), :]
    qpos = i * QB + lax.broadcasted_iota(jnp.int32, (QB, BLK), 0)
    kpos = cur * BLK + lax.broadcasted_iota(jnp.int32, (QB, BLK), 1)
    carry = []
    for qx in qs:
        s = jnp.where(kpos <= qpos, _dot_nt(qx, k_own) * scale, NEG)
        m = jnp.max(s, axis=1, keepdims=True)
        p = jnp.exp(s - m)
        carry += [m, jnp.sum(p, axis=1, keepdims=True), _dot(p.astype(BF16), v_own)]

    def body(j, carry):
        start = pl.multiple_of(j * BLK, BLK)
        k_j = k_ref[pl.ds(start, BLK), :]
        v_j = v_ref[pl.ds(start, BLK), :]
        new = []
        for x, qx in enumerate(qs):
            m, l, acc = carry[3 * x:3 * x + 3]
            chosen = jnp.sum(jnp.where(blk_lane == j, sels[x], 0.0), axis=1, keepdims=True) > 0.0
            s = jnp.where(chosen, _dot_nt(qx, k_j) * scale, NEG)
            m_new = jnp.maximum(m, jnp.max(s, axis=1, keepdims=True))
            alpha = jnp.exp(m - m_new)
            p = jnp.exp(s - m_new)
            new += [m_new, alpha * l + jnp.sum(p, axis=1, keepdims=True),
                    alpha * acc + _dot(p.astype(BF16), v_j)]
        return tuple(new)

    m_a, l_a, acc_a, m_b, l_b, acc_b = lax.fori_loop(0, cur, body, tuple(carry))
    out_ref[...] = _merge_pair(acc_a / l_a, acc_b / l_b).astype(BF16)


def _moba_attn(q, k, v, kmean):
    S, D = q.shape
    nb = S // MOBA_BLOCK
    pairs = D // LANES
    return pl.pallas_call(
        _moba_attn_kernel,
        grid=(pairs, S // Q_BLOCK),
        in_specs=[pl.BlockSpec((Q_BLOCK, LANES), lambda p, i: (i, p)),
                  pl.BlockSpec((S, LANES), lambda p, i: (0, p)),
                  pl.BlockSpec((S, LANES), lambda p, i: (0, p)),
                  pl.BlockSpec((nb, LANES), lambda p, i: (0, p))],
        out_specs=pl.BlockSpec((Q_BLOCK, LANES), lambda p, i: (i, p)),
        out_shape=jax.ShapeDtypeStruct((S, D), BF16),
        compiler_params=_cparams("arbitrary", "arbitrary"),
        name="moba_attn",
    )(q, k, v, kmean)


def _sb_attn_kernel(q_ref, k_ref, v_ref, out_ref):
    QB, KT = Q_BLOCK, SB_KEY_TILE
    scale = ATTN_DH ** -0.5
    i = pl.program_id(1)
    diag = (i * QB) // KT
    qs = _split_pair(q_ref[...])
    upper = (lax.broadcasted_iota(jnp.int32, (KT, KT), 0)
             > lax.broadcasted_iota(jnp.int32, (KT, KT), 1)).astype(BF16)

    def tile(j, carry, masked):
        start = pl.multiple_of(j * KT, KT)
        k_j = k_ref[pl.ds(start, KT), :]
        v_j = v_ref[pl.ds(start, KT), :]
        if masked:
            qpos = i * QB + lax.broadcasted_iota(jnp.int32, (QB, KT), 0)
            kpos = j * KT + lax.broadcasted_iota(jnp.int32, (QB, KT), 1)
            strict = kpos < qpos
        new = []
        for x, qx in enumerate(qs):
            run, acc = carry[2 * x:2 * x + 2]
            z = _dot_nt(qx, k_j) * scale
            lf = -(jnp.maximum(z, 0.0) + jnp.log(1.0 + jnp.exp(-jnp.abs(z))))
            if masked:
                lf = jnp.where(strict, lf, 0.0)
            hi = lf.astype(BF16)
            lo = (lf - hi.astype(F32)).astype(BF16)
            after = _dot(hi, upper) + _dot(lo, upper) + run
            a = jnp.exp(z + lf + after)
            if masked:
                a = jnp.where(strict, a, 0.0)
            new += [run + jnp.sum(lf, axis=1, keepdims=True), acc + _dot(a.astype(BF16), v_j)]
        return tuple(new)

    zero = (jnp.zeros((QB, 1), F32), jnp.zeros((QB, LANES), F32))
    carry = tile(diag, zero + zero, True)
    carry = lax.fori_loop(0, diag, lambda t, c: tile(diag - 1 - t, c, False), carry)
    out_ref[...] = _merge_pair(carry[1], carry[3]).astype(BF16)


def _sb_attn(q, k, v):
    S, D = q.shape
    pairs = D // LANES
    return pl.pallas_call(
        _sb_attn_kernel,
        grid=(pairs, S // Q_BLOCK),
        in_specs=[pl.BlockSpec((Q_BLOCK, LANES), lambda p, i: (i, p)),
                  pl.BlockSpec((S, LANES), lambda p, i: (0, p)),
                  pl.BlockSpec((S, LANES), lambda p, i: (0, p))],
        out_specs=pl.BlockSpec((Q_BLOCK, LANES), lambda p, i: (i, p)),
        out_shape=jax.ShapeDtypeStruct((S, D), BF16),
        compiler_params=_cparams("arbitrary", "arbitrary"),
        name="sb_attn",
    )(q, k, v)


def _conv_proj_kernel(h_ref, g_ref, wb_ref, wc_ref, wu_ref, cw_ref, out_ref, tail_ref):
    tm = h_ref.shape[0]

    @pl.when(pl.program_id(0) == 0)
    def _():
        tail_ref[...] = jnp.zeros_like(tail_ref)

    a = _rms(h_ref[...], g_ref[...]).astype(BF16)
    z = _dot(a, wc_ref[...]) * _dot(a, wu_ref[...])
    row = lax.broadcasted_iota(jnp.int32, z.shape, 0)
    prev1 = tail_ref[7:8, :]
    prev2 = tail_ref[6:7, :]
    z1 = jnp.where(row == 0, prev1, pltpu.roll(z, 1, axis=0))
    z2 = jnp.where(row == 0, prev2, jnp.where(row == 1, prev1, pltpu.roll(z, 2, axis=0)))
    y = cw_ref[0:1, :] * z2 + cw_ref[1:2, :] * z1 + cw_ref[2:3, :] * z
    tail_ref[...] = z[tm - 8:, :]
    out_ref[...] = (_dot(a, wb_ref[...]) * y).astype(BF16)


def _conv_mixer_pre(h, gain, w_in, conv_w):
    S, D = h.shape
    assert conv_w.shape[0] == CONV_WIDTH == 3
    wb = w_in[:, :D].astype(BF16)
    wc = w_in[:, D:2 * D].astype(BF16)
    wu = w_in[:, 2 * D:].astype(BF16)
    cw = jnp.pad(conv_w.astype(F32), ((0, 8 - CONV_WIDTH), (0, 0)))
    tm = min(ROW_TILE, S)
    row = pl.BlockSpec((tm, D), lambda i: (i, 0))
    return pl.pallas_call(
        _conv_proj_kernel,
        grid=(S // tm,),
        in_specs=[row, _const_spec((1, D)), _const_spec((D, D)), _const_spec((D, D)), _const_spec((D, D)),
                  _const_spec((8, D))],
        out_specs=row,
        out_shape=jax.ShapeDtypeStruct((S, D), BF16),
        scratch_shapes=[pltpu.VMEM((8, D), F32)],
        compiler_params=_cparams("arbitrary"),
        name="conv_proj",
    )(h, gain.reshape(1, D), wb, wc, wu, cw)


def _post_kernel(*refs, final):
    if final:
        h_ref, x_ref, wo_ref, g_ref, wg_ref, wu_ref, wd_ref, fg_ref, out_ref = refs
    else:
        h_ref, x_ref, wo_ref, g_ref, wg_ref, wu_ref, wd_ref, out_ref = refs
    h1 = h_ref[...] + _dot(x_ref[...], wo_ref[...])
    a = _rms(h1, g_ref[...]).astype(BF16)
    fc = wg_ref.shape[1] // FFN_CHUNKS
    y = h1
    for c in range(FFN_CHUNKS):
        gate = _dot(a, wg_ref[:, c * fc:(c + 1) * fc])
        up = _dot(a, wu_ref[:, c * fc:(c + 1) * fc])
        act = (gate * jax.nn.sigmoid(gate) * up).astype(BF16)
        y = y + _dot(act, wd_ref[c * fc:(c + 1) * fc, :])
    if final:
        y = _rms(y, fg_ref[...])
    out_ref[...] = y


def _post(h, x, w_out, gain, w_gate, w_up, w_down, final_gain=None):
    S, D = h.shape
    K = x.shape[1]
    F = w_gate.shape[1]
    assert F % (FFN_CHUNKS * LANES) == 0
    final = final_gain is not None
    tm = min(ROW_TILE, S)
    in_specs = [pl.BlockSpec((tm, D), lambda i: (i, 0)), pl.BlockSpec((tm, K), lambda i: (i, 0)),
                _const_spec((K, D)), _const_spec((1, D)), _const_spec((D, F)), _const_spec((D, F)),
                _const_spec((F, D))]
    args = [h, x, w_out.astype(BF16), gain.reshape(1, D), w_gate.astype(BF16), w_up.astype(BF16),
            w_down.astype(BF16)]
    if final:
        in_specs.append(_const_spec((1, D)))
        args.append(final_gain.reshape(1, D))
    return pl.pallas_call(
        functools.partial(_post_kernel, final=final),
        grid=(S // tm,),
        in_specs=in_specs,
        out_specs=pl.BlockSpec((tm, D), lambda i: (i, 0)),
        out_shape=jax.ShapeDtypeStruct((S, D), F32),
        compiler_params=_cparams("arbitrary"),
        name="post_final" if final else "post",
    )(*args)


def _mixer(kind, j, h, gain, positions, p):
    if kind == 0:
        q, k, v, o, gates = _mlstm_proj(h, gain, p["mlstm_w_in"][j])
        return _mlstm_core(q, k, v, o, gates, p["mlstm_b_gate"][j], p["mlstm_head_gain"][j]), p["mlstm_w_out"][j]
    if kind == 1:
        q, k, v, kmean = _qkv_proj(h, gain, p["moba_w_qkv"][j], positions)
        return _moba_attn(q, k, v, kmean), p["moba_w_out"][j]
    if kind == 2:
        return _conv_mixer_pre(h, gain, p["conv_w_in"][j], p["conv_w"][j]), p["conv_w_out"][j]
    q, k, v = _qkv_proj(h, gain, p["sb_w_qkv"][j])
    return _sb_attn(q, k, v), p["sb_w_out"][j]


def kernel(x, positions, norm_gains, mlstm_w_in, mlstm_b_gate, mlstm_head_gain, mlstm_w_out, moba_w_qkv, moba_w_out, conv_w_in, conv_w, conv_w_out, sb_w_qkv, sb_w_out, ffn_w_gate, ffn_w_up, ffn_w_down, final_gain):
    B, S, D = x.shape
    assert D == D_MODEL
    depth = norm_gains.shape[0]
    p = dict(mlstm_w_in=mlstm_w_in, mlstm_b_gate=mlstm_b_gate, mlstm_head_gain=mlstm_head_gain,
             mlstm_w_out=mlstm_w_out, moba_w_qkv=moba_w_qkv, moba_w_out=moba_w_out,
             conv_w_in=conv_w_in, conv_w=conv_w, conv_w_out=conv_w_out,
             sb_w_qkv=sb_w_qkv, sb_w_out=sb_w_out)
    outs = []
    for b in range(B):
        h = x[b]
        for layer in range(depth):
            kind, j = layer % 4, layer // 4
            mix, w_out = _mixer(kind, j, h, norm_gains[layer, 0], positions[b], p)
            h = _post(h, mix, w_out, norm_gains[layer, 1], ffn_w_gate[layer], ffn_w_up[layer],
                      ffn_w_down[layer], final_gain if layer == depth - 1 else None)
        outs.append(h)
    return jnp.stack(outs)
```

```python
import functools

import jax
import jax.numpy as jnp
from jax import lax
from jax.experimental import pallas as pl
from jax.experimental.pallas import tpu as pltpu

F32 = jnp.float32
BF16 = jnp.bfloat16

EPS = 1e-6
NEG = -1e30
D_MODEL = 1024
MLSTM_HEADS = 4
MLSTM_DQK = 128
MLSTM_DV = 256
MLSTM_CHUNK = 128
ATTN_HEADS = 16
ATTN_DH = 64
ROPE_DIMS = 16
ROPE_THETA = 500000.0
MOBA_BLOCK = 256
MOBA_TOPK = 3
Q_BLOCK = 128
ATTN_TILE = 256
LOG2E = 1.4426950408889634
SB_RUN_CUTOFF = 150.0
CONV_WIDTH = 3

LANES = 128
ROW_TILE = 512
FFN_CHUNKS = 2
VMEM_LIMIT = 56 * 1024 * 1024


def _cparams(*sem):
    return pltpu.CompilerParams(dimension_semantics=sem, vmem_limit_bytes=VMEM_LIMIT)


def _const_spec(shape):
    nd = len(shape)
    return pl.BlockSpec(shape, lambda *_: (0,) * nd, pipeline_mode=pl.Buffered(1))


def _rms(x, g):
    return x * lax.rsqrt(jnp.mean(x * x, axis=-1, keepdims=True) + EPS) * g


def _dot(a, b):
    return jnp.dot(a, b, preferred_element_type=F32)


def _dot_nt(a, b):
    return lax.dot_general(a, b, (((1,), (1,)), ((), ())), preferred_element_type=F32)


def _dot_tn(a, b):
    return lax.dot_general(a, b, (((0,), (0,)), ((), ())), preferred_element_type=F32)


def _mlstm_proj_kernel(h_ref, g_ref, wq_ref, wk_ref, wv_ref, wo_ref, wg_ref,
                       q_ref, k_ref, v_ref, o_ref, gate_ref):
    a = _rms(h_ref[...], g_ref[...]).astype(BF16)
    q_ref[...] = (_dot(a, wq_ref[...]) * (MLSTM_DQK ** -0.5)).astype(BF16)
    k_ref[...] = _dot(a, wk_ref[...]).astype(BF16)
    v_ref[...] = _dot(a, wv_ref[...]).astype(BF16)
    o_ref[...] = _dot(a, wo_ref[...])
    gate_ref[...] = _dot(a, wg_ref[...])


def _mlstm_proj(h, gain, w_in):
    S, D = h.shape
    NH, DK, DV = MLSTM_HEADS, MLSTM_DQK, MLSTM_DV
    nq, nv = NH * DK, NH * DV
    wq = w_in[:, :nq].astype(BF16)
    wk = w_in[:, nq:2 * nq].astype(BF16)
    wv = w_in[:, 2 * nq:2 * nq + nv].astype(BF16)
    wo = w_in[:, 2 * nq + nv:2 * nq + 2 * nv].astype(BF16)
    wg = jnp.pad(w_in[:, 2 * nq + 2 * nv:], ((0, 0), (0, LANES - 2 * NH))).astype(BF16)
    tm = min(ROW_TILE, S)
    row = lambda n: pl.BlockSpec((tm, n), lambda i: (i, 0))
    return pl.pallas_call(
        _mlstm_proj_kernel,
        grid=(S // tm,),
        in_specs=[row(D), _const_spec((1, D)), _const_spec((D, nq)), _const_spec((D, nq)),
                  _const_spec((D, nv)), _const_spec((D, nv)), _const_spec((D, LANES))],
        out_specs=[row(nq), row(nq), row(nv), row(nv), row(LANES)],
        out_shape=[jax.ShapeDtypeStruct((S, nq), BF16), jax.ShapeDtypeStruct((S, nq), BF16),
                   jax.ShapeDtypeStruct((S, nv), BF16), jax.ShapeDtypeStruct((S, nv), F32),
                   jax.ShapeDtypeStruct((S, LANES), F32)],
        compiler_params=_cparams("arbitrary"),
        name="mlstm_proj",
    )(h, gain.reshape(1, D), wq, wk, wv, wo, wg)


def _mlstm_core_kernel(q_ref, k_ref, v_ref, o_ref, gate_ref, bg_ref, hg_ref, out_ref,
                       c_ref, n_ref, m_ref):
    NH, DK, DV, L = MLSTM_HEADS, MLSTM_DQK, MLSTM_DV, MLSTM_CHUNK

    @pl.when(pl.program_id(0) == 0)
    def _():
        c_ref[...] = jnp.zeros_like(c_ref)
        n_ref[...] = jnp.zeros_like(n_ref)
        m_ref[...] = jnp.zeros_like(m_ref)

    g = gate_ref[...] + bg_ref[...]
    lane = lax.broadcasted_iota(jnp.int32, (L, LANES), 1)
    log_sig = jnp.minimum(g, 0.0) - jnp.log(1.0 + jnp.exp(-jnp.abs(g)))
    gl = jnp.where(lane >= NH, log_sig, g)
    gl_t = gl.T
    t_idx = lax.broadcasted_iota(jnp.int32, (L, L), 0)
    s_idx = lax.broadcasted_iota(jnp.int32, (L, L), 1)
    causal = s_idx <= t_idx

    for hd in range(NH):
        i_col = gl[:, hd:hd + 1]
        f_col = gl[:, NH + hd:NH + hd + 1]
        i_row = gl_t[hd:hd + 1, :]
        f_row = gl_t[NH + hd:NH + hd + 1, :]
        b_col = jnp.sum(jnp.where(causal, f_row, 0.0), axis=1, keepdims=True)
        b_row = jnp.sum(jnp.where(t_idx <= s_idx, f_col, 0.0), axis=0, keepdims=True)
        m_prev = m_ref[hd]
        dmat = jnp.where(causal, b_col - b_row + i_row, -jnp.inf)
        inter = b_col + m_prev
        m_t = jnp.maximum(inter, jnp.max(dmat, axis=1, keepdims=True))
        w_intra = jnp.exp(dmat - m_t)
        w_inter = jnp.exp(inter - m_t)

        q = q_ref[:, hd * DK:(hd + 1) * DK]
        k = k_ref[:, hd * DK:(hd + 1) * DK]
        v = v_ref[:, hd * DV:(hd + 1) * DV]
        c_old = c_ref[hd]
        n_old = n_ref[hd]
        s = _dot_nt(q, k) * w_intra
        num = _dot(s.astype(BF16), v) + w_inter * _dot(q, c_old.astype(BF16))
        qn = jnp.sum(q.astype(F32) * n_old, axis=1, keepdims=True)
        den = jnp.sum(s, axis=1, keepdims=True) + w_inter * qn
        den = jnp.maximum(jnp.abs(den), jnp.exp(-m_t))
        h_out = num / den

        g_tot = b_col[L - 1:L, :]
        a_col = g_tot - b_col + i_col
        m_new = jnp.maximum(g_tot + m_prev, jnp.max(a_col, axis=0, keepdims=True))
        decay = jnp.exp(g_tot + m_prev - m_new)
        wa = jnp.exp(a_col - m_new)
        kw = k.astype(F32) * wa
        c_ref[hd] = decay * c_old + _dot_tn(kw.astype(BF16), v)
        n_ref[hd] = decay * n_old + jnp.sum(kw, axis=0, keepdims=True)
        m_ref[hd] = m_new

        hn = h_out * lax.rsqrt(jnp.mean(h_out * h_out, axis=1, keepdims=True) + EPS)
        hn = hn * hg_ref[:, hd * DV:(hd + 1) * DV]
        out = hn * jax.nn.sigmoid(o_ref[:, hd * DV:(hd + 1) * DV])
        out_ref[:, hd * DV:(hd + 1) * DV] = out.astype(BF16)


def _mlstm_core(q, k, v, o, gates, b_gate, head_gain):
    S = q.shape[0]
    NH, DK, DV, L = MLSTM_HEADS, MLSTM_DQK, MLSTM_DV, MLSTM_CHUNK
    bg = jnp.pad(b_gate.astype(F32), (0, LANES - 2 * NH)).reshape(1, LANES)
    row = lambda n: pl.BlockSpec((L, n), lambda c: (c, 0))
    return pl.pallas_call(
        _mlstm_core_kernel,
        grid=(S // L,),
        in_specs=[row(NH * DK), row(NH * DK), row(NH * DV), row(NH * DV), row(LANES),
                  _const_spec((1, LANES)), _const_spec((1, NH * DV))],
        out_specs=row(NH * DV),
        out_shape=jax.ShapeDtypeStruct((S, NH * DV), BF16),
        scratch_shapes=[pltpu.VMEM((NH, DK, DV), F32), pltpu.VMEM((NH, 1, DK), F32),
                        pltpu.VMEM((NH, 1, 1), F32)],
        compiler_params=_cparams("arbitrary"),
        name="mlstm_core",
    )(q, k, v, o, gates, bg, head_gain.astype(F32).reshape(1, NH * DV))


def _rope_tile(x, cos, sin_lo, sin_hi):
    half = ROPE_DIMS // 2
    cols = []
    for c in range(x.shape[1] // LANES):
        xc = x[:, c * LANES:(c + 1) * LANES]
        up = pltpu.roll(xc, LANES - half, axis=1)
        down = pltpu.roll(xc, half, axis=1)
        cols.append(xc * cos + up * sin_lo + down * sin_hi)
    return jnp.concatenate(cols, axis=1)


def _qkv_proj_kernel(*refs, rope):
    if rope:
        (h_ref, g_ref, wq_ref, wk_ref, wv_ref, pos_ref, inv_ref,
         q_ref, k_ref, v_ref, kmean_ref) = refs
    else:
        h_ref, g_ref, wq_ref, wk_ref, wv_ref, q_ref, k_ref, v_ref = refs
    a = _rms(h_ref[...], g_ref[...]).astype(BF16)
    q = _dot(a, wq_ref[...])
    k = _dot(a, wk_ref[...])
    v_ref[...] = _dot(a, wv_ref[...]).astype(BF16)
    if rope:
        tm = q.shape[0]
        half = ROPE_DIMS // 2
        ang = pos_ref[...].astype(F32) * inv_ref[...]
        cos = jnp.cos(ang)
        sin = jnp.sin(ang)
        dim = lax.broadcasted_iota(jnp.int32, (tm, LANES), 1) % ATTN_DH
        sin_lo = jnp.where(dim < half, -sin, 0.0)
        sin_hi = jnp.where((dim >= half) & (dim < ROPE_DIMS), sin, 0.0)
        q = _rope_tile(q, cos, sin_lo, sin_hi)
        k = _rope_tile(k, cos, sin_lo, sin_hi)
        nblk = tm // MOBA_BLOCK
        kmean_ref[0] = jnp.sum(k.reshape(nblk, MOBA_BLOCK, k.shape[1]), axis=1) * (1.0 / MOBA_BLOCK)
    q_ref[...] = q.astype(BF16)
    k_ref[...] = k.astype(BF16)


def _qkv_proj(h, gain, w_qkv, positions=None):
    S, D = h.shape
    rope = positions is not None
    wq = w_qkv[:, :D].astype(BF16)
    wk = w_qkv[:, D:2 * D].astype(BF16)
    wv = w_qkv[:, 2 * D:].astype(BF16)
    tm = min(ROW_TILE, S)
    row = lambda n: pl.BlockSpec((tm, n), lambda i: (i, 0))
    in_specs = [row(D), _const_spec((1, D)), _const_spec((D, D)), _const_spec((D, D)), _const_spec((D, D))]
    args = [h, gain.reshape(1, D), wq, wk, wv]
    out_specs = [row(D), row(D), row(D)]
    out_shape = [jax.ShapeDtypeStruct((S, D), BF16)] * 3
    if rope:
        assert S % MOBA_BLOCK == 0 and tm % MOBA_BLOCK == 0
        half = ROPE_DIMS // 2
        inv = ROPE_THETA ** (-jnp.arange(half, dtype=F32) / half)
        dim = jnp.arange(LANES) % ATTN_DH
        inv_lane = jnp.where(dim < ROPE_DIMS, inv[dim % half], 0.0).astype(F32).reshape(1, LANES)
        in_specs += [row(1), _const_spec((1, LANES))]
        args += [positions.reshape(S, 1), inv_lane]
        nblk = tm // MOBA_BLOCK
        out_specs.append(pl.BlockSpec((1, nblk, D), lambda i: (i, 0, 0)))
        out_shape.append(jax.ShapeDtypeStruct((S // tm, nblk, D), F32))
    outs = pl.pallas_call(
        functools.partial(_qkv_proj_kernel, rope=rope),
        grid=(S // tm,),
        in_specs=in_specs,
        out_specs=out_specs,
        out_shape=out_shape,
        compiler_params=_cparams("arbitrary"),
        name="moba_proj" if rope else "sb_proj",
    )(*args)
    if rope:
        q, k, v, kmean = outs
        return q, k, v, kmean.reshape(S // MOBA_BLOCK, D)
    return outs


def _split_pair(q):
    lane = lax.broadcasted_iota(jnp.int32, q.shape, 1)
    zero = jnp.zeros_like(q)
    return jnp.where(lane < ATTN_DH, q, zero), jnp.where(lane >= ATTN_DH, q, zero)


def _merge_pair(acc_a, acc_b):
    lane = lax.broadcasted_iota(jnp.int32, acc_a.shape, 1)
    return jnp.where(lane < ATTN_DH, acc_a, acc_b)


def _moba_select(gate, cur):
    nb = gate.shape[1]
    blk = lax.broadcasted_iota(jnp.int32, gate.shape, 1)
    valid = blk < cur
    g = jnp.where(valid, gate, -jnp.inf)
    sel = jnp.zeros(gate.shape, F32)
    for _ in range(MOBA_TOPK):
        mx = jnp.max(g, axis=1, keepdims=True)
        first = jnp.min(jnp.where(g == mx, blk, nb), axis=1, keepdims=True)
        hit = (blk == first) & valid
        sel = jnp.where(hit, 1.0, sel)
        g = jnp.where(blk == first, -jnp.inf, g)
    return sel


def _moba_attn_kernel(q_ref, k_ref, v_ref, kmean_ref, out_ref):
    T = ATTN_TILE
    c2 = ATTN_DH ** -0.5 * LOG2E
    cur = pl.program_id(1)
    qs = _split_pair(q_ref[...])
    km = kmean_ref[...].astype(BF16)
    sels = [_moba_select(_dot_nt(qx, km), cur) for qx in qs]
    blk_lane = lax.broadcasted_iota(jnp.int32, sels[0].shape, 1)

    start = pl.multiple_of(cur * T, T)
    k_own = k_ref[pl.ds(start, T), :]
    v_own = v_ref[pl.ds(start, T), :]
    causal = (lax.broadcasted_iota(jnp.int32, (T, T), 1) <= lax.broadcasted_iota(jnp.int32, (T, T), 0))
    carry = []
    for qx in qs:
        s = jnp.where(causal, _dot_nt(qx, k_own) * c2, NEG)
        m = jnp.max(s, axis=1, keepdims=True)
        p = jnp.exp2(s - m)
        carry += [m, jnp.sum(p, axis=1, keepdims=True), _dot(p.astype(BF16), v_own)]

    def body(j, carry):
        start = pl.multiple_of(j * T, T)
        k_j = k_ref[pl.ds(start, T), :]
        v_j = v_ref[pl.ds(start, T), :]
        new = []
        for x, qx in enumerate(qs):
            m, l, acc = carry[3 * x:3 * x + 3]
            chosen = jnp.sum(jnp.where(blk_lane == j, sels[x], 0.0), axis=1, keepdims=True) > 0.0
            s = jnp.where(chosen, _dot_nt(qx, k_j) * c2, NEG)
            m_new = jnp.maximum(m, jnp.max(s, axis=1, keepdims=True))
            alpha = jnp.exp2(m - m_new)
            p = jnp.exp2(s - m_new)
            new += [m_new, alpha * l + jnp.sum(p, axis=1, keepdims=True),
                    alpha * acc + _dot(p.astype(BF16), v_j)]
        return tuple(new)

    m_a, l_a, acc_a, m_b, l_b, acc_b = lax.fori_loop(0, cur, body, tuple(carry))
    out_ref[...] = _merge_pair(acc_a / l_a, acc_b / l_b).astype(BF16)


def _attn_specs(S):
    tile = pl.BlockSpec((ATTN_TILE, LANES), lambda p, i: (i, p))
    resident = pl.BlockSpec((S, LANES), lambda p, i: (0, p))
    return tile, resident


def _moba_attn(q, k, v, kmean):
    S, D = q.shape
    assert MOBA_BLOCK == ATTN_TILE and ATTN_TILE % Q_BLOCK == 0 and S % ATTN_TILE == 0
    nb = S // MOBA_BLOCK
    tile, resident = _attn_specs(S)
    return pl.pallas_call(
        _moba_attn_kernel,
        grid=(D // LANES, S // ATTN_TILE),
        in_specs=[tile, resident, resident, pl.BlockSpec((nb, LANES), lambda p, i: (0, p))],
        out_specs=tile,
        out_shape=jax.ShapeDtypeStruct((S, D), BF16),
        compiler_params=_cparams("arbitrary", "arbitrary"),
        name="moba_attn",
    )(q, k, v, kmean)


def _sb_attn_kernel(q_ref, k_ref, v_ref, out_ref):
    T = ATTN_TILE
    c2 = ATTN_DH ** -0.5 * LOG2E
    i = pl.program_id(1)
    qs = _split_pair(q_ref[...])
    key_row = lax.broadcasted_iota(jnp.int32, (T, T), 0)
    key_col = lax.broadcasted_iota(jnp.int32, (T, T), 1)
    later_keys = (key_row > key_col).astype(BF16)
    strict = key_col < key_row

    def tile(j, carry, masked):
        start = pl.multiple_of(j * T, T)
        k_j = k_ref[pl.ds(start, T), :]
        v_j = v_ref[pl.ds(start, T), :]
        new = []
        for x, qx in enumerate(qs):
            run, acc = carry[2 * x:2 * x + 2]
            z2 = _dot_nt(qx, k_j) * c2
            sp = jnp.maximum(z2, 0.0) + jnp.log2(1.0 + jnp.exp2(-jnp.abs(z2)))
            if masked:
                sp = jnp.where(strict, sp, 0.0)
            hi = sp.astype(BF16)
            lo = (sp - hi.astype(F32)).astype(BF16)
            later = _dot(hi, later_keys) + _dot(lo, later_keys) + run
            a = jnp.exp2(z2 - sp - later)
            if masked:
                a = jnp.where(strict, a, 0.0)
            new += [run + jnp.sum(sp, axis=1, keepdims=True), acc + _dot(a.astype(BF16), v_j)]
        return tuple(new)

    def least_run(carry):
        return jnp.min(jnp.minimum(carry[0], carry[2]))

    def more(state):
        j, least = state[0], state[1]
        return jnp.logical_and(j >= 0, least < SB_RUN_CUTOFF)

    def step(state):
        carry = tile(state[0], state[2:], False)
        return (state[0] - 1, least_run(carry)) + carry

    zero = (jnp.zeros((T, 1), F32), jnp.zeros((T, LANES), F32))
    carry = tile(i, zero + zero, True)
    state = lax.while_loop(more, step, (i - 1, least_run(carry)) + carry)
    out_ref[...] = _merge_pair(state[3], state[5]).astype(BF16)


def _sb_attn(q, k, v):
    S, D = q.shape
    assert ATTN_TILE % Q_BLOCK == 0 and S % ATTN_TILE == 0
    tile, resident = _attn_specs(S)
    return pl.pallas_call(
        _sb_attn_kernel,
        grid=(D // LANES, S // ATTN_TILE),
        in_specs=[tile, resident, resident],
        out_specs=tile,
        out_shape=jax.ShapeDtypeStruct((S, D), BF16),
        compiler_params=_cparams("arbitrary", "arbitrary"),
        name="sb_attn",
    )(q, k, v)


def _conv_proj_kernel(h_ref, g_ref, wb_ref, wc_ref, wu_ref, cw_ref, out_ref, tail_ref):
    tm = h_ref.shape[0]

    @pl.when(pl.program_id(0) == 0)
    def _():
        tail_ref[...] = jnp.zeros_like(tail_ref)

    a = _rms(h_ref[...], g_ref[...]).astype(BF16)
    z = _dot(a, wc_ref[...]) * _dot(a, wu_ref[...])
    row = lax.broadcasted_iota(jnp.int32, z.shape, 0)
    prev1 = tail_ref[7:8, :]
    prev2 = tail_ref[6:7, :]
    z1 = jnp.where(row == 0, prev1, pltpu.roll(z, 1, axis=0))
    z2 = jnp.where(row == 0, prev2, jnp.where(row == 1, prev1, pltpu.roll(z, 2, axis=0)))
    y = cw_ref[0:1, :] * z2 + cw_ref[1:2, :] * z1 + cw_ref[2:3, :] * z
    tail_ref[...] = z[tm - 8:, :]
    out_ref[...] = (_dot(a, wb_ref[...]) * y).astype(BF16)


def _conv_mixer_pre(h, gain, w_in, conv_w):
    S, D = h.shape
    assert conv_w.shape[0] == CONV_WIDTH == 3
    wb = w_in[:, :D].astype(BF16)
    wc = w_in[:, D:2 * D].astype(BF16)
    wu = w_in[:, 2 * D:].astype(BF16)
    cw = jnp.pad(conv_w.astype(F32), ((0, 8 - CONV_WIDTH), (0, 0)))
    tm = min(ROW_TILE, S)
    row = pl.BlockSpec((tm, D), lambda i: (i, 0))
    return pl.pallas_call(
        _conv_proj_kernel,
        grid=(S // tm,),
        in_specs=[row, _const_spec((1, D)), _const_spec((D, D)), _const_spec((D, D)), _const_spec((D, D)),
                  _const_spec((8, D))],
        out_specs=row,
        out_shape=jax.ShapeDtypeStruct((S, D), BF16),
        scratch_shapes=[pltpu.VMEM((8, D), F32)],
        compiler_params=_cparams("arbitrary"),
        name="conv_proj",
    )(h, gain.reshape(1, D), wb, wc, wu, cw)


def _post_kernel(*refs, final):
    if final:
        h_ref, x_ref, wo_ref, g_ref, wg_ref, wu_ref, wd_ref, fg_ref, out_ref = refs
    else:
        h_ref, x_ref, wo_ref, g_ref, wg_ref, wu_ref, wd_ref, out_ref = refs
    h1 = h_ref[...] + _dot(x_ref[...], wo_ref[...])
    a = _rms(h1, g_ref[...]).astype(BF16)
    fc = wg_ref.shape[1] // FFN_CHUNKS
    y = h1
    for c in range(FFN_CHUNKS):
        gate = _dot(a, wg_ref[:, c * fc:(c + 1) * fc])
        up = _dot(a, wu_ref[:, c * fc:(c + 1) * fc])
        act = (gate * jax.nn.sigmoid(gate) * up).astype(BF16)
        y = y + _dot(act, wd_ref[c * fc:(c + 1) * fc, :])
    if final:
        y = _rms(y, fg_ref[...])
    out_ref[...] = y


def _post(h, x, w_out, gain, w_gate, w_up, w_down, final_gain=None):
    S, D = h.shape
    K = x.shape[1]
    F = w_gate.shape[1]
    assert F % (FFN_CHUNKS * LANES) == 0
    final = final_gain is not None
    tm = min(ROW_TILE, S)
    in_specs = [pl.BlockSpec((tm, D), lambda i: (i, 0)), pl.BlockSpec((tm, K), lambda i: (i, 0)),
                _const_spec((K, D)), _const_spec((1, D)), _const_spec((D, F)), _const_spec((D, F)),
                _const_spec((F, D))]
    args = [h, x, w_out.astype(BF16), gain.reshape(1, D), w_gate.astype(BF16), w_up.astype(BF16),
            w_down.astype(BF16)]
    if final:
        in_specs.append(_const_spec((1, D)))
        args.append(final_gain.reshape(1, D))
    return pl.pallas_call(
        functools.partial(_post_kernel, final=final),
        grid=(S // tm,),
        in_specs=in_specs,
        out_specs=pl.BlockSpec((tm, D), lambda i: (i, 0)),
        out_shape=jax.ShapeDtypeStruct((S, D), F32),
        compiler_params=_cparams("arbitrary"),
        name="post_final" if final else "post",
    )(*args)


def _mixer(kind, j, h, gain, positions, p):
    if kind == 0:
        q, k, v, o, gates = _mlstm_proj(h, gain, p["mlstm_w_in"][j])
        return _mlstm_core(q, k, v, o, gates, p["mlstm_b_gate"][j], p["mlstm_head_gain"][j]), p["mlstm_w_out"][j]
    if kind == 1:
        q, k, v, kmean = _qkv_proj(h, gain, p["moba_w_qkv"][j], positions)
        return _moba_attn(q, k, v, kmean), p["moba_w_out"][j]
    if kind == 2:
        return _conv_mixer_pre(h, gain, p["conv_w_in"][j], p["conv_w"][j]), p["conv_w_out"][j]
    q, k, v = _qkv_proj(h, gain, p["sb_w_qkv"][j])
    return _sb_attn(q, k, v), p["sb_w_out"][j]


def kernel(x, positions, norm_gains, mlstm_w_in, mlstm_b_gate, mlstm_head_gain, mlstm_w_out, moba_w_qkv, moba_w_out, conv_w_in, conv_w, conv_w_out, sb_w_qkv, sb_w_out, ffn_w_gate, ffn_w_up, ffn_w_down, final_gain):
    B, S, D = x.shape
    assert D == D_MODEL
    depth = norm_gains.shape[0]
    p = dict(mlstm_w_in=mlstm_w_in, mlstm_b_gate=mlstm_b_gate, mlstm_head_gain=mlstm_head_gain,
             mlstm_w_out=mlstm_w_out, moba_w_qkv=moba_w_qkv, moba_w_out=moba_w_out,
             conv_w_in=conv_w_in, conv_w=conv_w, conv_w_out=conv_w_out,
             sb_w_qkv=sb_w_qkv, sb_w_out=sb_w_out)
    outs = []
    for b in range(B):
        h = x[b]
        for layer in range(depth):
            kind, j = layer % 4, layer // 4
            mix, w_out = _mixer(kind, j, h, norm_gains[layer, 0], positions[b], p)
            h = _post(h, mix, w_out, norm_gains[layer, 1], ffn_w_gate[layer], ffn_w_up[layer],
                      ffn_w_down[layer], final_gain if layer == depth - 1 else None)
        outs.append(h)
    return jnp.stack(outs)
```

```python
import functools

import jax
import jax.numpy as jnp
from jax import lax
from jax.experimental import pallas as pl
from jax.experimental.pallas import tpu as pltpu

F32 = jnp.float32
BF16 = jnp.bfloat16

EPS = 1e-6
NEG = -1e30
D_MODEL = 1024
MLSTM_HEADS = 4
MLSTM_DQK = 128
MLSTM_DV = 256
MLSTM_CHUNK = 128
ATTN_HEADS = 16
ATTN_DH = 64
ROPE_DIMS = 16
ROPE_THETA = 500000.0
MOBA_BLOCK = 256
MOBA_TOPK = 3
Q_BLOCK = 128
ATTN_TILE = 256
LOG2E = 1.4426950408889634
SB_RUN_CUTOFF = 150.0
CONV_WIDTH = 3

LANES = 128
ROW_TILE = 512
FFN_CHUNKS = 2
VMEM_LIMIT = 56 * 1024 * 1024


def _cparams(*sem):
    return pltpu.CompilerParams(dimension_semantics=sem, vmem_limit_bytes=VMEM_LIMIT)


def _const_spec(shape):
    nd = len(shape)
    return pl.BlockSpec(shape, lambda *_: (0,) * nd, pipeline_mode=pl.Buffered(1))


def _rms(x, g):
    return x * lax.rsqrt(jnp.mean(x * x, axis=-1, keepdims=True) + EPS) * g


def _dot(a, b):
    return jnp.dot(a, b, preferred_element_type=F32)


def _dot_nt(a, b):
    return lax.dot_general(a, b, (((1,), (1,)), ((), ())), preferred_element_type=F32)


def _dot_tn(a, b):
    return lax.dot_general(a, b, (((0,), (0,)), ((), ())), preferred_element_type=F32)


def _mlstm_proj_kernel(h_ref, g_ref, wq_ref, wk_ref, wv_ref, wo_ref, wg_ref,
                       q_ref, k_ref, v_ref, o_ref, gate_ref):
    a = _rms(h_ref[...], g_ref[...]).astype(BF16)
    q_ref[...] = (_dot(a, wq_ref[...]) * (MLSTM_DQK ** -0.5)).astype(BF16)
    k_ref[...] = _dot(a, wk_ref[...]).astype(BF16)
    v_ref[...] = _dot(a, wv_ref[...]).astype(BF16)
    o_ref[...] = _dot(a, wo_ref[...])
    gate_ref[...] = _dot(a, wg_ref[...])


def _mlstm_proj(h, gain, w_in):
    S, D = h.shape
    NH, DK, DV = MLSTM_HEADS, MLSTM_DQK, MLSTM_DV
    nq, nv = NH * DK, NH * DV
    wq = w_in[:, :nq].astype(BF16)
    wk = w_in[:, nq:2 * nq].astype(BF16)
    wv = w_in[:, 2 * nq:2 * nq + nv].astype(BF16)
    wo = w_in[:, 2 * nq + nv:2 * nq + 2 * nv].astype(BF16)
    wg = jnp.pad(w_in[:, 2 * nq + 2 * nv:], ((0, 0), (0, LANES - 2 * NH))).astype(BF16)
    tm = min(ROW_TILE, S)
    row = lambda n: pl.BlockSpec((tm, n), lambda i: (i, 0))
    return pl.pallas_call(
        _mlstm_proj_kernel,
        grid=(S // tm,),
        in_specs=[row(D), _const_spec((1, D)), _const_spec((D, nq)), _const_spec((D, nq)),
                  _const_spec((D, nv)), _const_spec((D, nv)), _const_spec((D, LANES))],
        out_specs=[row(nq), row(nq), row(nv), row(nv), row(LANES)],
        out_shape=[jax.ShapeDtypeStruct((S, nq), BF16), jax.ShapeDtypeStruct((S, nq), BF16),
                   jax.ShapeDtypeStruct((S, nv), BF16), jax.ShapeDtypeStruct((S, nv), F32),
                   jax.ShapeDtypeStruct((S, LANES), F32)],
        compiler_params=_cparams("arbitrary"),
        name="mlstm_proj",
    )(h, gain.reshape(1, D), wq, wk, wv, wo, wg)


def _mlstm_core_kernel(q_ref, k_ref, v_ref, o_ref, gate_ref, bg_ref, hg_ref, out_ref,
                       c_ref, n_ref, m_ref):
    NH, DK, DV, L = MLSTM_HEADS, MLSTM_DQK, MLSTM_DV, MLSTM_CHUNK

    @pl.when(pl.program_id(0) == 0)
    def _():
        c_ref[...] = jnp.zeros_like(c_ref)
        n_ref[...] = jnp.zeros_like(n_ref)
        m_ref[...] = jnp.zeros_like(m_ref)

    g = gate_ref[...] + bg_ref[...]
    lane = lax.broadcasted_iota(jnp.int32, (L, LANES), 1)
    log_sig = jnp.minimum(g, 0.0) - jnp.log(1.0 + jnp.exp(-jnp.abs(g)))
    gl = jnp.where(lane >= NH, log_sig, g)
    gl_t = gl.T
    t_idx = lax.broadcasted_iota(jnp.int32, (L, L), 0)
    s_idx = lax.broadcasted_iota(jnp.int32, (L, L), 1)
    causal = s_idx <= t_idx

    for hd in range(NH):
        i_col = gl[:, hd:hd + 1]
        f_col = gl[:, NH + hd:NH + hd + 1]
        i_row = gl_t[hd:hd + 1, :]
        f_row = gl_t[NH + hd:NH + hd + 1, :]
        b_col = jnp.sum(jnp.where(causal, f_row, 0.0), axis=1, keepdims=True)
        b_row = jnp.sum(jnp.where(t_idx <= s_idx, f_col, 0.0), axis=0, keepdims=True)
        m_prev = m_ref[hd]
        dmat = jnp.where(causal, b_col - b_row + i_row, -jnp.inf)
        inter = b_col + m_prev
        m_t = jnp.maximum(inter, jnp.max(dmat, axis=1, keepdims=True))
        w_intra = jnp.exp(dmat - m_t)
        w_inter = jnp.exp(inter - m_t)

        q = q_ref[:, hd * DK:(hd + 1) * DK]
        k = k_ref[:, hd * DK:(hd + 1) * DK]
        v = v_ref[:, hd * DV:(hd + 1) * DV]
        c_old = c_ref[hd]
        n_old = n_ref[hd]
        s = _dot_nt(q, k) * w_intra
        num = _dot(s.astype(BF16), v) + w_inter * _dot(q, c_old.astype(BF16))
        qn = jnp.sum(q.astype(F32) * n_old, axis=1, keepdims=True)
        den = jnp.sum(s, axis=1, keepdims=True) + w_inter * qn
        den = jnp.maximum(jnp.abs(den), jnp.exp(-m_t))
        h_out = num / den

        g_tot = b_col[L - 1:L, :]
        a_col = g_tot - b_col + i_col
        m_new = jnp.maximum(g_tot + m_prev, jnp.max(a_col, axis=0, keepdims=True))
        decay = jnp.exp(g_tot + m_prev - m_new)
        wa = jnp.exp(a_col - m_new)
        kw = k.astype(F32) * wa
        c_ref[hd] = decay * c_old + _dot_tn(kw.astype(BF16), v)
        n_ref[hd] = decay * n_old + jnp.sum(kw, axis=0, keepdims=True)
        m_ref[hd] = m_new

        hn = h_out * lax.rsqrt(jnp.mean(h_out * h_out, axis=1, keepdims=True) + EPS)
        hn = hn * hg_ref[:, hd * DV:(hd + 1) * DV]
        out = hn * jax.nn.sigmoid(o_ref[:, hd * DV:(hd + 1) * DV])
        out_ref[:, hd * DV:(hd + 1) * DV] = out.astype(BF16)


def _mlstm_core(q, k, v, o, gates, b_gate, head_gain):
    S = q.shape[0]
    NH, DK, DV, L = MLSTM_HEADS, MLSTM_DQK, MLSTM_DV, MLSTM_CHUNK
    bg = jnp.pad(b_gate.astype(F32), (0, LANES - 2 * NH)).reshape(1, LANES)
    row = lambda n: pl.BlockSpec((L, n), lambda c: (c, 0))
    return pl.pallas_call(
        _mlstm_core_kernel,
        grid=(S // L,),
        in_specs=[row(NH * DK), row(NH * DK), row(NH * DV), row(NH * DV), row(LANES),
                  _const_spec((1, LANES)), _const_spec((1, NH * DV))],
        out_specs=row(NH * DV),
        out_shape=jax.ShapeDtypeStruct((S, NH * DV), BF16),
        scratch_shapes=[pltpu.VMEM((NH, DK, DV), F32), pltpu.VMEM((NH, 1, DK), F32),
                        pltpu.VMEM((NH, 1, 1), F32)],
        compiler_params=_cparams("arbitrary"),
        name="mlstm_core",
    )(q, k, v, o, gates, bg, head_gain.astype(F32).reshape(1, NH * DV))


def _rope_tile(x, cos, sin_lo, sin_hi):
    half = ROPE_DIMS // 2
    cols = []
    for c in range(x.shape[1] // LANES):
        xc = x[:, c * LANES:(c + 1) * LANES]
        up = pltpu.roll(xc, LANES - half, axis=1)
        down = pltpu.roll(xc, half, axis=1)
        cols.append(xc * cos + up * sin_lo + down * sin_hi)
    return jnp.concatenate(cols, axis=1)


def _qkv_proj_kernel(*refs, rope):
    if rope:
        (h_ref, g_ref, wq_ref, wk_ref, wv_ref, pos_ref, inv_ref,
         q_ref, k_ref, v_ref, kmean_ref) = refs
    else:
        h_ref, g_ref, wq_ref, wk_ref, wv_ref, q_ref, k_ref, v_ref = refs
    a = _rms(h_ref[...], g_ref[...]).astype(BF16)
    q = _dot(a, wq_ref[...])
    k = _dot(a, wk_ref[...])
    v = _dot(a, wv_ref[...])
    if not rope:
        v_ref[...] = v.astype(BF16)
    if rope:
        tm = q.shape[0]
        for b in range(tm // MOBA_BLOCK):
            v_ref[b] = v[b * MOBA_BLOCK:(b + 1) * MOBA_BLOCK, :].T.astype(BF16)
        half = ROPE_DIMS // 2
        ang = pos_ref[...].astype(F32) * inv_ref[...]
        cos = jnp.cos(ang)
        sin = jnp.sin(ang)
        dim = lax.broadcasted_iota(jnp.int32, (tm, LANES), 1) % ATTN_DH
        sin_lo = jnp.where(dim < half, -sin, 0.0)
        sin_hi = jnp.where((dim >= half) & (dim < ROPE_DIMS), sin, 0.0)
        q = _rope_tile(q, cos, sin_lo, sin_hi)
        k = _rope_tile(k, cos, sin_lo, sin_hi)
        nblk = tm // MOBA_BLOCK
        kmean_ref[0] = jnp.sum(k.reshape(nblk, MOBA_BLOCK, k.shape[1]), axis=1) * (1.0 / MOBA_BLOCK)
    q_ref[...] = q.astype(BF16)
    k_ref[...] = k.astype(BF16)


def _qkv_proj(h, gain, w_qkv, positions=None):
    S, D = h.shape
    rope = positions is not None
    wq = w_qkv[:, :D].astype(BF16)
    wk = w_qkv[:, D:2 * D].astype(BF16)
    wv = w_qkv[:, 2 * D:].astype(BF16)
    tm = min(ROW_TILE, S)
    row = lambda n: pl.BlockSpec((tm, n), lambda i: (i, 0))
    in_specs = [row(D), _const_spec((1, D)), _const_spec((D, D)), _const_spec((D, D)), _const_spec((D, D))]
    args = [h, gain.reshape(1, D), wq, wk, wv]
    out_specs = [row(D), row(D), row(D)]
    out_shape = [jax.ShapeDtypeStruct((S, D), BF16)] * 3
    if rope:
        assert S % MOBA_BLOCK == 0 and tm % MOBA_BLOCK == 0
        half = ROPE_DIMS // 2
        inv = ROPE_THETA ** (-jnp.arange(half, dtype=F32) / half)
        dim = jnp.arange(LANES) % ATTN_DH
        inv_lane = jnp.where(dim < ROPE_DIMS, inv[dim % half], 0.0).astype(F32).reshape(1, LANES)
        in_specs += [row(1), _const_spec((1, LANES))]
        args += [positions.reshape(S, 1), inv_lane]
        nblk = tm // MOBA_BLOCK
        out_specs[2] = pl.BlockSpec((nblk, D, MOBA_BLOCK), lambda i: (i, 0, 0))
        out_shape[2] = jax.ShapeDtypeStruct((S // MOBA_BLOCK, D, MOBA_BLOCK), BF16)
        out_specs.append(pl.BlockSpec((1, nblk, D), lambda i: (i, 0, 0)))
        out_shape.append(jax.ShapeDtypeStruct((S // tm, nblk, D), F32))
    outs = pl.pallas_call(
        functools.partial(_qkv_proj_kernel, rope=rope),
        grid=(S // tm,),
        in_specs=in_specs,
        out_specs=out_specs,
        out_shape=out_shape,
        compiler_params=_cparams("arbitrary"),
        name="moba_proj" if rope else "sb_proj",
    )(*args)
    if rope:
        q, k, v, kmean = outs
        return q, k, v, kmean.reshape(S // MOBA_BLOCK, D)
    return outs


def _split_pair(q):
    lane = lax.broadcasted_iota(jnp.int32, q.shape, 1)
    zero = jnp.zeros_like(q)
    return jnp.where(lane < ATTN_DH, q, zero), jnp.where(lane >= ATTN_DH, q, zero)


def _merge_pair(acc_a, acc_b):
    lane = lax.broadcasted_iota(jnp.int32, acc_a.shape, 1)
    return jnp.where(lane < ATTN_DH, acc_a, acc_b)


def _moba_select(gate, cur):
    nb = gate.shape[0]
    blk = lax.broadcasted_iota(jnp.int32, gate.shape, 0)
    valid = blk < cur
    g = jnp.where(valid, gate, -jnp.inf)
    sel = jnp.zeros(gate.shape, F32)
    for _ in range(MOBA_TOPK):
        mx = jnp.max(g, axis=0, keepdims=True)
        first = jnp.min(jnp.where(g == mx, blk, nb), axis=0, keepdims=True)
        hit = (blk == first) & valid
        sel = jnp.where(hit, 1.0, sel)
        g = jnp.where(blk == first, -jnp.inf, g)
    return sel


def _moba_attn_kernel(q_ref, k_ref, vt_ref, kmean_ref, out_ref,
                      sel_ref, s_ref, p_ref, alpha_ref, m_ref, l_ref, acc_ref):
    T = ATTN_TILE
    H = ATTN_DH
    c2 = H ** -0.5 * LOG2E
    cur = pl.program_id(1)
    qt = q_ref[...].astype(F32).T
    dim = lax.broadcasted_iota(jnp.int32, qt.shape, 0)
    qts = [jnp.where(dim < H, qt, 0.0).astype(BF16), jnp.where(dim >= H, qt, 0.0).astype(BF16)]
    km = kmean_ref[...].astype(BF16)
    for x in range(2):
        sel_ref[x] = _moba_select(_dot(km, qts[x]), cur)
    key = lax.broadcasted_iota(jnp.int32, (T, T), 0)
    query = lax.broadcasted_iota(jnp.int32, (T, T), 1)
    causal = key <= query

    def scores(b):
        k_b = k_ref[pl.ds(pl.multiple_of(b * T, T), T), :]
        for x in range(2):
            s_ref[x] = _dot(k_b, qts[x]) * c2

    def fold(b):
        for x in range(2):
            acc_ref[x] = alpha_ref[x] * acc_ref[x] + _dot(vt_ref[b, x * H:(x + 1) * H, :], p_ref[x])

    scores(cur)
    for x in range(2):
        s = jnp.where(causal, s_ref[x], NEG)
        m = jnp.max(s, axis=0, keepdims=True)
        p = jnp.exp2(s - m)
        m_ref[x] = m
        l_ref[x] = jnp.sum(p, axis=0, keepdims=True)
        alpha_ref[x] = jnp.ones_like(m)
        p_ref[x] = p.astype(BF16)
        acc_ref[x] = jnp.zeros((H, T), F32)
    scores(0)

    def trip(j, _):
        fold(jnp.where(j == 1, cur, j - 2))
        for x in range(2):
            chosen = sel_ref[x, pl.ds(j - 1, 1), :] > 0.0
            s = jnp.where(chosen, s_ref[x], NEG)
            m = m_ref[x]
            m_new = jnp.maximum(m, jnp.max(s, axis=0, keepdims=True))
            alpha = jnp.exp2(m - m_new)
            p = jnp.exp2(s - m_new)
            l_ref[x] = alpha * l_ref[x] + jnp.sum(p, axis=0, keepdims=True)
            m_ref[x] = m_new
            alpha_ref[x] = alpha
            p_ref[x] = p.astype(BF16)
        scores(j)
        return 0

    lax.fori_loop(1, cur + 1, trip, 0)
    fold(jnp.maximum(cur - 1, 0))
    out_t = jnp.concatenate([acc_ref[0] / l_ref[0], acc_ref[1] / l_ref[1]], axis=0)
    out_ref[...] = out_t.T.astype(BF16)


def _attn_specs(S):
    tile = pl.BlockSpec((ATTN_TILE, LANES), lambda p, i: (i, p))
    resident = pl.BlockSpec((S, LANES), lambda p, i: (0, p))
    return tile, resident


def _moba_attn(q, k, v, kmean):
    S, D = q.shape
    assert MOBA_BLOCK == ATTN_TILE and ATTN_TILE % Q_BLOCK == 0 and S % ATTN_TILE == 0
    nb = S // MOBA_BLOCK
    tile, resident = _attn_specs(S)
    return pl.pallas_call(
        _moba_attn_kernel,
        grid=(D // LANES, S // ATTN_TILE),
        in_specs=[tile, resident, pl.BlockSpec((nb, LANES, MOBA_BLOCK), lambda p, i: (0, p, 0)),
                  pl.BlockSpec((nb, LANES), lambda p, i: (0, p))],
        out_specs=tile,
        out_shape=jax.ShapeDtypeStruct((S, D), BF16),
        scratch_shapes=[pltpu.VMEM((2, nb, ATTN_TILE), F32),
                        pltpu.VMEM((2, ATTN_TILE, ATTN_TILE), F32), pltpu.VMEM((2, ATTN_TILE, ATTN_TILE), BF16),
                        pltpu.VMEM((2, 1, ATTN_TILE), F32), pltpu.VMEM((2, 1, ATTN_TILE), F32),
                        pltpu.VMEM((2, 1, ATTN_TILE), F32), pltpu.VMEM((2, ATTN_DH, ATTN_TILE), F32)],
        compiler_params=_cparams("arbitrary", "arbitrary"),
        name="moba_attn",
    )(q, k, v, kmean)


def _sb_attn_kernel(q_ref, k_ref, v_ref, out_ref):
    T = ATTN_TILE
    c2 = ATTN_DH ** -0.5 * LOG2E
    i = pl.program_id(1)
    qs = _split_pair(q_ref[...])
    key_row = lax.broadcasted_iota(jnp.int32, (T, T), 0)
    key_col = lax.broadcasted_iota(jnp.int32, (T, T), 1)
    later_keys = (key_row > key_col).astype(BF16)
    strict = key_col < key_row

    def tile(j, carry, masked):
        start = pl.multiple_of(j * T, T)
        k_j = k_ref[pl.ds(start, T), :]
        v_j = v_ref[pl.ds(start, T), :]
        new = []
        for x, qx in enumerate(qs):
            run, acc = carry[2 * x:2 * x + 2]
            z2 = _dot_nt(qx, k_j) * c2
            sp = jnp.maximum(z2, 0.0) + jnp.log2(1.0 + jnp.exp2(-jnp.abs(z2)))
            if masked:
                sp = jnp.where(strict, sp, 0.0)
            hi = sp.astype(BF16)
            lo = (sp - hi.astype(F32)).astype(BF16)
            later = _dot(hi, later_keys) + _dot(lo, later_keys) + run
            a = jnp.exp2(z2 - sp - later)
            if masked:
                a = jnp.where(strict, a, 0.0)
            new += [run + jnp.sum(sp, axis=1, keepdims=True), acc + _dot(a.astype(BF16), v_j)]
        return tuple(new)

    def least_run(carry):
        return jnp.min(jnp.minimum(carry[0], carry[2]))

    def more(state):
        j, least = state[0], state[1]
        return jnp.logical_and(j >= 0, least < SB_RUN_CUTOFF)

    def step(state):
        carry = tile(state[0], state[2:], False)
        return (state[0] - 1, least_run(carry)) + carry

    zero = (jnp.zeros((T, 1), F32), jnp.zeros((T, LANES), F32))
    carry = tile(i, zero + zero, True)
    state = lax.while_loop(more, step, (i - 1, least_run(carry)) + carry)
    out_ref[...] = _merge_pair(state[3], state[5]).astype(BF16)


def _sb_attn(q, k, v):
    S, D = q.shape
    assert ATTN_TILE % Q_BLOCK == 0 and S % ATTN_TILE == 0
    tile, resident = _attn_specs(S)
    return pl.pallas_call(
        _sb_attn_kernel,
        grid=(D // LANES, S // ATTN_TILE),
        in_specs=[tile, resident, resident],
        out_specs=tile,
        out_shape=jax.ShapeDtypeStruct((S, D), BF16),
        compiler_params=_cparams("arbitrary", "arbitrary"),
        name="sb_attn",
    )(q, k, v)


def _conv_proj_kernel(h_ref, g_ref, wb_ref, wc_ref, wu_ref, cw_ref, out_ref, tail_ref):
    tm = h_ref.shape[0]

    @pl.when(pl.program_id(0) == 0)
    def _():
        tail_ref[...] = jnp.zeros_like(tail_ref)

    a = _rms(h_ref[...], g_ref[...]).astype(BF16)
    z = _dot(a, wc_ref[...]) * _dot(a, wu_ref[...])
    row = lax.broadcasted_iota(jnp.int32, z.shape, 0)
    prev1 = tail_ref[7:8, :]
    prev2 = tail_ref[6:7, :]
    z1 = jnp.where(row == 0, prev1, pltpu.roll(z, 1, axis=0))
    z2 = jnp.where(row == 0, prev2, jnp.where(row == 1, prev1, pltpu.roll(z, 2, axis=0)))
    y = cw_ref[0:1, :] * z2 + cw_ref[1:2, :] * z1 + cw_ref[2:3, :] * z
    tail_ref[...] = z[tm - 8:, :]
    out_ref[...] = (_dot(a, wb_ref[...]) * y).astype(BF16)


def _conv_mixer_pre(h, gain, w_in, conv_w):
    S, D = h.shape
    assert conv_w.shape[0] == CONV_WIDTH == 3
    wb = w_in[:, :D].astype(BF16)
    wc = w_in[:, D:2 * D].astype(BF16)
    wu = w_in[:, 2 * D:].astype(BF16)
    cw = jnp.pad(conv_w.astype(F32), ((0, 8 - CONV_WIDTH), (0, 0)))
    tm = min(ROW_TILE, S)
    row = pl.BlockSpec((tm, D), lambda i: (i, 0))
    return pl.pallas_call(
        _conv_proj_kernel,
        grid=(S // tm,),
        in_specs=[row, _const_spec((1, D)), _const_spec((D, D)), _const_spec((D, D)), _const_spec((D, D)),
                  _const_spec((8, D))],
        out_specs=row,
        out_shape=jax.ShapeDtypeStruct((S, D), BF16),
        scratch_shapes=[pltpu.VMEM((8, D), F32)],
        compiler_params=_cparams("arbitrary"),
        name="conv_proj",
    )(h, gain.reshape(1, D), wb, wc, wu, cw)


def _post_kernel(*refs, final):
    if final:
        h_ref, x_ref, wo_ref, g_ref, wg_ref, wu_ref, wd_ref, fg_ref, out_ref = refs
    else:
        h_ref, x_ref, wo_ref, g_ref, wg_ref, wu_ref, wd_ref, out_ref = refs
    h1 = h_ref[...] + _dot(x_ref[...], wo_ref[...])
    a = _rms(h1, g_ref[...]).astype(BF16)
    fc = wg_ref.shape[1] // FFN_CHUNKS
    y = h1
    for c in range(FFN_CHUNKS):
        gate = _dot(a, wg_ref[:, c * fc:(c + 1) * fc])
        up = _dot(a, wu_ref[:, c * fc:(c + 1) * fc])
        act = (gate * jax.nn.sigmoid(gate) * up).astype(BF16)
        y = y + _dot(act, wd_ref[c * fc:(c + 1) * fc, :])
    if final:
        y = _rms(y, fg_ref[...])
    out_ref[...] = y


def _post(h, x, w_out, gain, w_gate, w_up, w_down, final_gain=None):
    S, D = h.shape
    K = x.shape[1]
    F = w_gate.shape[1]
    assert F % (FFN_CHUNKS * LANES) == 0
    final = final_gain is not None
    tm = min(ROW_TILE, S)
    in_specs = [pl.BlockSpec((tm, D), lambda i: (i, 0)), pl.BlockSpec((tm, K), lambda i: (i, 0)),
                _const_spec((K, D)), _const_spec((1, D)), _const_spec((D, F)), _const_spec((D, F)),
                _const_spec((F, D))]
    args = [h, x, w_out.astype(BF16), gain.reshape(1, D), w_gate.astype(BF16), w_up.astype(BF16),
            w_down.astype(BF16)]
    if final:
        in_specs.append(_const_spec((1, D)))
        args.append(final_gain.reshape(1, D))
    return pl.pallas_call(
        functools.partial(_post_kernel, final=final),
        grid=(S // tm,),
        in_specs=in_specs,
        out_specs=pl.BlockSpec((tm, D), lambda i: (i, 0)),
        out_shape=jax.ShapeDtypeStruct((S, D), F32),
        compiler_params=_cparams("arbitrary"),
        name="post_final" if final else "post",
    )(*args)


def _mixer(kind, j, h, gain, positions, p):
    if kind == 0:
        q, k, v, o, gates = _mlstm_proj(h, gain, p["mlstm_w_in"][j])
        return _mlstm_core(q, k, v, o, gates, p["mlstm_b_gate"][j], p["mlstm_head_gain"][j]), p["mlstm_w_out"][j]
    if kind == 1:
        q, k, v, kmean = _qkv_proj(h, gain, p["moba_w_qkv"][j], positions)
        return _moba_attn(q, k, v, kmean), p["moba_w_out"][j]
    if kind == 2:
        return _conv_mixer_pre(h, gain, p["conv_w_in"][j], p["conv_w"][j]), p["conv_w_out"][j]
    q, k, v = _qkv_proj(h, gain, p["sb_w_qkv"][j])
    return _sb_attn(q, k, v), p["sb_w_out"][j]


def kernel(x, positions, norm_gains, mlstm_w_in, mlstm_b_gate, mlstm_head_gain, mlstm_w_out, moba_w_qkv, moba_w_out, conv_w_in, conv_w, conv_w_out, sb_w_qkv, sb_w_out, ffn_w_gate, ffn_w_up, ffn_w_down, final_gain):
    B, S, D = x.shape
    assert D == D_MODEL
    depth = norm_gains.shape[0]
    p = dict(mlstm_w_in=mlstm_w_in, mlstm_b_gate=mlstm_b_gate, mlstm_head_gain=mlstm_head_gain,
             mlstm_w_out=mlstm_w_out, moba_w_qkv=moba_w_qkv, moba_w_out=moba_w_out,
             conv_w_in=conv_w_in, conv_w=conv_w, conv_w_out=conv_w_out,
             sb_w_qkv=sb_w_qkv, sb_w_out=sb_w_out)
    outs = []
    for b in range(B):
        h = x[b]
        for layer in range(depth):
            kind, j = layer % 4, layer // 4
            mix, w_out = _mixer(kind, j, h, norm_gains[layer, 0], positions[b], p)
            h = _post(h, mix, w_out, norm_gains[layer, 1], ffn_w_gate[layer], ffn_w_up[layer],
                      ffn_w_down[layer], final_gain if layer == depth - 1 else None)
        outs.append(h)
    return jnp.stack(outs)
```

```python
import functools

import jax
import jax.numpy as jnp
from jax import lax
from jax.experimental import pallas as pl
from jax.experimental.pallas import tpu as pltpu

F32 = jnp.float32
BF16 = jnp.bfloat16

EPS = 1e-6
NEG = -1e30
D_MODEL = 1024
MLSTM_HEADS = 4
MLSTM_DQK = 128
MLSTM_DV = 256
MLSTM_CHUNK = 128
ATTN_HEADS = 16
ATTN_DH = 64
ROPE_DIMS = 16
ROPE_THETA = 500000.0
MOBA_BLOCK = 256
MOBA_TOPK = 3
Q_BLOCK = 128
ATTN_TILE = 256
LOG2E = 1.4426950408889634
SB_RUN_CUTOFF = 150.0
VT_ROWS = 80
BIG = 1e30
MOBA_GROUP = 4
CONV_WIDTH = 3

LANES = 128
ROW_TILE = 512
FFN_CHUNKS = 2
VMEM_LIMIT = 56 * 1024 * 1024


def _cparams(*sem):
    return pltpu.CompilerParams(dimension_semantics=sem, vmem_limit_bytes=VMEM_LIMIT)


def _const_spec(shape):
    nd = len(shape)
    return pl.BlockSpec(shape, lambda *_: (0,) * nd, pipeline_mode=pl.Buffered(1))


def _rms(x, g):
    return x * lax.rsqrt(jnp.mean(x * x, axis=-1, keepdims=True) + EPS) * g


def _dot(a, b):
    return jnp.dot(a, b, preferred_element_type=F32)


def _dot_nt(a, b):
    return lax.dot_general(a, b, (((1,), (1,)), ((), ())), preferred_element_type=F32)


def _dot_tn(a, b):
    return lax.dot_general(a, b, (((0,), (0,)), ((), ())), preferred_element_type=F32)


def _mlstm_proj_kernel(h_ref, g_ref, wq_ref, wk_ref, wv_ref, wo_ref, wg_ref,
                       q_ref, k_ref, v_ref, o_ref, gate_ref):
    a = _rms(h_ref[...], g_ref[...]).astype(BF16)
    q_ref[...] = (_dot(a, wq_ref[...]) * (MLSTM_DQK ** -0.5)).astype(BF16)
    k_ref[...] = _dot(a, wk_ref[...]).astype(BF16)
    v_ref[...] = _dot(a, wv_ref[...]).astype(BF16)
    o_ref[...] = _dot(a, wo_ref[...])
    gate_ref[...] = _dot(a, wg_ref[...])


def _mlstm_proj(h, gain, w_in):
    S, D = h.shape
    NH, DK, DV = MLSTM_HEADS, MLSTM_DQK, MLSTM_DV
    nq, nv = NH * DK, NH * DV
    wq = w_in[:, :nq].astype(BF16)
    wk = w_in[:, nq:2 * nq].astype(BF16)
    wv = w_in[:, 2 * nq:2 * nq + nv].astype(BF16)
    wo = w_in[:, 2 * nq + nv:2 * nq + 2 * nv].astype(BF16)
    wg = jnp.pad(w_in[:, 2 * nq + 2 * nv:], ((0, 0), (0, LANES - 2 * NH))).astype(BF16)
    tm = min(ROW_TILE, S)
    row = lambda n: pl.BlockSpec((tm, n), lambda i: (i, 0))
    return pl.pallas_call(
        _mlstm_proj_kernel,
        grid=(S // tm,),
        in_specs=[row(D), _const_spec((1, D)), _const_spec((D, nq)), _const_spec((D, nq)),
                  _const_spec((D, nv)), _const_spec((D, nv)), _const_spec((D, LANES))],
        out_specs=[row(nq), row(nq), row(nv), row(nv), row(LANES)],
        out_shape=[jax.ShapeDtypeStruct((S, nq), BF16), jax.ShapeDtypeStruct((S, nq), BF16),
                   jax.ShapeDtypeStruct((S, nv), BF16), jax.ShapeDtypeStruct((S, nv), F32),
                   jax.ShapeDtypeStruct((S, LANES), F32)],
        compiler_params=_cparams("arbitrary"),
        name="mlstm_proj",
    )(h, gain.reshape(1, D), wq, wk, wv, wo, wg)


def _mlstm_core_kernel(q_ref, k_ref, v_ref, o_ref, gate_ref, bg_ref, hg_ref, out_ref,
                       c_ref, n_ref, m_ref):
    NH, DK, DV, L = MLSTM_HEADS, MLSTM_DQK, MLSTM_DV, MLSTM_CHUNK

    @pl.when(pl.program_id(0) == 0)
    def _():
        c_ref[...] = jnp.zeros_like(c_ref)
        n_ref[...] = jnp.zeros_like(n_ref)
        m_ref[...] = jnp.zeros_like(m_ref)

    g = gate_ref[...] + bg_ref[...]
    lane = lax.broadcasted_iota(jnp.int32, (L, LANES), 1)
    log_sig = jnp.minimum(g, 0.0) - jnp.log(1.0 + jnp.exp(-jnp.abs(g)))
    gl = jnp.where(lane >= NH, log_sig, g)
    gl_t = gl.T
    t_idx = lax.broadcasted_iota(jnp.int32, (L, L), 0)
    s_idx = lax.broadcasted_iota(jnp.int32, (L, L), 1)
    causal = s_idx <= t_idx

    for hd in range(NH):
        i_col = gl[:, hd:hd + 1]
        f_col = gl[:, NH + hd:NH + hd + 1]
        i_row = gl_t[hd:hd + 1, :]
        f_row = gl_t[NH + hd:NH + hd + 1, :]
        b_col = jnp.sum(jnp.where(causal, f_row, 0.0), axis=1, keepdims=True)
        b_row = jnp.sum(jnp.where(t_idx <= s_idx, f_col, 0.0), axis=0, keepdims=True)
        m_prev = m_ref[hd]
        dmat = jnp.where(causal, b_col - b_row + i_row, -jnp.inf)
        inter = b_col + m_prev
        m_t = jnp.maximum(inter, jnp.max(dmat, axis=1, keepdims=True))
        w_intra = jnp.exp(dmat - m_t)
        w_inter = jnp.exp(inter - m_t)

        q = q_ref[:, hd * DK:(hd + 1) * DK]
        k = k_ref[:, hd * DK:(hd + 1) * DK]
        v = v_ref[:, hd * DV:(hd + 1) * DV]
        c_old = c_ref[hd]
        n_old = n_ref[hd]
        s = _dot_nt(q, k) * w_intra
        num = _dot(s.astype(BF16), v) + w_inter * _dot(q, c_old.astype(BF16))
        qn = jnp.sum(q.astype(F32) * n_old, axis=1, keepdims=True)
        den = jnp.sum(s, axis=1, keepdims=True) + w_inter * qn
        den = jnp.maximum(jnp.abs(den), jnp.exp(-m_t))
        h_out = num / den

        g_tot = b_col[L - 1:L, :]
        a_col = g_tot - b_col + i_col
        m_new = jnp.maximum(g_tot + m_prev, jnp.max(a_col, axis=0, keepdims=True))
        decay = jnp.exp(g_tot + m_prev - m_new)
        wa = jnp.exp(a_col - m_new)
        kw = k.astype(F32) * wa
        c_ref[hd] = decay * c_old + _dot_tn(kw.astype(BF16), v)
        n_ref[hd] = decay * n_old + jnp.sum(kw, axis=0, keepdims=True)
        m_ref[hd] = m_new

        hn = h_out * lax.rsqrt(jnp.mean(h_out * h_out, axis=1, keepdims=True) + EPS)
        hn = hn * hg_ref[:, hd * DV:(hd + 1) * DV]
        out = hn * jax.nn.sigmoid(o_ref[:, hd * DV:(hd + 1) * DV])
        out_ref[:, hd * DV:(hd + 1) * DV] = out.astype(BF16)


def _mlstm_core(q, k, v, o, gates, b_gate, head_gain):
    S = q.shape[0]
    NH, DK, DV, L = MLSTM_HEADS, MLSTM_DQK, MLSTM_DV, MLSTM_CHUNK
    bg = jnp.pad(b_gate.astype(F32), (0, LANES - 2 * NH)).reshape(1, LANES)
    row = lambda n: pl.BlockSpec((L, n), lambda c: (c, 0))
    return pl.pallas_call(
        _mlstm_core_kernel,
        grid=(S // L,),
        in_specs=[row(NH * DK), row(NH * DK), row(NH * DV), row(NH * DV), row(LANES),
                  _const_spec((1, LANES)), _const_spec((1, NH * DV))],
        out_specs=row(NH * DV),
        out_shape=jax.ShapeDtypeStruct((S, NH * DV), BF16),
        scratch_shapes=[pltpu.VMEM((NH, DK, DV), F32), pltpu.VMEM((NH, 1, DK), F32),
                        pltpu.VMEM((NH, 1, 1), F32)],
        compiler_params=_cparams("arbitrary"),
        name="mlstm_core",
    )(q, k, v, o, gates, bg, head_gain.astype(F32).reshape(1, NH * DV))


def _rope_tile(x, cos, sin_lo, sin_hi):
    half = ROPE_DIMS // 2
    cols = []
    for c in range(x.shape[1] // LANES):
        xc = x[:, c * LANES:(c + 1) * LANES]
        up = pltpu.roll(xc, LANES - half, axis=1)
        down = pltpu.roll(xc, half, axis=1)
        cols.append(xc * cos + up * sin_lo + down * sin_hi)
    return jnp.concatenate(cols, axis=1)


def _qkv_proj_kernel(*refs, rope):
    if rope:
        (h_ref, g_ref, wq_ref, wk_ref, wv_ref, pos_ref, inv_ref,
         q_ref, k_ref, v_ref, kmean_ref) = refs
    else:
        h_ref, g_ref, wq_ref, wk_ref, wv_ref, q_ref, k_ref, v_ref = refs
    a = _rms(h_ref[...], g_ref[...]).astype(BF16)
    q = _dot(a, wq_ref[...])
    k = _dot(a, wk_ref[...])
    v = _dot(a, wv_ref[...])
    if not rope:
        v_ref[...] = v.astype(BF16)
    q = q * (ATTN_DH ** -0.5 * LOG2E)
    if rope:
        tm = q.shape[0]
        row = lax.broadcasted_iota(jnp.int32, (VT_ROWS - ATTN_DH, MOBA_BLOCK), 0)
        ones_pad = jnp.where(row == 0, 1.0, 0.0).astype(BF16)
        for b in range(tm // MOBA_BLOCK):
            vt = v[b * MOBA_BLOCK:(b + 1) * MOBA_BLOCK, :].T.astype(BF16)
            for hd in range(ATTN_HEADS):
                v_ref[b, hd * VT_ROWS:hd * VT_ROWS + ATTN_DH, :] = vt[hd * ATTN_DH:(hd + 1) * ATTN_DH, :]
                v_ref[b, hd * VT_ROWS + ATTN_DH:(hd + 1) * VT_ROWS, :] = ones_pad
        half = ROPE_DIMS // 2
        ang = pos_ref[...].astype(F32) * inv_ref[...]
        cos = jnp.cos(ang)
        sin = jnp.sin(ang)
        dim = lax.broadcasted_iota(jnp.int32, (tm, LANES), 1) % ATTN_DH
        sin_lo = jnp.where(dim < half, -sin, 0.0)
        sin_hi = jnp.where((dim >= half) & (dim < ROPE_DIMS), sin, 0.0)
        q = _rope_tile(q, cos, sin_lo, sin_hi)
        k = _rope_tile(k, cos, sin_lo, sin_hi)
        nblk = tm // MOBA_BLOCK
        kmean_ref[0] = jnp.sum(k.reshape(nblk, MOBA_BLOCK, k.shape[1]), axis=1) * (1.0 / MOBA_BLOCK)
    q_ref[...] = q.astype(BF16)
    k_ref[...] = k.astype(BF16)


def _qkv_proj(h, gain, w_qkv, positions=None):
    S, D = h.shape
    rope = positions is not None
    wq = w_qkv[:, :D].astype(BF16)
    wk = w_qkv[:, D:2 * D].astype(BF16)
    wv = w_qkv[:, 2 * D:].astype(BF16)
    tm = min(ROW_TILE, S)
    row = lambda n: pl.BlockSpec((tm, n), lambda i: (i, 0))
    in_specs = [row(D), _const_spec((1, D)), _const_spec((D, D)), _const_spec((D, D)), _const_spec((D, D))]
    args = [h, gain.reshape(1, D), wq, wk, wv]
    out_specs = [row(D), row(D), row(D)]
    out_shape = [jax.ShapeDtypeStruct((S, D), BF16)] * 3
    if rope:
        assert S % MOBA_BLOCK == 0 and tm % MOBA_BLOCK == 0
        half = ROPE_DIMS // 2
        inv = ROPE_THETA ** (-jnp.arange(half, dtype=F32) / half)
        dim = jnp.arange(LANES) % ATTN_DH
        inv_lane = jnp.where(dim < ROPE_DIMS, inv[dim % half], 0.0).astype(F32).reshape(1, LANES)
        in_specs += [row(1), _const_spec((1, LANES))]
        args += [positions.reshape(S, 1), inv_lane]
        nblk = tm // MOBA_BLOCK
        vt_rows = ATTN_HEADS * VT_ROWS
        out_specs[2] = pl.BlockSpec((nblk, vt_rows, MOBA_BLOCK), lambda i: (i, 0, 0))
        out_shape[2] = jax.ShapeDtypeStruct((S // MOBA_BLOCK, vt_rows, MOBA_BLOCK), BF16)
        out_specs.append(pl.BlockSpec((1, nblk, D), lambda i: (i, 0, 0)))
        out_shape.append(jax.ShapeDtypeStruct((S // tm, nblk, D), F32))
    outs = pl.pallas_call(
        functools.partial(_qkv_proj_kernel, rope=rope),
        grid=(S // tm,),
        in_specs=in_specs,
        out_specs=out_specs,
        out_shape=out_shape,
        compiler_params=_cparams("arbitrary"),
        name="moba_proj" if rope else "sb_proj",
    )(*args)
    if rope:
        q, k, v, kmean = outs
        return q, k, v, kmean.reshape(S // MOBA_BLOCK, D)
    return outs


def _split_pair(q):
    lane = lax.broadcasted_iota(jnp.int32, q.shape, 1)
    zero = jnp.zeros_like(q)
    return jnp.where(lane < ATTN_DH, q, zero), jnp.where(lane >= ATTN_DH, q, zero)


def _merge_pair(acc_a, acc_b):
    lane = lax.broadcasted_iota(jnp.int32, acc_a.shape, 1)
    return jnp.where(lane < ATTN_DH, acc_a, acc_b)


def _moba_select(gate, cur):
    nb = gate.shape[0]
    blk = lax.broadcasted_iota(jnp.int32, gate.shape, 0)
    valid = blk < cur
    g = jnp.where(valid, gate, -jnp.inf)
    sel = jnp.zeros(gate.shape, F32)
    for _ in range(MOBA_TOPK):
        mx = jnp.max(g, axis=0, keepdims=True)
        first = jnp.min(jnp.where(g == mx, blk, nb), axis=0, keepdims=True)
        hit = (blk == first) & valid
        sel = jnp.where(hit, 1.0, sel)
        g = jnp.where(blk == first, -jnp.inf, g)
    return sel


def _moba_attn_kernel(q_ref, k_ref, vt_ref, kmean_ref, out_ref,
                      sel_ref, s_ref, p_ref, alpha_ref, m_ref, acc_ref):
    T = ATTN_TILE
    H = ATTN_DH
    R = VT_ROWS
    cur = pl.program_id(1)
    qt = q_ref[...].astype(F32).T
    dim = lax.broadcasted_iota(jnp.int32, qt.shape, 0)
    qts = [jnp.where(dim < H, qt, 0.0).astype(BF16), jnp.where(dim >= H, qt, 0.0).astype(BF16)]
    km = kmean_ref[...].astype(BF16)
    for x in range(2):
        sel_ref[x] = _moba_select(_dot(km, qts[x]), cur)
    key = lax.broadcasted_iota(jnp.int32, (T, T), 0)
    query = lax.broadcasted_iota(jnp.int32, (T, T), 1)
    causal = key <= query

    G = MOBA_GROUP
    nb = sel_ref.shape[1]

    k_own = k_ref[pl.ds(pl.multiple_of(cur * T, T), T), :]
    for x in range(2):
        s = jnp.where(causal, _dot(k_own, qts[x]), NEG)
        m = jnp.max(s, axis=0, keepdims=True)
        m_ref[x] = m
        acc_ref[x] = _dot(vt_ref[cur, x * R:(x + 1) * R, :], jnp.exp2(s - m).astype(BF16))
        alpha_ref[x] = jnp.ones_like(m)

    @pl.when(cur == 0)
    def _():
        p_ref[...] = jnp.zeros(p_ref.shape, BF16)

    def scores(t):
        first = jnp.minimum(t * G, nb - G)
        k_g = k_ref[pl.ds(pl.multiple_of(first * T, T), G * T), :]
        for x in range(2):
            s_ref[x] = _dot(k_g, qts[x])

    def fold(t, gate):
        for x in range(2):
            part = _dot(vt_ref[t * G, x * R:(x + 1) * R, :], p_ref[x, 0:T, :])
            for g in range(1, G):
                part = part + _dot(vt_ref[t * G + g, x * R:(x + 1) * R, :], p_ref[x, g * T:(g + 1) * T, :])
            acc_ref[x] = alpha_ref[x] * acc_ref[x] + gate * part

    scores(0)

    def trip(t, _):
        fold(jnp.maximum(t - 1, 0), jnp.where(t > 0, 1.0, 0.0))
        for x in range(2):
            chosen = [sel_ref[x, pl.ds(t * G + g, 1), :] > 0.0 for g in range(G)]
            m = m_ref[x]
            m_new = m
            for g in range(G):
                col_max = jnp.max(s_ref[x, g * T:(g + 1) * T, :], axis=0, keepdims=True)
                m_new = jnp.maximum(m_new, jnp.where(chosen[g], col_max, NEG))
            m_ref[x] = m_new
            alpha_ref[x] = jnp.exp2(m - m_new)
            for g in range(G):
                s = s_ref[x, g * T:(g + 1) * T, :]
                p_ref[x, g * T:(g + 1) * T, :] = jnp.exp2(s - jnp.where(chosen[g], m_new, BIG)).astype(BF16)
        scores(t + 1)
        return 0

    trips = (cur + G - 1) // G
    lax.fori_loop(0, trips, trip, 0)
    fold(jnp.maximum(trips - 1, 0), jnp.where(trips > 0, 1.0, 0.0))
    out_t = jnp.concatenate([acc_ref[x][:H, :] / acc_ref[x][H:H + 1, :] for x in range(2)], axis=0)
    out_ref[...] = out_t.T.astype(BF16)


def _attn_specs(S):
    tile = pl.BlockSpec((ATTN_TILE, LANES), lambda p, i: (i, p))
    resident = pl.BlockSpec((S, LANES), lambda p, i: (0, p))
    return tile, resident


def _moba_attn(q, k, v, kmean):
    S, D = q.shape
    assert MOBA_BLOCK == ATTN_TILE and ATTN_TILE % Q_BLOCK == 0 and S % ATTN_TILE == 0
    nb = S // MOBA_BLOCK
    assert nb % MOBA_GROUP == 0
    tile, resident = _attn_specs(S)
    return pl.pallas_call(
        _moba_attn_kernel,
        grid=(D // LANES, S // ATTN_TILE),
        in_specs=[tile, resident, pl.BlockSpec((nb, 2 * VT_ROWS, MOBA_BLOCK), lambda p, i: (0, p, 0)),
                  pl.BlockSpec((nb, LANES), lambda p, i: (0, p))],
        out_specs=tile,
        out_shape=jax.ShapeDtypeStruct((S, D), BF16),
        scratch_shapes=[pltpu.VMEM((2, nb, ATTN_TILE), F32),
                        pltpu.VMEM((2, MOBA_GROUP * ATTN_TILE, ATTN_TILE), F32),
                        pltpu.VMEM((2, MOBA_GROUP * ATTN_TILE, ATTN_TILE), BF16),
                        pltpu.VMEM((2, 1, ATTN_TILE), F32), pltpu.VMEM((2, 1, ATTN_TILE), F32),
                        pltpu.VMEM((2, VT_ROWS, ATTN_TILE), F32)],
        compiler_params=_cparams("arbitrary", "arbitrary"),
        name="moba_attn",
    )(q, k, v, kmean)


def _sb_attn_kernel(q_ref, k_ref, v_ref, out_ref):
    T = ATTN_TILE
    i = pl.program_id(1)
    qs = _split_pair(q_ref[...])
    key_row = lax.broadcasted_iota(jnp.int32, (T, T), 0)
    key_col = lax.broadcasted_iota(jnp.int32, (T, T), 1)
    later_keys = (key_row > key_col).astype(BF16)
    strict = key_col < key_row

    def tile(j, carry, masked, weight=None):
        start = pl.multiple_of(j * T, T)
        k_j = k_ref[pl.ds(start, T), :]
        v_j = v_ref[pl.ds(start, T), :]
        new = []
        for x, qx in enumerate(qs):
            run, acc = carry[2 * x:2 * x + 2]
            z2 = _dot_nt(qx, k_j)
            sp = jnp.maximum(z2, 0.0) + jnp.log2(1.0 + jnp.exp2(-jnp.abs(z2)))
            if masked:
                sp = jnp.where(strict, sp, 0.0)
            hi = sp.astype(BF16)
            lo = (sp - hi.astype(F32)).astype(BF16)
            later = _dot(hi, later_keys) + _dot(lo, later_keys) + run
            a = jnp.exp2(z2 - sp - later)
            if masked:
                a = jnp.where(strict, a, 0.0)
            sp_sum = jnp.sum(sp, axis=1, keepdims=True)
            av = _dot(a.astype(BF16), v_j)
            if weight is not None:
                sp_sum, av = weight * sp_sum, weight * av
            new += [run + sp_sum, acc + av]
        return tuple(new)

    def least_run(carry):
        return jnp.min(jnp.minimum(carry[0], carry[2]))

    def more(state):
        j, least = state[0], state[1]
        return jnp.logical_and(j >= 0, least < SB_RUN_CUTOFF)

    def step(state):
        carry = tile(state[0], state[2:], False)
        return (state[0] - 1, least_run(carry)) + carry

    zero = (jnp.zeros((T, 1), F32), jnp.zeros((T, LANES), F32))
    carry = tile(i, zero + zero, True)
    carry = tile(jnp.maximum(i - 1, 0), carry, False, jnp.where(i > 0, 1.0, 0.0))
    state = lax.while_loop(more, step, (i - 2, least_run(carry)) + carry)
    out_ref[...] = _merge_pair(state[3], state[5]).astype(BF16)


def _sb_attn(q, k, v):
    S, D = q.shape
    assert ATTN_TILE % Q_BLOCK == 0 and S % ATTN_TILE == 0
    tile, resident = _attn_specs(S)
    return pl.pallas_call(
        _sb_attn_kernel,
        grid=(D // LANES, S // ATTN_TILE),
        in_specs=[tile, resident, resident],
        out_specs=tile,
        out_shape=jax.ShapeDtypeStruct((S, D), BF16),
        compiler_params=_cparams("arbitrary", "arbitrary"),
        name="sb_attn",
    )(q, k, v)


def _conv_proj_kernel(h_ref, g_ref, wb_ref, wc_ref, wu_ref, cw_ref, out_ref, tail_ref):
    tm = h_ref.shape[0]

    @pl.when(pl.program_id(0) == 0)
    def _():
        tail_ref[...] = jnp.zeros_like(tail_ref)

    a = _rms(h_ref[...], g_ref[...]).astype(BF16)
    z = _dot(a, wc_ref[...]) * _dot(a, wu_ref[...])
    row = lax.broadcasted_iota(jnp.int32, z.shape, 0)
    prev1 = tail_ref[7:8, :]
    prev2 = tail_ref[6:7, :]
    z1 = jnp.where(row == 0, prev1, pltpu.roll(z, 1, axis=0))
    z2 = jnp.where(row == 0, prev2, jnp.where(row == 1, prev1, pltpu.roll(z, 2, axis=0)))
    y = cw_ref[0:1, :] * z2 + cw_ref[1:2, :] * z1 + cw_ref[2:3, :] * z
    tail_ref[...] = z[tm - 8:, :]
    out_ref[...] = (_dot(a, wb_ref[...]) * y).astype(BF16)


def _conv_mixer_pre(h, gain, w_in, conv_w):
    S, D = h.shape
    assert conv_w.shape[0] == CONV_WIDTH == 3
    wb = w_in[:, :D].astype(BF16)
    wc = w_in[:, D:2 * D].astype(BF16)
    wu = w_in[:, 2 * D:].astype(BF16)
    cw = jnp.pad(conv_w.astype(F32), ((0, 8 - CONV_WIDTH), (0, 0)))
    tm = min(ROW_TILE, S)
    row = pl.BlockSpec((tm, D), lambda i: (i, 0))
    return pl.pallas_call(
        _conv_proj_kernel,
        grid=(S // tm,),
        in_specs=[row, _const_spec((1, D)), _const_spec((D, D)), _const_spec((D, D)), _const_spec((D, D)),
                  _const_spec((8, D))],
        out_specs=row,
        out_shape=jax.ShapeDtypeStruct((S, D), BF16),
        scratch_shapes=[pltpu.VMEM((8, D), F32)],
        compiler_params=_cparams("arbitrary"),
        name="conv_proj",
    )(h, gain.reshape(1, D), wb, wc, wu, cw)


def _post_kernel(*refs, final):
    if final:
        h_ref, x_ref, wo_ref, g_ref, wg_ref, wu_ref, wd_ref, fg_ref, out_ref = refs
    else:
        h_ref, x_ref, wo_ref, g_ref, wg_ref, wu_ref, wd_ref, out_ref = refs
    h1 = h_ref[...] + _dot(x_ref[...], wo_ref[...])
    a = _rms(h1, g_ref[...]).astype(BF16)
    fc = wg_ref.shape[1] // FFN_CHUNKS
    y = h1
    for c in range(FFN_CHUNKS):
        gate = _dot(a, wg_ref[:, c * fc:(c + 1) * fc])
        up = _dot(a, wu_ref[:, c * fc:(c + 1) * fc])
        act = (gate * jax.nn.sigmoid(gate) * up).astype(BF16)
        y = y + _dot(act, wd_ref[c * fc:(c + 1) * fc, :])
    if final:
        y = _rms(y, fg_ref[...])
    out_ref[...] = y


def _post(h, x, w_out, gain, w_gate, w_up, w_down, final_gain=None):
    S, D = h.shape
    K = x.shape[1]
    F = w_gate.shape[1]
    assert F % (FFN_CHUNKS * LANES) == 0
    final = final_gain is not None
    tm = min(ROW_TILE, S)
    in_specs = [pl.BlockSpec((tm, D), lambda i: (i, 0)), pl.BlockSpec((tm, K), lambda i: (i, 0)),
                _const_spec((K, D)), _const_spec((1, D)), _const_spec((D, F)), _const_spec((D, F)),
                _const_spec((F, D))]
    args = [h, x, w_out.astype(BF16), gain.reshape(1, D), w_gate.astype(BF16), w_up.astype(BF16),
            w_down.astype(BF16)]
    if final:
        in_specs.append(_const_spec((1, D)))
        args.append(final_gain.reshape(1, D))
    return pl.pallas_call(
        functools.partial(_post_kernel, final=final),
        grid=(S // tm,),
        in_specs=in_specs,
        out_specs=pl.BlockSpec((tm, D), lambda i: (i, 0)),
        out_shape=jax.ShapeDtypeStruct((S, D), F32),
        compiler_params=_cparams("arbitrary"),
        name="post_final" if final else "post",
    )(*args)


def _mixer(kind, j, h, gain, positions, p):
    if kind == 0:
        q, k, v, o, gates = _mlstm_proj(h, gain, p["mlstm_w_in"][j])
        return _mlstm_core(q, k, v, o, gates, p["mlstm_b_gate"][j], p["mlstm_head_gain"][j]), p["mlstm_w_out"][j]
    if kind == 1:
        q, k, v, kmean = _qkv_proj(h, gain, p["moba_w_qkv"][j], positions)
        return _moba_attn(q, k, v, kmean), p["moba_w_out"][j]
    if kind == 2:
        return _conv_mixer_pre(h, gain, p["conv_w_in"][j], p["conv_w"][j]), p["conv_w_out"][j]
    q, k, v = _qkv_proj(h, gain, p["sb_w_qkv"][j])
    return _sb_attn(q, k, v), p["sb_w_out"][j]


def kernel(x, positions, norm_gains, mlstm_w_in, mlstm_b_gate, mlstm_head_gain, mlstm_w_out, moba_w_qkv, moba_w_out, conv_w_in, conv_w, conv_w_out, sb_w_qkv, sb_w_out, ffn_w_gate, ffn_w_up, ffn_w_down, final_gain):
    B, S, D = x.shape
    assert D == D_MODEL
    depth = norm_gains.shape[0]
    p = dict(mlstm_w_in=mlstm_w_in, mlstm_b_gate=mlstm_b_gate, mlstm_head_gain=mlstm_head_gain,
             mlstm_w_out=mlstm_w_out, moba_w_qkv=moba_w_qkv, moba_w_out=moba_w_out,
             conv_w_in=conv_w_in, conv_w=conv_w, conv_w_out=conv_w_out,
             sb_w_qkv=sb_w_qkv, sb_w_out=sb_w_out)
    outs = []
    for b in range(B):
        h = x[b]
        for layer in range(depth):
            kind, j = layer % 4, layer // 4
            mix, w_out = _mixer(kind, j, h, norm_gains[layer, 0], positions[b], p)
            h = _post(h, mix, w_out, norm_gains[layer, 1], ffn_w_gate[layer], ffn_w_up[layer],
                      ffn_w_down[layer], final_gain if layer == depth - 1 else None)
        outs.append(h)
    return jnp.stack(outs)
```

```python
import functools

import jax
import jax.numpy as jnp
from jax import lax
from jax.experimental import pallas as pl
from jax.experimental.pallas import tpu as pltpu

F32 = jnp.float32
BF16 = jnp.bfloat16

EPS = 1e-6
NEG = -1e30
D_MODEL = 1024
MLSTM_HEADS = 4
MLSTM_DQK = 128
MLSTM_DV = 256
MLSTM_CHUNK = 128
ATTN_HEADS = 16
ATTN_DH = 64
ROPE_DIMS = 16
ROPE_THETA = 500000.0
MOBA_BLOCK = 256
MOBA_TOPK = 3
Q_BLOCK = 128
ATTN_TILE = 256
LOG2E = 1.4426950408889634
SB_RUN_CUTOFF = 150.0
VT_ROWS = 80
BIG = 1e30
MOBA_QUERIES = 512
MOBA_GROUP = 4
CONV_WIDTH = 3

LANES = 128
ROW_TILE = 512
FFN_CHUNKS = 2
VMEM_LIMIT = 56 * 1024 * 1024


def _cparams(*sem):
    return pltpu.CompilerParams(dimension_semantics=sem, vmem_limit_bytes=VMEM_LIMIT)


def _const_spec(shape):
    nd = len(shape)
    return pl.BlockSpec(shape, lambda *_: (0,) * nd, pipeline_mode=pl.Buffered(1))


def _rms(x, g):
    return x * lax.rsqrt(jnp.mean(x * x, axis=-1, keepdims=True) + EPS) * g


def _dot(a, b):
    return jnp.dot(a, b, preferred_element_type=F32)


def _dot_nt(a, b):
    return lax.dot_general(a, b, (((1,), (1,)), ((), ())), preferred_element_type=F32)


def _dot_tn(a, b):
    return lax.dot_general(a, b, (((0,), (0,)), ((), ())), preferred_element_type=F32)


def _mlstm_proj_kernel(h_ref, g_ref, wq_ref, wk_ref, wv_ref, wo_ref, wg_ref,
                       q_ref, k_ref, v_ref, o_ref, gate_ref):
    a = _rms(h_ref[...], g_ref[...]).astype(BF16)
    q_ref[...] = (_dot(a, wq_ref[...]) * (MLSTM_DQK ** -0.5)).astype(BF16)
    k_ref[...] = _dot(a, wk_ref[...]).astype(BF16)
    v_ref[...] = _dot(a, wv_ref[...]).astype(BF16)
    o_ref[...] = _dot(a, wo_ref[...])
    gate_ref[...] = _dot(a, wg_ref[...])


def _mlstm_proj(h, gain, w_in):
    S, D = h.shape
    NH, DK, DV = MLSTM_HEADS, MLSTM_DQK, MLSTM_DV
    nq, nv = NH * DK, NH * DV
    wq = w_in[:, :nq].astype(BF16)
    wk = w_in[:, nq:2 * nq].astype(BF16)
    wv = w_in[:, 2 * nq:2 * nq + nv].astype(BF16)
    wo = w_in[:, 2 * nq + nv:2 * nq + 2 * nv].astype(BF16)
    wg = jnp.pad(w_in[:, 2 * nq + 2 * nv:], ((0, 0), (0, LANES - 2 * NH))).astype(BF16)
    tm = min(ROW_TILE, S)
    row = lambda n: pl.BlockSpec((tm, n), lambda i: (i, 0))
    return pl.pallas_call(
        _mlstm_proj_kernel,
        grid=(S // tm,),
        in_specs=[row(D), _const_spec((1, D)), _const_spec((D, nq)), _const_spec((D, nq)),
                  _const_spec((D, nv)), _const_spec((D, nv)), _const_spec((D, LANES))],
        out_specs=[row(nq), row(nq), row(nv), row(nv), row(LANES)],
        out_shape=[jax.ShapeDtypeStruct((S, nq), BF16), jax.ShapeDtypeStruct((S, nq), BF16),
                   jax.ShapeDtypeStruct((S, nv), BF16), jax.ShapeDtypeStruct((S, nv), F32),
                   jax.ShapeDtypeStruct((S, LANES), F32)],
        compiler_params=_cparams("arbitrary"),
        name="mlstm_proj",
    )(h, gain.reshape(1, D), wq, wk, wv, wo, wg)


def _mlstm_core_kernel(q_ref, k_ref, v_ref, o_ref, gate_ref, bg_ref, hg_ref, out_ref,
                       c_ref, n_ref, m_ref):
    NH, DK, DV, L = MLSTM_HEADS, MLSTM_DQK, MLSTM_DV, MLSTM_CHUNK

    @pl.when(pl.program_id(0) == 0)
    def _():
        c_ref[...] = jnp.zeros_like(c_ref)
        n_ref[...] = jnp.zeros_like(n_ref)
        m_ref[...] = jnp.zeros_like(m_ref)

    g = gate_ref[...] + bg_ref[...]
    lane = lax.broadcasted_iota(jnp.int32, (L, LANES), 1)
    log_sig = jnp.minimum(g, 0.0) - jnp.log(1.0 + jnp.exp(-jnp.abs(g)))
    gl = jnp.where(lane >= NH, log_sig, g)
    gl_t = gl.T
    t_idx = lax.broadcasted_iota(jnp.int32, (L, L), 0)
    s_idx = lax.broadcasted_iota(jnp.int32, (L, L), 1)
    causal = s_idx <= t_idx

    for hd in range(NH):
        i_col = gl[:, hd:hd + 1]
        f_col = gl[:, NH + hd:NH + hd + 1]
        i_row = gl_t[hd:hd + 1, :]
        f_row = gl_t[NH + hd:NH + hd + 1, :]
        b_col = jnp.sum(jnp.where(causal, f_row, 0.0), axis=1, keepdims=True)
        b_row = jnp.sum(jnp.where(t_idx <= s_idx, f_col, 0.0), axis=0, keepdims=True)
        m_prev = m_ref[hd]
        dmat = jnp.where(causal, b_col - b_row + i_row, -jnp.inf)
        inter = b_col + m_prev
        m_t = jnp.maximum(inter, jnp.max(dmat, axis=1, keepdims=True))
        w_intra = jnp.exp(dmat - m_t)
        w_inter = jnp.exp(inter - m_t)

        q = q_ref[:, hd * DK:(hd + 1) * DK]
        k = k_ref[:, hd * DK:(hd + 1) * DK]
        v = v_ref[:, hd * DV:(hd + 1) * DV]
        c_old = c_ref[hd]
        n_old = n_ref[hd]
        s = _dot_nt(q, k) * w_intra
        num = _dot(s.astype(BF16), v) + w_inter * _dot(q, c_old.astype(BF16))
        qn = jnp.sum(q.astype(F32) * n_old, axis=1, keepdims=True)
        den = jnp.sum(s, axis=1, keepdims=True) + w_inter * qn
        den = jnp.maximum(jnp.abs(den), jnp.exp(-m_t))
        h_out = num / den

        g_tot = b_col[L - 1:L, :]
        a_col = g_tot - b_col + i_col
        m_new = jnp.maximum(g_tot + m_prev, jnp.max(a_col, axis=0, keepdims=True))
        decay = jnp.exp(g_tot + m_prev - m_new)
        wa = jnp.exp(a_col - m_new)
        kw = k.astype(F32) * wa
        c_ref[hd] = decay * c_old + _dot_tn(kw.astype(BF16), v)
        n_ref[hd] = decay * n_old + jnp.sum(kw, axis=0, keepdims=True)
        m_ref[hd] = m_new

        hn = h_out * lax.rsqrt(jnp.mean(h_out * h_out, axis=1, keepdims=True) + EPS)
        hn = hn * hg_ref[:, hd * DV:(hd + 1) * DV]
        out = hn * jax.nn.sigmoid(o_ref[:, hd * DV:(hd + 1) * DV])
        out_ref[:, hd * DV:(hd + 1) * DV] = out.astype(BF16)


def _mlstm_core(q, k, v, o, gates, b_gate, head_gain):
    S = q.shape[0]
    NH, DK, DV, L = MLSTM_HEADS, MLSTM_DQK, MLSTM_DV, MLSTM_CHUNK
    bg = jnp.pad(b_gate.astype(F32), (0, LANES - 2 * NH)).reshape(1, LANES)
    row = lambda n: pl.BlockSpec((L, n), lambda c: (c, 0))
    return pl.pallas_call(
        _mlstm_core_kernel,
        grid=(S // L,),
        in_specs=[row(NH * DK), row(NH * DK), row(NH * DV), row(NH * DV), row(LANES),
                  _const_spec((1, LANES)), _const_spec((1, NH * DV))],
        out_specs=row(NH * DV),
        out_shape=jax.ShapeDtypeStruct((S, NH * DV), BF16),
        scratch_shapes=[pltpu.VMEM((NH, DK, DV), F32), pltpu.VMEM((NH, 1, DK), F32),
                        pltpu.VMEM((NH, 1, 1), F32)],
        compiler_params=_cparams("arbitrary"),
        name="mlstm_core",
    )(q, k, v, o, gates, bg, head_gain.astype(F32).reshape(1, NH * DV))


def _rope_tile(x, cos, sin_lo, sin_hi):
    half = ROPE_DIMS // 2
    cols = []
    for c in range(x.shape[1] // LANES):
        xc = x[:, c * LANES:(c + 1) * LANES]
        up = pltpu.roll(xc, LANES - half, axis=1)
        down = pltpu.roll(xc, half, axis=1)
        cols.append(xc * cos + up * sin_lo + down * sin_hi)
    return jnp.concatenate(cols, axis=1)


def _qkv_proj_kernel(*refs, rope):
    if rope:
        (h_ref, g_ref, wq_ref, wk_ref, wv_ref, pos_ref, inv_ref,
         q_ref, k_ref, v_ref, kmean_ref) = refs
    else:
        h_ref, g_ref, wq_ref, wk_ref, wv_ref, q_ref, k_ref, v_ref = refs
    a = _rms(h_ref[...], g_ref[...]).astype(BF16)
    q = _dot(a, wq_ref[...])
    k = _dot(a, wk_ref[...])
    v = _dot(a, wv_ref[...])
    if not rope:
        v_ref[...] = v.astype(BF16)
    q = q * (ATTN_DH ** -0.5 * LOG2E)
    if rope:
        tm = q.shape[0]
        row = lax.broadcasted_iota(jnp.int32, (VT_ROWS - ATTN_DH, MOBA_BLOCK), 0)
        ones_pad = jnp.where(row == 0, 1.0, 0.0).astype(BF16)
        for b in range(tm // MOBA_BLOCK):
            vt = v[b * MOBA_BLOCK:(b + 1) * MOBA_BLOCK, :].T.astype(BF16)
            for hd in range(ATTN_HEADS):
                v_ref[b, hd * VT_ROWS:hd * VT_ROWS + ATTN_DH, :] = vt[hd * ATTN_DH:(hd + 1) * ATTN_DH, :]
                v_ref[b, hd * VT_ROWS + ATTN_DH:(hd + 1) * VT_ROWS, :] = ones_pad
        half = ROPE_DIMS // 2
        ang = pos_ref[...].astype(F32) * inv_ref[...]
        cos = jnp.cos(ang)
        sin = jnp.sin(ang)
        dim = lax.broadcasted_iota(jnp.int32, (tm, LANES), 1) % ATTN_DH
        sin_lo = jnp.where(dim < half, -sin, 0.0)
        sin_hi = jnp.where((dim >= half) & (dim < ROPE_DIMS), sin, 0.0)
        q = _rope_tile(q, cos, sin_lo, sin_hi)
        k = _rope_tile(k, cos, sin_lo, sin_hi)
        nblk = tm // MOBA_BLOCK
        kmean_ref[0] = jnp.sum(k.reshape(nblk, MOBA_BLOCK, k.shape[1]), axis=1) * (1.0 / MOBA_BLOCK)
    q_ref[...] = q.astype(BF16)
    k_ref[...] = k.astype(BF16)


def _qkv_proj(h, gain, w_qkv, positions=None):
    S, D = h.shape
    rope = positions is not None
    wq = w_qkv[:, :D].astype(BF16)
    wk = w_qkv[:, D:2 * D].astype(BF16)
    wv = w_qkv[:, 2 * D:].astype(BF16)
    tm = min(ROW_TILE, S)
    row = lambda n: pl.BlockSpec((tm, n), lambda i: (i, 0))
    in_specs = [row(D), _const_spec((1, D)), _const_spec((D, D)), _const_spec((D, D)), _const_spec((D, D))]
    args = [h, gain.reshape(1, D), wq, wk, wv]
    out_specs = [row(D), row(D), row(D)]
    out_shape = [jax.ShapeDtypeStruct((S, D), BF16)] * 3
    if rope:
        assert S % MOBA_BLOCK == 0 and tm % MOBA_BLOCK == 0
        half = ROPE_DIMS // 2
        inv = ROPE_THETA ** (-jnp.arange(half, dtype=F32) / half)
        dim = jnp.arange(LANES) % ATTN_DH
        inv_lane = jnp.where(dim < ROPE_DIMS, inv[dim % half], 0.0).astype(F32).reshape(1, LANES)
        in_specs += [row(1), _const_spec((1, LANES))]
        args += [positions.reshape(S, 1), inv_lane]
        nblk = tm // MOBA_BLOCK
        vt_rows = ATTN_HEADS * VT_ROWS
        out_specs[2] = pl.BlockSpec((nblk, vt_rows, MOBA_BLOCK), lambda i: (i, 0, 0))
        out_shape[2] = jax.ShapeDtypeStruct((S // MOBA_BLOCK, vt_rows, MOBA_BLOCK), BF16)
        out_specs.append(pl.BlockSpec((1, nblk, D), lambda i: (i, 0, 0)))
        out_shape.append(jax.ShapeDtypeStruct((S // tm, nblk, D), F32))
    outs = pl.pallas_call(
        functools.partial(_qkv_proj_kernel, rope=rope),
        grid=(S // tm,),
        in_specs=in_specs,
        out_specs=out_specs,
        out_shape=out_shape,
        compiler_params=_cparams("arbitrary"),
        name="moba_proj" if rope else "sb_proj",
    )(*args)
    if rope:
        q, k, v, kmean = outs
        return q, k, v, kmean.reshape(S // MOBA_BLOCK, D)
    return outs


def _split_pair(q):
    lane = lax.broadcasted_iota(jnp.int32, q.shape, 1)
    zero = jnp.zeros_like(q)
    return jnp.where(lane < ATTN_DH, q, zero), jnp.where(lane >= ATTN_DH, q, zero)


def _merge_pair(acc_a, acc_b):
    lane = lax.broadcasted_iota(jnp.int32, acc_a.shape, 1)
    return jnp.where(lane < ATTN_DH, acc_a, acc_b)


def _moba_select(gate, cur):
    nb = gate.shape[0]
    blk = lax.broadcasted_iota(jnp.int32, gate.shape, 0)
    valid = blk < cur
    g = jnp.where(valid, gate, -jnp.inf)
    sel = jnp.zeros(gate.shape, F32)
    for _ in range(MOBA_TOPK):
        mx = jnp.max(g, axis=0, keepdims=True)
        first = jnp.min(jnp.where(g == mx, blk, nb), axis=0, keepdims=True)
        hit = (blk == first) & valid
        sel = jnp.where(hit, 1.0, sel)
        g = jnp.where(blk == first, -jnp.inf, g)
    return sel


def _moba_attn_kernel(q_ref, k_ref, vt_ref, kmean_ref, out_ref,
                      sel_ref, s_ref, cm_ref, p_ref, alpha_ref, m_ref, acc_ref):
    T = ATTN_TILE
    H = ATTN_DH
    R = VT_ROWS
    TQ = q_ref.shape[0]
    first_own = pl.program_id(1) * (TQ // T)
    qt = q_ref[...].astype(F32).T
    dim = lax.broadcasted_iota(jnp.int32, qt.shape, 0)
    qts = [jnp.where(dim < H, qt, 0.0).astype(BF16), jnp.where(dim >= H, qt, 0.0).astype(BF16)]
    km = kmean_ref[...].astype(BF16)
    cur = first_own + lax.broadcasted_iota(jnp.int32, (1, TQ), 1) // T
    for x in range(2):
        sel_ref[x] = _moba_select(_dot(km, qts[x]), cur)
    key = lax.broadcasted_iota(jnp.int32, (T, T), 0)
    query = lax.broadcasted_iota(jnp.int32, (T, T), 1)
    causal = key <= query

    G = MOBA_GROUP
    nb = sel_ref.shape[1]

    for x in range(2):
        for h in range(TQ // T):
            own = first_own + h
            k_own = k_ref[pl.ds(pl.multiple_of(own * T, T), T), :]
            s = jnp.where(causal, _dot(k_own, qts[x][:, h * T:(h + 1) * T]), NEG)
            m = jnp.max(s, axis=0, keepdims=True)
            m_ref[x, :, h * T:(h + 1) * T] = m
            acc_ref[x, :, h * T:(h + 1) * T] = _dot(vt_ref[own, x * R:(x + 1) * R, :],
                                                    jnp.exp2(s - m).astype(BF16))

    def scores(t, slot):
        first = jnp.minimum(t * G, nb - G)
        k_g = k_ref[pl.ds(pl.multiple_of(first * T, T), G * T), :]
        for x in range(2):
            s = _dot(k_g, qts[x])
            s_ref[slot, x] = s
            for g in range(G):
                cm_ref[slot, x, g:g + 1, :] = jnp.max(s[g * T:(g + 1) * T, :], axis=0, keepdims=True)

    def fold(t, slot):
        for x in range(2):
            part = _dot(vt_ref[t * G, x * R:(x + 1) * R, :], p_ref[slot, x, 0:T, :])
            for g in range(1, G):
                part = part + _dot(vt_ref[t * G + g, x * R:(x + 1) * R, :], p_ref[slot, x, g * T:(g + 1) * T, :])
            acc_ref[x] = alpha_ref[slot, x] * acc_ref[x] + part

    @pl.when(first_own == 0)
    def _():
        p_ref[...] = jnp.zeros(p_ref.shape, BF16)
        alpha_ref[...] = jnp.ones(alpha_ref.shape, F32)

    scores(0, 0)

    def trip(t, slot):
        prev = 1 - slot
        live = t > 0
        gate = jnp.where(live, 1.0, 0.0)
        t_prev = jnp.maximum(t - 1, 0)
        first_next = jnp.minimum((t + 1) * G, nb - G)
        shifts = []
        for x in range(2):
            chosen = [sel_ref[x, pl.ds(t * G + g, 1), :] > 0.0 for g in range(G)]
            m = m_ref[x]
            m_new = m
            for g in range(G):
                m_new = jnp.maximum(m_new, jnp.where(chosen[g], cm_ref[slot, x, g:g + 1, :], NEG))
            m_ref[x] = m_new
            alpha_ref[slot, x] = jnp.exp2(m - m_new)
            shifts.append([jnp.where(chosen[g], m_new, BIG) for g in range(G)])
            acc_ref[x] = jnp.where(live, alpha_ref[prev, x], 1.0) * acc_ref[x]
        for g in range(G):
            rows = slice(g * T, (g + 1) * T)
            k_b = k_ref[pl.ds(pl.multiple_of((first_next + g) * T, T), T), :]
            for x in range(2):
                s_next = _dot(k_b, qts[x])
                s_ref[prev, x, rows, :] = s_next
                cm_ref[prev, x, g:g + 1, :] = jnp.max(s_next, axis=0, keepdims=True)
                p_ref[slot, x, rows, :] = jnp.exp2(s_ref[slot, x, rows, :] - shifts[x][g]).astype(BF16)
                acc_ref[x] += gate * _dot(vt_ref[t_prev * G + g, x * R:(x + 1) * R, :], p_ref[prev, x, rows, :])

    trips = (first_own + TQ // T - 1 + G - 1) // G

    def two_trips(u, _):
        trip(2 * u, 0)

        @pl.when(2 * u + 1 < trips)
        def _():
            trip(2 * u + 1, 1)
        return 0

    lax.fori_loop(0, (trips + 1) // 2, two_trips, 0)
    for slot in range(2):
        @pl.when(jnp.logical_and(trips > 0, (trips - 1) % 2 == slot))
        def _():
            fold(trips - 1, slot)
    out_t = jnp.concatenate([acc_ref[x][:H, :] / acc_ref[x][H:H + 1, :] for x in range(2)], axis=0)
    out_ref[...] = out_t.T.astype(BF16)


def _attn_specs(S):
    tile = pl.BlockSpec((ATTN_TILE, LANES), lambda p, i: (i, p))
    resident = pl.BlockSpec((S, LANES), lambda p, i: (0, p))
    return tile, resident


def _moba_attn(q, k, v, kmean):
    S, D = q.shape
    assert MOBA_BLOCK == ATTN_TILE and ATTN_TILE % Q_BLOCK == 0 and S % ATTN_TILE == 0
    nb = S // MOBA_BLOCK
    assert nb % MOBA_GROUP == 0
    tq = min(MOBA_QUERIES, S)
    assert tq % ATTN_TILE == 0 and S % tq == 0
    tile = pl.BlockSpec((tq, LANES), lambda p, i: (i, p))
    _, resident = _attn_specs(S)
    return pl.pallas_call(
        _moba_attn_kernel,
        grid=(D // LANES, S // tq),
        in_specs=[tile, resident, pl.BlockSpec((nb, 2 * VT_ROWS, MOBA_BLOCK), lambda p, i: (0, p, 0)),
                  pl.BlockSpec((nb, LANES), lambda p, i: (0, p))],
        out_specs=tile,
        out_shape=jax.ShapeDtypeStruct((S, D), BF16),
        scratch_shapes=[pltpu.VMEM((2, nb, tq), F32),
                        pltpu.VMEM((2, 2, MOBA_GROUP * ATTN_TILE, tq), F32),
                        pltpu.VMEM((2, 2, MOBA_GROUP, tq), F32),
                        pltpu.VMEM((2, 2, MOBA_GROUP * ATTN_TILE, tq), BF16),
                        pltpu.VMEM((2, 2, 1, tq), F32),
                        pltpu.VMEM((2, 1, tq), F32), pltpu.VMEM((2, VT_ROWS, tq), F32)],
        compiler_params=_cparams("arbitrary", "arbitrary"),
        name="moba_attn",
    )(q, k, v, kmean)


def _sb_attn_kernel(q_ref, k_ref, v_ref, out_ref):
    T = ATTN_TILE
    i = pl.program_id(1)
    qs = _split_pair(q_ref[...])
    key_row = lax.broadcasted_iota(jnp.int32, (T, T), 0)
    key_col = lax.broadcasted_iota(jnp.int32, (T, T), 1)
    later_keys = (key_row > key_col).astype(BF16)
    strict = key_col < key_row

    def walk(tiles, carry):
        heads = range(len(qs))
        k_t = [k_ref[pl.ds(pl.multiple_of(j * T, T), T), :] for j, _, _ in tiles]
        v_t = [v_ref[pl.ds(pl.multiple_of(j * T, T), T), :] for j, _, _ in tiles]
        z2 = [[_dot_nt(qs[x], k_j) for x in heads] for k_j in k_t]
        sp = [[jnp.maximum(z, 0.0) + jnp.log2(1.0 + jnp.exp2(-jnp.abs(z))) for z in zs] for zs in z2]
        sp = [[jnp.where(strict, s, 0.0) if masked else s for s in ss] for ss, (_, masked, _) in zip(sp, tiles)]
        hi = [[s.astype(BF16) for s in ss] for ss in sp]
        lo = [[(s - h.astype(F32)).astype(BF16) for s, h in zip(ss, hs)] for ss, hs in zip(sp, hi)]
        later = [[_dot(h, later_keys) + _dot(l, later_keys) for h, l in zip(hs, ls)] for hs, ls in zip(hi, lo)]
        run = [carry[2 * x] for x in heads]
        acc = [carry[2 * x + 1] for x in heads]
        for n, (_, masked, weight) in enumerate(tiles):
            for x in heads:
                a = jnp.exp2(z2[n][x] - sp[n][x] - (later[n][x] + run[x]))
                if masked:
                    a = jnp.where(strict, a, 0.0)
                sp_sum = jnp.sum(sp[n][x], axis=1, keepdims=True)
                av = _dot(a.astype(BF16), v_t[n])
                if weight is not None:
                    sp_sum, av = weight * sp_sum, weight * av
                run[x] = run[x] + sp_sum
                acc[x] = acc[x] + av
        return tuple(val for x in heads for val in (run[x], acc[x]))

    def least_run(carry):
        return jnp.min(jnp.minimum(carry[0], carry[2]))

    def more(state):
        j, least = state[0], state[1]
        return jnp.logical_and(j >= 0, least < SB_RUN_CUTOFF)

    def step(state):
        carry = walk([(state[0], False, None)], state[2:])
        return (state[0] - 1, least_run(carry)) + carry

    zero = (jnp.zeros((T, 1), F32), jnp.zeros((T, LANES), F32))
    carry = walk([(i, True, None), (jnp.maximum(i - 1, 0), False, jnp.where(i > 0, 1.0, 0.0))], zero + zero)
    state = lax.while_loop(more, step, (i - 2, least_run(carry)) + carry)
    out_ref[...] = _merge_pair(state[3], state[5]).astype(BF16)


def _sb_attn(q, k, v):
    S, D = q.shape
    assert ATTN_TILE % Q_BLOCK == 0 and S % ATTN_TILE == 0
    tile, resident = _attn_specs(S)
    return pl.pallas_call(
        _sb_attn_kernel,
        grid=(D // LANES, S // ATTN_TILE),
        in_specs=[tile, resident, resident],
        out_specs=tile,
        out_shape=jax.ShapeDtypeStruct((S, D), BF16),
        compiler_params=_cparams("arbitrary", "arbitrary"),
        name="sb_attn",
    )(q, k, v)


def _conv_proj_kernel(h_ref, g_ref, wb_ref, wc_ref, wu_ref, cw_ref, out_ref, tail_ref):
    tm = h_ref.shape[0]

    @pl.when(pl.program_id(0) == 0)
    def _():
        tail_ref[...] = jnp.zeros_like(tail_ref)

    a = _rms(h_ref[...], g_ref[...]).astype(BF16)
    z = _dot(a, wc_ref[...]) * _dot(a, wu_ref[...])
    row = lax.broadcasted_iota(jnp.int32, z.shape, 0)
    prev1 = tail_ref[7:8, :]
    prev2 = tail_ref[6:7, :]
    z1 = jnp.where(row == 0, prev1, pltpu.roll(z, 1, axis=0))
    z2 = jnp.where(row == 0, prev2, jnp.where(row == 1, prev1, pltpu.roll(z, 2, axis=0)))
    y = cw_ref[0:1, :] * z2 + cw_ref[1:2, :] * z1 + cw_ref[2:3, :] * z
    tail_ref[...] = z[tm - 8:, :]
    out_ref[...] = (_dot(a, wb_ref[...]) * y).astype(BF16)


def _conv_mixer_pre(h, gain, w_in, conv_w):
    S, D = h.shape
    assert conv_w.shape[0] == CONV_WIDTH == 3
    wb = w_in[:, :D].astype(BF16)
    wc = w_in[:, D:2 * D].astype(BF16)
    wu = w_in[:, 2 * D:].astype(BF16)
    cw = jnp.pad(conv_w.astype(F32), ((0, 8 - CONV_WIDTH), (0, 0)))
    tm = min(ROW_TILE, S)
    row = pl.BlockSpec((tm, D), lambda i: (i, 0))
    return pl.pallas_call(
        _conv_proj_kernel,
        grid=(S // tm,),
        in_specs=[row, _const_spec((1, D)), _const_spec((D, D)), _const_spec((D, D)), _const_spec((D, D)),
                  _const_spec((8, D))],
        out_specs=row,
        out_shape=jax.ShapeDtypeStruct((S, D), BF16),
        scratch_shapes=[pltpu.VMEM((8, D), F32)],
        compiler_params=_cparams("arbitrary"),
        name="conv_proj",
    )(h, gain.reshape(1, D), wb, wc, wu, cw)


def _post_kernel(*refs, final):
    if final:
        h_ref, x_ref, wo_ref, g_ref, wg_ref, wu_ref, wd_ref, fg_ref, out_ref = refs
    else:
        h_ref, x_ref, wo_ref, g_ref, wg_ref, wu_ref, wd_ref, out_ref = refs
    h1 = h_ref[...] + _dot(x_ref[...], wo_ref[...])
    a = _rms(h1, g_ref[...]).astype(BF16)
    fc = wg_ref.shape[1] // FFN_CHUNKS
    y = h1
    for c in range(FFN_CHUNKS):
        gate = _dot(a, wg_ref[:, c * fc:(c + 1) * fc])
        up = _dot(a, wu_ref[:, c * fc:(c + 1) * fc])
        act = (gate * jax.nn.sigmoid(gate) * up).astype(BF16)
        y = y + _dot(act, wd_ref[c * fc:(c + 1) * fc, :])
    if final:
        y = _rms(y, fg_ref[...])
    out_ref[...] = y


def _post(h, x, w_out, gain, w_gate, w_up, w_down, final_gain=None):
    S, D = h.shape
    K = x.shape[1]
    F = w_gate.shape[1]
    assert F % (FFN_CHUNKS * LANES) == 0
    final = final_gain is not None
    tm = min(ROW_TILE, S)
    in_specs = [pl.BlockSpec((tm, D), lambda i: (i, 0)), pl.BlockSpec((tm, K), lambda i: (i, 0)),
                _const_spec((K, D)), _const_spec((1, D)), _const_spec((D, F)), _const_spec((D, F)),
                _const_spec((F, D))]
    args = [h, x, w_out.astype(BF16), gain.reshape(1, D), w_gate.astype(BF16), w_up.astype(BF16),
            w_down.astype(BF16)]
    if final:
        in_specs.append(_const_spec((1, D)))
        args.append(final_gain.reshape(1, D))
    return pl.pallas_call(
        functools.partial(_post_kernel, final=final),
        grid=(S // tm,),
        in_specs=in_specs,
        out_specs=pl.BlockSpec((tm, D), lambda i: (i, 0)),
        out_shape=jax.ShapeDtypeStruct((S, D), F32),
        compiler_params=_cparams("arbitrary"),
        name="post_final" if final else "post",
    )(*args)


def _mixer(kind, j, h, gain, positions, p):
    if kind == 0:
        q, k, v, o, gates = _mlstm_proj(h, gain, p["mlstm_w_in"][j])
        return _mlstm_core(q, k, v, o, gates, p["mlstm_b_gate"][j], p["mlstm_head_gain"][j]), p["mlstm_w_out"][j]
    if kind == 1:
        q, k, v, kmean = _qkv_proj(h, gain, p["moba_w_qkv"][j], positions)
        return _moba_attn(q, k, v, kmean), p["moba_w_out"][j]
    if kind == 2:
        return _conv_mixer_pre(h, gain, p["conv_w_in"][j], p["conv_w"][j]), p["conv_w_out"][j]
    q, k, v = _qkv_proj(h, gain, p["sb_w_qkv"][j])
    return _sb_attn(q, k, v), p["sb_w_out"][j]


def kernel(x, positions, norm_gains, mlstm_w_in, mlstm_b_gate, mlstm_head_gain, mlstm_w_out, moba_w_qkv, moba_w_out, conv_w_in, conv_w, conv_w_out, sb_w_qkv, sb_w_out, ffn_w_gate, ffn_w_up, ffn_w_down, final_gain):
    B, S, D = x.shape
    assert D == D_MODEL
    depth = norm_gains.shape[0]
    p = dict(mlstm_w_in=mlstm_w_in, mlstm_b_gate=mlstm_b_gate, mlstm_head_gain=mlstm_head_gain,
             mlstm_w_out=mlstm_w_out, moba_w_qkv=moba_w_qkv, moba_w_out=moba_w_out,
             conv_w_in=conv_w_in, conv_w=conv_w, conv_w_out=conv_w_out,
             sb_w_qkv=sb_w_qkv, sb_w_out=sb_w_out)
    outs = []
    for b in range(B):
        h = x[b]
        for layer in range(depth):
            kind, j = layer % 4, layer // 4
            mix, w_out = _mixer(kind, j, h, norm_gains[layer, 0], positions[b], p)
            h = _post(h, mix, w_out, norm_gains[layer, 1], ffn_w_gate[layer], ffn_w_up[layer],
                      ffn_w_down[layer], final_gain if layer == depth - 1 else None)
        outs.append(h)
    return jnp.stack(outs)
```

```python
import functools

import jax
import jax.numpy as jnp
from jax import lax
from jax.experimental import pallas as pl
from jax.experimental.pallas import tpu as pltpu

F32 = jnp.float32
BF16 = jnp.bfloat16

EPS = 1e-6
NEG = -1e30
D_MODEL = 1024
MLSTM_HEADS = 4
MLSTM_DQK = 128
MLSTM_DV = 256
MLSTM_CHUNK = 128
ATTN_HEADS = 16
ATTN_DH = 64
ROPE_DIMS = 16
ROPE_THETA = 500000.0
MOBA_BLOCK = 256
MOBA_TOPK = 3
Q_BLOCK = 128
ATTN_TILE = 256
LOG2E = 1.4426950408889634
SB_RUN_CUTOFF = 150.0
VT_ROWS = 80
BIG = 1e30
MOBA_QUERIES = 512
MOBA_GROUP = 4
CONV_WIDTH = 3

LANES = 128
ROW_TILE = 512
MXU_WIDTH = 256
FFN_CHUNK = 6 * MXU_WIDTH
VMEM_LIMIT = 56 * 1024 * 1024


def _cparams(*sem):
    return pltpu.CompilerParams(dimension_semantics=sem, vmem_limit_bytes=VMEM_LIMIT)


def _const_spec(shape):
    nd = len(shape)
    return pl.BlockSpec(shape, lambda *_: (0,) * nd, pipeline_mode=pl.Buffered(1))


def _rms(x, g):
    return x * lax.rsqrt(jnp.mean(x * x, axis=-1, keepdims=True) + EPS) * g


def _dot(a, b):
    return jnp.dot(a, b, preferred_element_type=F32)


def _dot_nt(a, b):
    return lax.dot_general(a, b, (((1,), (1,)), ((), ())), preferred_element_type=F32)


def _dot_tn(a, b):
    return lax.dot_general(a, b, (((0,), (0,)), ((), ())), preferred_element_type=F32)


def _mlstm_proj_kernel(h_ref, g_ref, wq_ref, wk_ref, wv_ref, wo_ref, wg_ref,
                       q_ref, k_ref, v_ref, o_ref, gate_ref):
    a = _rms(h_ref[...], g_ref[...]).astype(BF16)
    q_ref[...] = (_dot(a, wq_ref[...]) * (MLSTM_DQK ** -0.5)).astype(BF16)
    k_ref[...] = _dot(a, wk_ref[...]).astype(BF16)
    v_ref[...] = _dot(a, wv_ref[...]).astype(BF16)
    o_ref[...] = _dot(a, wo_ref[...])
    gate_ref[...] = _dot(a, wg_ref[...])


def _mlstm_proj(h, gain, w_in):
    S, D = h.shape
    NH, DK, DV = MLSTM_HEADS, MLSTM_DQK, MLSTM_DV
    nq, nv = NH * DK, NH * DV
    wq = w_in[:, :nq].astype(BF16)
    wk = w_in[:, nq:2 * nq].astype(BF16)
    wv = w_in[:, 2 * nq:2 * nq + nv].astype(BF16)
    wo = w_in[:, 2 * nq + nv:2 * nq + 2 * nv].astype(BF16)
    wg = jnp.pad(w_in[:, 2 * nq + 2 * nv:], ((0, 0), (0, LANES - 2 * NH))).astype(BF16)
    tm = min(ROW_TILE, S)
    row = lambda n: pl.BlockSpec((tm, n), lambda i: (i, 0))
    return pl.pallas_call(
        _mlstm_proj_kernel,
        grid=(S // tm,),
        in_specs=[row(D), _const_spec((1, D)), _const_spec((D, nq)), _const_spec((D, nq)),
                  _const_spec((D, nv)), _const_spec((D, nv)), _const_spec((D, LANES))],
        out_specs=[row(nq), row(nq), row(nv), row(nv), row(LANES)],
        out_shape=[jax.ShapeDtypeStruct((S, nq), BF16), jax.ShapeDtypeStruct((S, nq), BF16),
                   jax.ShapeDtypeStruct((S, nv), BF16), jax.ShapeDtypeStruct((S, nv), F32),
                   jax.ShapeDtypeStruct((S, LANES), F32)],
        compiler_params=_cparams("arbitrary"),
        name="mlstm_proj",
    )(h, gain.reshape(1, D), wq, wk, wv, wo, wg)


def _mlstm_core_kernel(q_ref, k_ref, v_ref, o_ref, gate_ref, bg_ref, hg_ref, out_ref,
                       c_ref, n_ref, m_ref):
    NH, DK, DV, L = MLSTM_HEADS, MLSTM_DQK, MLSTM_DV, MLSTM_CHUNK

    @pl.when(pl.program_id(0) == 0)
    def _():
        c_ref[...] = jnp.zeros_like(c_ref)
        n_ref[...] = jnp.zeros_like(n_ref)
        m_ref[...] = jnp.zeros_like(m_ref)

    g = gate_ref[...] + bg_ref[...]
    lane = lax.broadcasted_iota(jnp.int32, (L, LANES), 1)
    log_sig = jnp.minimum(g, 0.0) - jnp.log(1.0 + jnp.exp(-jnp.abs(g)))
    gl = jnp.where(lane >= NH, log_sig, g)
    gl_t = gl.T
    t_idx = lax.broadcasted_iota(jnp.int32, (L, L), 0)
    s_idx = lax.broadcasted_iota(jnp.int32, (L, L), 1)
    causal = s_idx <= t_idx

    heads = range(NH)
    i_col = [gl[:, hd:hd + 1] for hd in heads]
    f_col = [gl[:, NH + hd:NH + hd + 1] for hd in heads]
    i_row = [gl_t[hd:hd + 1, :] for hd in heads]
    f_row = [gl_t[NH + hd:NH + hd + 1, :] for hd in heads]
    b_col = [jnp.sum(jnp.where(causal, f_row[hd], 0.0), axis=1, keepdims=True) for hd in heads]
    b_row = [jnp.sum(jnp.where(t_idx <= s_idx, f_col[hd], 0.0), axis=0, keepdims=True) for hd in heads]
    m_prev = [m_ref[hd] for hd in heads]
    dmat = [jnp.where(causal, b_col[hd] - b_row[hd] + i_row[hd], -jnp.inf) for hd in heads]
    inter = [b_col[hd] + m_prev[hd] for hd in heads]
    m_t = [jnp.maximum(inter[hd], jnp.max(dmat[hd], axis=1, keepdims=True)) for hd in heads]
    w_intra = [jnp.exp(dmat[hd] - m_t[hd]) for hd in heads]
    w_inter = [jnp.exp(inter[hd] - m_t[hd]) for hd in heads]

    q = [q_ref[:, hd * DK:(hd + 1) * DK] for hd in heads]
    k = [k_ref[:, hd * DK:(hd + 1) * DK] for hd in heads]
    v = [v_ref[:, hd * DV:(hd + 1) * DV] for hd in heads]
    c_old = [c_ref[hd] for hd in heads]
    n_old = [n_ref[hd] for hd in heads]
    s = [_dot_nt(q[hd], k[hd]) * w_intra[hd] for hd in heads]
    q_c = [_dot(q[hd], c_old[hd].astype(BF16)) for hd in heads]
    num = [_dot(s[hd].astype(BF16), v[hd]) + w_inter[hd] * q_c[hd] for hd in heads]
    qn = [jnp.sum(q[hd].astype(F32) * n_old[hd], axis=1, keepdims=True) for hd in heads]
    den = [jnp.sum(s[hd], axis=1, keepdims=True) + w_inter[hd] * qn[hd] for hd in heads]
    den = [jnp.maximum(jnp.abs(den[hd]), jnp.exp(-m_t[hd])) for hd in heads]
    h_out = [num[hd] / den[hd] for hd in heads]

    g_tot = [b_col[hd][L - 1:L, :] for hd in heads]
    a_col = [g_tot[hd] - b_col[hd] + i_col[hd] for hd in heads]
    m_new = [jnp.maximum(g_tot[hd] + m_prev[hd], jnp.max(a_col[hd], axis=0, keepdims=True)) for hd in heads]
    decay = [jnp.exp(g_tot[hd] + m_prev[hd] - m_new[hd]) for hd in heads]
    kw = [k[hd].astype(F32) * jnp.exp(a_col[hd] - m_new[hd]) for hd in heads]
    for hd in heads:
        c_ref[hd] = decay[hd] * c_old[hd] + _dot_tn(kw[hd].astype(BF16), v[hd])
        n_ref[hd] = decay[hd] * n_old[hd] + jnp.sum(kw[hd], axis=0, keepdims=True)
        m_ref[hd] = m_new[hd]

    for hd in heads:
        hn = h_out[hd] * lax.rsqrt(jnp.mean(h_out[hd] * h_out[hd], axis=1, keepdims=True) + EPS)
        hn = hn * hg_ref[:, hd * DV:(hd + 1) * DV]
        out = hn * jax.nn.sigmoid(o_ref[:, hd * DV:(hd + 1) * DV])
        out_ref[:, hd * DV:(hd + 1) * DV] = out.astype(BF16)


def _mlstm_core(q, k, v, o, gates, b_gate, head_gain):
    S = q.shape[0]
    NH, DK, DV, L = MLSTM_HEADS, MLSTM_DQK, MLSTM_DV, MLSTM_CHUNK
    bg = jnp.pad(b_gate.astype(F32), (0, LANES - 2 * NH)).reshape(1, LANES)
    row = lambda n: pl.BlockSpec((L, n), lambda c: (c, 0))
    return pl.pallas_call(
        _mlstm_core_kernel,
        grid=(S // L,),
        in_specs=[row(NH * DK), row(NH * DK), row(NH * DV), row(NH * DV), row(LANES),
                  _const_spec((1, LANES)), _const_spec((1, NH * DV))],
        out_specs=row(NH * DV),
        out_shape=jax.ShapeDtypeStruct((S, NH * DV), BF16),
        scratch_shapes=[pltpu.VMEM((NH, DK, DV), F32), pltpu.VMEM((NH, 1, DK), F32),
                        pltpu.VMEM((NH, 1, 1), F32)],
        compiler_params=_cparams("arbitrary"),
        name="mlstm_core",
    )(q, k, v, o, gates, bg, head_gain.astype(F32).reshape(1, NH * DV))


def _rope_tile(x, cos, sin_lo, sin_hi):
    half = ROPE_DIMS // 2
    cols = []
    for c in range(x.shape[1] // LANES):
        xc = x[:, c * LANES:(c + 1) * LANES]
        up = pltpu.roll(xc, LANES - half, axis=1)
        down = pltpu.roll(xc, half, axis=1)
        cols.append(xc * cos + up * sin_lo + down * sin_hi)
    return jnp.concatenate(cols, axis=1)


def _qkv_proj_kernel(*refs, rope):
    if rope:
        (h_ref, g_ref, wq_ref, wk_ref, wv_ref, pos_ref, inv_ref,
         q_ref, k_ref, v_ref, kmean_ref) = refs
    else:
        h_ref, g_ref, wq_ref, wk_ref, wv_ref, q_ref, k_ref, v_ref = refs
    a = _rms(h_ref[...], g_ref[...]).astype(BF16)
    q = _dot(a, wq_ref[...])
    k = _dot(a, wk_ref[...])
    v = _dot(a, wv_ref[...])
    if not rope:
        v_ref[...] = v.astype(BF16)
    q = q * (ATTN_DH ** -0.5 * LOG2E)
    if rope:
        tm = q.shape[0]
        row = lax.broadcasted_iota(jnp.int32, (VT_ROWS - ATTN_DH, MOBA_BLOCK), 0)
        ones_pad = jnp.where(row == 0, 1.0, 0.0).astype(BF16)
        for b in range(tm // MOBA_BLOCK):
            vt = v[b * MOBA_BLOCK:(b + 1) * MOBA_BLOCK, :].T.astype(BF16)
            for hd in range(ATTN_HEADS):
                v_ref[b, hd * VT_ROWS:hd * VT_ROWS + ATTN_DH, :] = vt[hd * ATTN_DH:(hd + 1) * ATTN_DH, :]
                v_ref[b, hd * VT_ROWS + ATTN_DH:(hd + 1) * VT_ROWS, :] = ones_pad
        half = ROPE_DIMS // 2
        ang = pos_ref[...].astype(F32) * inv_ref[...]
        cos = jnp.cos(ang)
        sin = jnp.sin(ang)
        dim = lax.broadcasted_iota(jnp.int32, (tm, LANES), 1) % ATTN_DH
        sin_lo = jnp.where(dim < half, -sin, 0.0)
        sin_hi = jnp.where((dim >= half) & (dim < ROPE_DIMS), sin, 0.0)
        q = _rope_tile(q, cos, sin_lo, sin_hi)
        k = _rope_tile(k, cos, sin_lo, sin_hi)
        nblk = tm // MOBA_BLOCK
        kmean_ref[0] = jnp.sum(k.reshape(nblk, MOBA_BLOCK, k.shape[1]), axis=1) * (1.0 / MOBA_BLOCK)
    q_ref[...] = q.astype(BF16)
    k_ref[...] = k.astype(BF16)


def _qkv_proj(h, gain, w_qkv, positions=None):
    S, D = h.shape
    rope = positions is not None
    wq = w_qkv[:, :D].astype(BF16)
    wk = w_qkv[:, D:2 * D].astype(BF16)
    wv = w_qkv[:, 2 * D:].astype(BF16)
    tm = min(ROW_TILE, S)
    row = lambda n: pl.BlockSpec((tm, n), lambda i: (i, 0))
    in_specs = [row(D), _const_spec((1, D)), _const_spec((D, D)), _const_spec((D, D)), _const_spec((D, D))]
    args = [h, gain.reshape(1, D), wq, wk, wv]
    out_specs = [row(D), row(D), row(D)]
    out_shape = [jax.ShapeDtypeStruct((S, D), BF16)] * 3
    if rope:
        assert S % MOBA_BLOCK == 0 and tm % MOBA_BLOCK == 0
        half = ROPE_DIMS // 2
        inv = ROPE_THETA ** (-jnp.arange(half, dtype=F32) / half)
        dim = jnp.arange(LANES) % ATTN_DH
        inv_lane = jnp.where(dim < ROPE_DIMS, inv[dim % half], 0.0).astype(F32).reshape(1, LANES)
        in_specs += [row(1), _const_spec((1, LANES))]
        args += [positions.reshape(S, 1), inv_lane]
        nblk = tm // MOBA_BLOCK
        vt_rows = ATTN_HEADS * VT_ROWS
        out_specs[2] = pl.BlockSpec((nblk, vt_rows, MOBA_BLOCK), lambda i: (i, 0, 0))
        out_shape[2] = jax.ShapeDtypeStruct((S // MOBA_BLOCK, vt_rows, MOBA_BLOCK), BF16)
        out_specs.append(pl.BlockSpec((1, nblk, D), lambda i: (i, 0, 0)))
        out_shape.append(jax.ShapeDtypeStruct((S // tm, nblk, D), F32))
    outs = pl.pallas_call(
        functools.partial(_qkv_proj_kernel, rope=rope),
        grid=(S // tm,),
        in_specs=in_specs,
        out_specs=out_specs,
        out_shape=out_shape,
        compiler_params=_cparams("arbitrary"),
        name="moba_proj" if rope else "sb_proj",
    )(*args)
    if rope:
        q, k, v, kmean = outs
        return q, k, v, kmean.reshape(S // MOBA_BLOCK, D)
    return outs


def _split_pair(q):
    lane = lax.broadcasted_iota(jnp.int32, q.shape, 1)
    zero = jnp.zeros_like(q)
    return jnp.where(lane < ATTN_DH, q, zero), jnp.where(lane >= ATTN_DH, q, zero)


def _merge_pair(acc_a, acc_b):
    lane = lax.broadcasted_iota(jnp.int32, acc_a.shape, 1)
    return jnp.where(lane < ATTN_DH, acc_a, acc_b)


def _moba_select(gate, cur):
    nb = gate.shape[0]
    blk = lax.broadcasted_iota(jnp.int32, gate.shape, 0)
    valid = blk < cur
    g = jnp.where(valid, gate, -jnp.inf)
    sel = jnp.zeros(gate.shape, F32)
    for _ in range(MOBA_TOPK):
        mx = jnp.max(g, axis=0, keepdims=True)
        first = jnp.min(jnp.where(g == mx, blk, nb), axis=0, keepdims=True)
        hit = (blk == first) & valid
        sel = jnp.where(hit, 1.0, sel)
        g = jnp.where(blk == first, -jnp.inf, g)
    return sel


def _moba_attn_kernel(q_ref, k_ref, vt_ref, kmean_ref, out_ref,
                      sel_ref, s_ref, cm_ref, p_ref, alpha_ref, m_ref, acc_ref):
    T = ATTN_TILE
    H = ATTN_DH
    R = VT_ROWS
    TQ = q_ref.shape[0]
    first_own = pl.program_id(1) * (TQ // T)

    @pl.when(first_own == 0)
    def _():
        p_ref[...] = jnp.zeros(p_ref.shape, BF16)
        alpha_ref[...] = jnp.ones(alpha_ref.shape, F32)

    qt = q_ref[...].astype(F32).T
    dim = lax.broadcasted_iota(jnp.int32, qt.shape, 0)
    qts = [jnp.where(dim < H, qt, 0.0).astype(BF16), jnp.where(dim >= H, qt, 0.0).astype(BF16)]
    km = kmean_ref[...].astype(BF16)
    cur = first_own + lax.broadcasted_iota(jnp.int32, (1, TQ), 1) // T
    key = lax.broadcasted_iota(jnp.int32, (T, T), 0)
    query = lax.broadcasted_iota(jnp.int32, (T, T), 1)
    causal = key <= query

    G = MOBA_GROUP

    def select(x):
        sel_ref[x] = _moba_select(_dot(km, qts[x]), cur)

    def own_block(x, h):
        own = first_own + h
        k_own = k_ref[pl.ds(pl.multiple_of(own * T, T), T), :]
        s = jnp.where(causal, _dot(k_own, qts[x][:, h * T:(h + 1) * T]), NEG)
        m = jnp.max(s, axis=0, keepdims=True)
        m_ref[x, :, h * T:(h + 1) * T] = m
        acc_ref[x, :, h * T:(h + 1) * T] = _dot(vt_ref[own, x * R:(x + 1) * R, :], jnp.exp2(s - m).astype(BF16))

    def score_block(t, slot, g, x):
        k_b = k_ref[pl.ds(pl.multiple_of((t * G + g) * T, T), T), :]
        s = _dot(k_b, qts[x])
        s_ref[slot, x, g * T:(g + 1) * T, :] = s
        cm_ref[slot, x, g:g + 1, :] = jnp.max(s, axis=0, keepdims=True)

    vpu_work = [lambda x=x: select(x) for x in range(2)]
    vpu_work += [lambda x=x, h=h: own_block(x, h) for x in range(2) for h in range(TQ // T)]
    mxu_work = [lambda g=g, x=x: score_block(0, 0, g, x) for g in range(G) for x in range(2)]
    for n in range(max(len(vpu_work), len(mxu_work))):
        if n < len(mxu_work):
            mxu_work[n]()
        if n < len(vpu_work):
            vpu_work[n]()

    def trip(t, slot, last):
        prev = 1 - slot
        live = t > 0
        gate = jnp.where(live, 1.0, 0.0)
        t_prev = jnp.maximum(t - 1, 0)
        shifts = []
        for x in range(2):
            chosen = [sel_ref[x, pl.ds(t * G + g, 1), :] > 0.0 for g in range(G)]
            m = m_ref[x]
            m_new = m
            for g in range(G):
                m_new = jnp.maximum(m_new, jnp.where(chosen[g], cm_ref[slot, x, g:g + 1, :], NEG))
            m_ref[x] = m_new
            alpha_ref[slot, x] = jnp.exp2(m - m_new)
            shifts.append([jnp.where(chosen[g], m_new, BIG) for g in range(G)])
            acc_ref[x] = jnp.where(live, alpha_ref[prev, x], 1.0) * acc_ref[x]
        for g in range(G):
            rows = slice(g * T, (g + 1) * T)
            for x in range(2):
                if not last:
                    score_block(t + 1, prev, g, x)
                p_ref[slot, x, rows, :] = jnp.exp2(s_ref[slot, x, rows, :] - shifts[x][g]).astype(BF16)
                acc_ref[x] += gate * _dot(vt_ref[t_prev * G + g, x * R:(x + 1) * R, :], p_ref[prev, x, rows, :])
        if last:
            for x in range(2):
                part = _dot(vt_ref[t * G, x * R:(x + 1) * R, :], p_ref[slot, x, 0:T, :])
                for g in range(1, G):
                    part = part + _dot(vt_ref[t * G + g, x * R:(x + 1) * R, :], p_ref[slot, x, g * T:(g + 1) * T, :])
                acc_ref[x] = alpha_ref[slot, x] * acc_ref[x] + part

    trips = (first_own + TQ // T - 1 + G - 1) // G

    def two_trips(u, _):
        for slot in range(2):
            t = 2 * u + slot
            for last in (False, True):
                @pl.when(t + 1 == trips if last else t + 1 < trips)
                def _():
                    trip(t, slot, last)
        return 0

    lax.fori_loop(0, (trips + 1) // 2, two_trips, 0)
    out_t = jnp.concatenate([acc_ref[x][:H, :] / acc_ref[x][H:H + 1, :] for x in range(2)], axis=0)
    out_ref[...] = out_t.T.astype(BF16)


def _attn_specs(S):
    tile = pl.BlockSpec((ATTN_TILE, LANES), lambda p, i: (i, p))
    resident = pl.BlockSpec((S, LANES), lambda p, i: (0, p))
    return tile, resident


def _moba_attn(q, k, v, kmean):
    S, D = q.shape
    assert MOBA_BLOCK == ATTN_TILE and ATTN_TILE % Q_BLOCK == 0 and S % ATTN_TILE == 0
    nb = S // MOBA_BLOCK
    assert nb % MOBA_GROUP == 0
    tq = min(MOBA_QUERIES, S)
    assert tq % ATTN_TILE == 0 and S % tq == 0
    tile = pl.BlockSpec((tq, LANES), lambda p, i: (i, p))
    _, resident = _attn_specs(S)
    return pl.pallas_call(
        _moba_attn_kernel,
        grid=(D // LANES, S // tq),
        in_specs=[tile, resident, pl.BlockSpec((nb, 2 * VT_ROWS, MOBA_BLOCK), lambda p, i: (0, p, 0)),
                  pl.BlockSpec((nb, LANES), lambda p, i: (0, p))],
        out_specs=tile,
        out_shape=jax.ShapeDtypeStruct((S, D), BF16),
        scratch_shapes=[pltpu.VMEM((2, nb, tq), F32),
                        pltpu.VMEM((2, 2, MOBA_GROUP * ATTN_TILE, tq), F32),
                        pltpu.VMEM((2, 2, MOBA_GROUP, tq), F32),
                        pltpu.VMEM((2, 2, MOBA_GROUP * ATTN_TILE, tq), BF16),
                        pltpu.VMEM((2, 2, 1, tq), F32),
                        pltpu.VMEM((2, 1, tq), F32), pltpu.VMEM((2, VT_ROWS, tq), F32)],
        compiler_params=_cparams("arbitrary", "arbitrary"),
        name="moba_attn",
    )(q, k, v, kmean)


def _sb_attn_kernel(q_ref, k_ref, v_ref, out_ref):
    T = ATTN_TILE
    i = pl.program_id(1)
    qs = _split_pair(q_ref[...])
    key_row = lax.broadcasted_iota(jnp.int32, (T, T), 0)
    key_col = lax.broadcasted_iota(jnp.int32, (T, T), 1)
    later_keys = (key_row > key_col).astype(BF16)
    strict = key_col < key_row

    def walk(tiles, carry):
        heads = range(len(qs))
        k_t = [k_ref[pl.ds(pl.multiple_of(j * T, T), T), :] for j, _, _ in tiles]
        v_t = [v_ref[pl.ds(pl.multiple_of(j * T, T), T), :] for j, _, _ in tiles]
        z2 = [[_dot_nt(qs[x], k_j) for x in heads] for k_j in k_t]
        sp = [[jnp.maximum(z, 0.0) + jnp.log2(1.0 + jnp.exp2(-jnp.abs(z))) for z in zs] for zs in z2]
        sp = [[jnp.where(strict, s, 0.0) if masked else s for s in ss] for ss, (_, masked, _) in zip(sp, tiles)]
        hi = [[s.astype(BF16) for s in ss] for ss in sp]
        lo = [[(s - h.astype(F32)).astype(BF16) for s, h in zip(ss, hs)] for ss, hs in zip(sp, hi)]
        later = [[_dot(h, later_keys) + _dot(l, later_keys) for h, l in zip(hs, ls)] for hs, ls in zip(hi, lo)]
        run = [carry[2 * x] for x in heads]
        acc = [carry[2 * x + 1] for x in heads]
        for n, (_, masked, weight) in enumerate(tiles):
            for x in heads:
                a = jnp.exp2(z2[n][x] - sp[n][x] - (later[n][x] + run[x]))
                if masked:
                    a = jnp.where(strict, a, 0.0)
                sp_sum = jnp.sum(sp[n][x], axis=1, keepdims=True)
                av = _dot(a.astype(BF16), v_t[n])
                if weight is not None:
                    sp_sum, av = weight * sp_sum, weight * av
                run[x] = run[x] + sp_sum
                acc[x] = acc[x] + av
        return tuple(val for x in heads for val in (run[x], acc[x]))

    def least_run(carry):
        return jnp.min(jnp.minimum(carry[0], carry[2]))

    def more(state):
        j, least = state[0], state[1]
        return jnp.logical_and(j >= 0, least < SB_RUN_CUTOFF)

    def step(state):
        carry = walk([(state[0], False, None)], state[2:])
        return (state[0] - 1, least_run(carry)) + carry

    zero = (jnp.zeros((T, 1), F32), jnp.zeros((T, LANES), F32))
    carry = walk([(i, True, None), (jnp.maximum(i - 1, 0), False, jnp.where(i > 0, 1.0, 0.0))], zero + zero)
    state = lax.while_loop(more, step, (i - 2, least_run(carry)) + carry)
    out_ref[...] = _merge_pair(state[3], state[5]).astype(BF16)


def _sb_attn(q, k, v):
    S, D = q.shape
    assert ATTN_TILE % Q_BLOCK == 0 and S % ATTN_TILE == 0
    tile, resident = _attn_specs(S)
    return pl.pallas_call(
        _sb_attn_kernel,
        grid=(D // LANES, S // ATTN_TILE),
        in_specs=[tile, resident, resident],
        out_specs=tile,
        out_shape=jax.ShapeDtypeStruct((S, D), BF16),
        compiler_params=_cparams("arbitrary", "arbitrary"),
        name="sb_attn",
    )(q, k, v)


def _conv_proj_kernel(h_ref, g_ref, wb_ref, wc_ref, wu_ref, cw_ref, out_ref, tail_ref):
    tm = h_ref.shape[0]

    @pl.when(pl.program_id(0) == 0)
    def _():
        tail_ref[...] = jnp.zeros_like(tail_ref)

    a = _rms(h_ref[...], g_ref[...]).astype(BF16)
    z = _dot(a, wc_ref[...]) * _dot(a, wu_ref[...])
    row = lax.broadcasted_iota(jnp.int32, z.shape, 0)
    prev1 = tail_ref[7:8, :]
    prev2 = tail_ref[6:7, :]
    z1 = jnp.where(row == 0, prev1, pltpu.roll(z, 1, axis=0))
    z2 = jnp.where(row == 0, prev2, jnp.where(row == 1, prev1, pltpu.roll(z, 2, axis=0)))
    y = cw_ref[0:1, :] * z2 + cw_ref[1:2, :] * z1 + cw_ref[2:3, :] * z
    tail_ref[...] = z[tm - 8:, :]
    out_ref[...] = (_dot(a, wb_ref[...]) * y).astype(BF16)


def _conv_mixer_pre(h, gain, w_in, conv_w):
    S, D = h.shape
    assert conv_w.shape[0] == CONV_WIDTH == 3
    wb = w_in[:, :D].astype(BF16)
    wc = w_in[:, D:2 * D].astype(BF16)
    wu = w_in[:, 2 * D:].astype(BF16)
    cw = jnp.pad(conv_w.astype(F32), ((0, 8 - CONV_WIDTH), (0, 0)))
    tm = min(ROW_TILE, S)
    row = pl.BlockSpec((tm, D), lambda i: (i, 0))
    return pl.pallas_call(
        _conv_proj_kernel,
        grid=(S // tm,),
        in_specs=[row, _const_spec((1, D)), _const_spec((D, D)), _const_spec((D, D)), _const_spec((D, D)),
                  _const_spec((8, D))],
        out_specs=row,
        out_shape=jax.ShapeDtypeStruct((S, D), BF16),
        scratch_shapes=[pltpu.VMEM((8, D), F32)],
        compiler_params=_cparams("arbitrary"),
        name="conv_proj",
    )(h, gain.reshape(1, D), wb, wc, wu, cw)


def _post_kernel(*refs, final):
    if final:
        h_ref, x_ref, wo_ref, g_ref, wg_ref, wu_ref, wd_ref, fg_ref, out_ref = refs
    else:
        h_ref, x_ref, wo_ref, g_ref, wg_ref, wu_ref, wd_ref, out_ref = refs
    h1 = h_ref[...] + _dot(x_ref[...], wo_ref[...])
    a = _rms(h1, g_ref[...]).astype(BF16)
    F = wg_ref.shape[1]
    y = h1
    for lo in range(0, F, FFN_CHUNK):
        hi = min(lo + FFN_CHUNK, F)
        gate = _dot(a, wg_ref[:, lo:hi])
        up = _dot(a, wu_ref[:, lo:hi])
        act = (gate * jax.nn.sigmoid(gate) * up).astype(BF16)
        y = y + _dot(act, wd_ref[lo:hi, :])
    if final:
        y = _rms(y, fg_ref[...])
    out_ref[...] = y


def _post(h, x, w_out, gain, w_gate, w_up, w_down, final_gain=None):
    S, D = h.shape
    K = x.shape[1]
    F = w_gate.shape[1]
    assert F % LANES == 0
    final = final_gain is not None
    tm = min(ROW_TILE, S)
    in_specs = [pl.BlockSpec((tm, D), lambda i: (i, 0)), pl.BlockSpec((tm, K), lambda i: (i, 0)),
                _const_spec((K, D)), _const_spec((1, D)), _const_spec((D, F)), _const_spec((D, F)),
                _const_spec((F, D))]
    args = [h, x, w_out.astype(BF16), gain.reshape(1, D), w_gate.astype(BF16), w_up.astype(BF16),
            w_down.astype(BF16)]
    if final:
        in_specs.append(_const_spec((1, D)))
        args.append(final_gain.reshape(1, D))
    return pl.pallas_call(
        functools.partial(_post_kernel, final=final),
        grid=(S // tm,),
        in_specs=in_specs,
        out_specs=pl.BlockSpec((tm, D), lambda i: (i, 0)),
        out_shape=jax.ShapeDtypeStruct((S, D), F32),
        compiler_params=_cparams("arbitrary"),
        name="post_final" if final else "post",
    )(*args)


def _mixer(kind, j, h, gain, positions, p):
    if kind == 0:
        q, k, v, o, gates = _mlstm_proj(h, gain, p["mlstm_w_in"][j])
        return _mlstm_core(q, k, v, o, gates, p["mlstm_b_gate"][j], p["mlstm_head_gain"][j]), p["mlstm_w_out"][j]
    if kind == 1:
        q, k, v, kmean = _qkv_proj(h, gain, p["moba_w_qkv"][j], positions)
        return _moba_attn(q, k, v, kmean), p["moba_w_out"][j]
    if kind == 2:
        return _conv_mixer_pre(h, gain, p["conv_w_in"][j], p["conv_w"][j]), p["conv_w_out"][j]
    q, k, v = _qkv_proj(h, gain, p["sb_w_qkv"][j])
    return _sb_attn(q, k, v), p["sb_w_out"][j]


def kernel(x, positions, norm_gains, mlstm_w_in, mlstm_b_gate, mlstm_head_gain, mlstm_w_out, moba_w_qkv, moba_w_out, conv_w_in, conv_w, conv_w_out, sb_w_qkv, sb_w_out, ffn_w_gate, ffn_w_up, ffn_w_down, final_gain):
    B, S, D = x.shape
    assert D == D_MODEL
    depth = norm_gains.shape[0]
    p = dict(mlstm_w_in=mlstm_w_in, mlstm_b_gate=mlstm_b_gate, mlstm_head_gain=mlstm_head_gain,
             mlstm_w_out=mlstm_w_out, moba_w_qkv=moba_w_qkv, moba_w_out=moba_w_out,
             conv_w_in=conv_w_in, conv_w=conv_w, conv_w_out=conv_w_out,
             sb_w_qkv=sb_w_qkv, sb_w_out=sb_w_out)
    outs = []
    for b in range(B):
        h = x[b]
        for layer in range(depth):
            kind, j = layer % 4, layer // 4
            mix, w_out = _mixer(kind, j, h, norm_gains[layer, 0], positions[b], p)
            h = _post(h, mix, w_out, norm_gains[layer, 1], ffn_w_gate[layer], ffn_w_up[layer],
                      ffn_w_down[layer], final_gain if layer == depth - 1 else None)
        outs.append(h)
    return jnp.stack(outs)
```

```python
import functools

import jax
import jax.numpy as jnp
from jax import lax
from jax.experimental import pallas as pl
from jax.experimental.pallas import tpu as pltpu

F32 = jnp.float32
BF16 = jnp.bfloat16

EPS = 1e-6
NEG = -1e30
D_MODEL = 1024
MLSTM_HEADS = 4
MLSTM_DQK = 128
MLSTM_DV = 256
MLSTM_CHUNK = 128
ATTN_HEADS = 16
ATTN_DH = 64
ROPE_DIMS = 16
ROPE_THETA = 500000.0
MOBA_BLOCK = 256
MOBA_TOPK = 3
Q_BLOCK = 128
ATTN_TILE = 256
LOG2E = 1.4426950408889634
SB_RUN_CUTOFF = 150.0
VT_ROWS = 80
BIG = 1e30
MOBA_QUERIES = 512
MOBA_GROUP = 4
CONV_WIDTH = 3

LANES = 128
ROW_TILE = 512
MXU_WIDTH = 256
FFN_CHUNK = 6 * MXU_WIDTH
VMEM_LIMIT = 56 * 1024 * 1024


def _cparams(*sem):
    return pltpu.CompilerParams(dimension_semantics=sem, vmem_limit_bytes=VMEM_LIMIT)


def _const_spec(shape):
    nd = len(shape)
    return pl.BlockSpec(shape, lambda *_: (0,) * nd, pipeline_mode=pl.Buffered(1))


def _rms(x, g):
    return x * lax.rsqrt(jnp.mean(x * x, axis=-1, keepdims=True) + EPS) * g


def _dot(a, b):
    return jnp.dot(a, b, preferred_element_type=F32)


def _dot_nt(a, b):
    return lax.dot_general(a, b, (((1,), (1,)), ((), ())), preferred_element_type=F32)


def _dot_tn(a, b):
    return lax.dot_general(a, b, (((0,), (0,)), ((), ())), preferred_element_type=F32)


def _mlstm_proj_kernel(h_ref, g_ref, wq_ref, wk_ref, wv_ref, wo_ref, wg_ref,
                       q_ref, k_ref, v_ref, o_ref, gate_ref):
    a = _rms(h_ref[...], g_ref[...]).astype(BF16)
    q_ref[...] = (_dot(a, wq_ref[...]) * (MLSTM_DQK ** -0.5)).astype(BF16)
    k_ref[...] = _dot(a, wk_ref[...]).astype(BF16)
    v_ref[...] = _dot(a, wv_ref[...]).astype(BF16)
    o_ref[...] = _dot(a, wo_ref[...])
    gate_ref[...] = _dot(a, wg_ref[...])


def _mlstm_proj(h, gain, w_in):
    S, D = h.shape
    NH, DK, DV = MLSTM_HEADS, MLSTM_DQK, MLSTM_DV
    nq, nv = NH * DK, NH * DV
    wq = w_in[:, :nq].astype(BF16)
    wk = w_in[:, nq:2 * nq].astype(BF16)
    wv = w_in[:, 2 * nq:2 * nq + nv].astype(BF16)
    wo = w_in[:, 2 * nq + nv:2 * nq + 2 * nv].astype(BF16)
    wg = jnp.pad(w_in[:, 2 * nq + 2 * nv:], ((0, 0), (0, LANES - 2 * NH))).astype(BF16)
    tm = min(ROW_TILE, S)
    row = lambda n: pl.BlockSpec((tm, n), lambda i: (i, 0))
    return pl.pallas_call(
        _mlstm_proj_kernel,
        grid=(S // tm,),
        in_specs=[row(D), _const_spec((1, D)), _const_spec((D, nq)), _const_spec((D, nq)),
                  _const_spec((D, nv)), _const_spec((D, nv)), _const_spec((D, LANES))],
        out_specs=[row(nq), row(nq), row(nv), row(nv), row(LANES)],
        out_shape=[jax.ShapeDtypeStruct((S, nq), BF16), jax.ShapeDtypeStruct((S, nq), BF16),
                   jax.ShapeDtypeStruct((S, nv), BF16), jax.ShapeDtypeStruct((S, nv), F32),
                   jax.ShapeDtypeStruct((S, LANES), F32)],
        compiler_params=_cparams("arbitrary"),
        name="mlstm_proj",
    )(h, gain.reshape(1, D), wq, wk, wv, wo, wg)


def _mlstm_core_kernel(q_ref, k_ref, v_ref, o_ref, gate_ref, bg_ref, hg_ref, out_ref,
                       c_ref, n_ref, m_ref):
    NH, DK, DV, L = MLSTM_HEADS, MLSTM_DQK, MLSTM_DV, MLSTM_CHUNK

    @pl.when(pl.program_id(0) == 0)
    def _():
        c_ref[...] = jnp.zeros_like(c_ref)
        n_ref[...] = jnp.zeros_like(n_ref)
        m_ref[...] = jnp.zeros_like(m_ref)

    g = gate_ref[...] + bg_ref[...]
    lane = lax.broadcasted_iota(jnp.int32, (L, LANES), 1)
    log_sig = jnp.minimum(g, 0.0) - jnp.log(1.0 + jnp.exp(-jnp.abs(g)))
    gl = jnp.where(lane >= NH, log_sig, g)
    gl_t = gl.T
    t_idx = lax.broadcasted_iota(jnp.int32, (L, L), 0)
    s_idx = lax.broadcasted_iota(jnp.int32, (L, L), 1)
    causal = s_idx <= t_idx

    heads = range(NH)
    i_col = [gl[:, hd:hd + 1] for hd in heads]
    f_col = [gl[:, NH + hd:NH + hd + 1] for hd in heads]
    i_row = [gl_t[hd:hd + 1, :] for hd in heads]
    f_row = [gl_t[NH + hd:NH + hd + 1, :] for hd in heads]
    b_col = [jnp.sum(jnp.where(causal, f_row[hd], 0.0), axis=1, keepdims=True) for hd in heads]
    b_row = [jnp.sum(jnp.where(t_idx <= s_idx, f_col[hd], 0.0), axis=0, keepdims=True) for hd in heads]
    m_prev = [m_ref[hd] for hd in heads]
    dmat = [jnp.where(causal, b_col[hd] - b_row[hd] + i_row[hd], -jnp.inf) for hd in heads]
    inter = [b_col[hd] + m_prev[hd] for hd in heads]
    m_t = [jnp.maximum(inter[hd], jnp.max(dmat[hd], axis=1, keepdims=True)) for hd in heads]
    w_intra = [jnp.exp(dmat[hd] - m_t[hd]) for hd in heads]
    w_inter = [jnp.exp(inter[hd] - m_t[hd]) for hd in heads]

    q = [q_ref[:, hd * DK:(hd + 1) * DK] for hd in heads]
    k = [k_ref[:, hd * DK:(hd + 1) * DK] for hd in heads]
    v = [v_ref[:, hd * DV:(hd + 1) * DV] for hd in heads]
    c_old = [c_ref[hd] for hd in heads]
    n_old = [n_ref[hd] for hd in heads]
    s = [_dot_nt(q[hd], k[hd]) * w_intra[hd] for hd in heads]
    q_c = [_dot(q[hd], c_old[hd].astype(BF16)) for hd in heads]
    num = [_dot(s[hd].astype(BF16), v[hd]) + w_inter[hd] * q_c[hd] for hd in heads]
    qn = [jnp.sum(q[hd].astype(F32) * n_old[hd], axis=1, keepdims=True) for hd in heads]
    den = [jnp.sum(s[hd], axis=1, keepdims=True) + w_inter[hd] * qn[hd] for hd in heads]
    den = [jnp.maximum(jnp.abs(den[hd]), jnp.exp(-m_t[hd])) for hd in heads]
    h_out = [num[hd] / den[hd] for hd in heads]

    g_tot = [b_col[hd][L - 1:L, :] for hd in heads]
    a_col = [g_tot[hd] - b_col[hd] + i_col[hd] for hd in heads]
    m_new = [jnp.maximum(g_tot[hd] + m_prev[hd], jnp.max(a_col[hd], axis=0, keepdims=True)) for hd in heads]
    decay = [jnp.exp(g_tot[hd] + m_prev[hd] - m_new[hd]) for hd in heads]
    kw = [k[hd].astype(F32) * jnp.exp(a_col[hd] - m_new[hd]) for hd in heads]
    for hd in heads:
        c_ref[hd] = decay[hd] * c_old[hd] + _dot_tn(kw[hd].astype(BF16), v[hd])
        n_ref[hd] = decay[hd] * n_old[hd] + jnp.sum(kw[hd], axis=0, keepdims=True)
        m_ref[hd] = m_new[hd]

    for hd in heads:
        hn = h_out[hd] * lax.rsqrt(jnp.mean(h_out[hd] * h_out[hd], axis=1, keepdims=True) + EPS)
        hn = hn * hg_ref[:, hd * DV:(hd + 1) * DV]
        out = hn * jax.nn.sigmoid(o_ref[:, hd * DV:(hd + 1) * DV])
        out_ref[:, hd * DV:(hd + 1) * DV] = out.astype(BF16)


def _mlstm_core(q, k, v, o, gates, b_gate, head_gain):
    S = q.shape[0]
    NH, DK, DV, L = MLSTM_HEADS, MLSTM_DQK, MLSTM_DV, MLSTM_CHUNK
    bg = jnp.pad(b_gate.astype(F32), (0, LANES - 2 * NH)).reshape(1, LANES)
    row = lambda n: pl.BlockSpec((L, n), lambda c: (c, 0))
    return pl.pallas_call(
        _mlstm_core_kernel,
        grid=(S // L,),
        in_specs=[row(NH * DK), row(NH * DK), row(NH * DV), row(NH * DV), row(LANES),
                  _const_spec((1, LANES)), _const_spec((1, NH * DV))],
        out_specs=row(NH * DV),
        out_shape=jax.ShapeDtypeStruct((S, NH * DV), BF16),
        scratch_shapes=[pltpu.VMEM((NH, DK, DV), F32), pltpu.VMEM((NH, 1, DK), F32),
                        pltpu.VMEM((NH, 1, 1), F32)],
        compiler_params=_cparams("arbitrary"),
        name="mlstm_core",
    )(q, k, v, o, gates, bg, head_gain.astype(F32).reshape(1, NH * DV))


def _rope_tile(x, cos, sin_lo, sin_hi):
    half = ROPE_DIMS // 2
    cols = []
    for c in range(x.shape[1] // LANES):
        xc = x[:, c * LANES:(c + 1) * LANES]
        up = pltpu.roll(xc, LANES - half, axis=1)
        down = pltpu.roll(xc, half, axis=1)
        cols.append(xc * cos + up * sin_lo + down * sin_hi)
    return jnp.concatenate(cols, axis=1)


def _qkv_proj_kernel(*refs, rope):
    if rope:
        (h_ref, g_ref, wq_ref, wk_ref, wv_ref, pos_ref, inv_ref,
         q_ref, k_ref, v_ref, kmean_ref) = refs
    else:
        h_ref, g_ref, wq_ref, wk_ref, wv_ref, q_ref, k_ref, v_ref = refs
    a = _rms(h_ref[...], g_ref[...]).astype(BF16)
    q = _dot(a, wq_ref[...])
    k = _dot(a, wk_ref[...])
    v = _dot(a, wv_ref[...])
    if not rope:
        v_ref[...] = v.astype(BF16)
    q = q * (ATTN_DH ** -0.5 * LOG2E)
    if rope:
        tm = q.shape[0]
        row = lax.broadcasted_iota(jnp.int32, (VT_ROWS - ATTN_DH, MOBA_BLOCK), 0)
        ones_pad = jnp.where(row == 0, 1.0, 0.0).astype(BF16)
        for b in range(tm // MOBA_BLOCK):
            vt = v[b * MOBA_BLOCK:(b + 1) * MOBA_BLOCK, :].T.astype(BF16)
            for hd in range(ATTN_HEADS):
                v_ref[b, hd * VT_ROWS:hd * VT_ROWS + ATTN_DH, :] = vt[hd * ATTN_DH:(hd + 1) * ATTN_DH, :]
                v_ref[b, hd * VT_ROWS + ATTN_DH:(hd + 1) * VT_ROWS, :] = ones_pad
        half = ROPE_DIMS // 2
        ang = pos_ref[...].astype(F32) * inv_ref[...]
        cos = jnp.cos(ang)
        sin = jnp.sin(ang)
        dim = lax.broadcasted_iota(jnp.int32, (tm, LANES), 1) % ATTN_DH
        sin_lo = jnp.where(dim < half, -sin, 0.0)
        sin_hi = jnp.where((dim >= half) & (dim < ROPE_DIMS), sin, 0.0)
        q = _rope_tile(q, cos, sin_lo, sin_hi)
        k = _rope_tile(k, cos, sin_lo, sin_hi)
        nblk = tm // MOBA_BLOCK
        kmean_ref[0] = jnp.sum(k.reshape(nblk, MOBA_BLOCK, k.shape[1]), axis=1) * (1.0 / MOBA_BLOCK)
    q_ref[...] = q.astype(BF16)
    k_ref[...] = k.astype(BF16)


def _qkv_proj(h, gain, w_qkv, positions=None):
    S, D = h.shape
    rope = positions is not None
    wq = w_qkv[:, :D].astype(BF16)
    wk = w_qkv[:, D:2 * D].astype(BF16)
    wv = w_qkv[:, 2 * D:].astype(BF16)
    tm = min(ROW_TILE, S)
    row = lambda n: pl.BlockSpec((tm, n), lambda i: (i, 0))
    in_specs = [row(D), _const_spec((1, D)), _const_spec((D, D)), _const_spec((D, D)), _const_spec((D, D))]
    args = [h, gain.reshape(1, D), wq, wk, wv]
    out_specs = [row(D), row(D), row(D)]
    out_shape = [jax.ShapeDtypeStruct((S, D), BF16)] * 3
    if rope:
        assert S % MOBA_BLOCK == 0 and tm % MOBA_BLOCK == 0
        half = ROPE_DIMS // 2
        inv = ROPE_THETA ** (-jnp.arange(half, dtype=F32) / half)
        dim = jnp.arange(LANES) % ATTN_DH
        inv_lane = jnp.where(dim < ROPE_DIMS, inv[dim % half], 0.0).astype(F32).reshape(1, LANES)
        in_specs += [row(1), _const_spec((1, LANES))]
        args += [positions.reshape(S, 1), inv_lane]
        nblk = tm // MOBA_BLOCK
        vt_rows = ATTN_HEADS * VT_ROWS
        out_specs[2] = pl.BlockSpec((nblk, vt_rows, MOBA_BLOCK), lambda i: (i, 0, 0))
        out_shape[2] = jax.ShapeDtypeStruct((S // MOBA_BLOCK, vt_rows, MOBA_BLOCK), BF16)
        out_specs.append(pl.BlockSpec((1, nblk, D), lambda i: (i, 0, 0)))
        out_shape.append(jax.ShapeDtypeStruct((S // tm, nblk, D), F32))
    outs = pl.pallas_call(
        functools.partial(_qkv_proj_kernel, rope=rope),
        grid=(S // tm,),
        in_specs=in_specs,
        out_specs=out_specs,
        out_shape=out_shape,
        compiler_params=_cparams("arbitrary"),
        name="moba_proj" if rope else "sb_proj",
    )(*args)
    if rope:
        q, k, v, kmean = outs
        return q, k, v, kmean.reshape(S // MOBA_BLOCK, D)
    return outs


def _split_pair(q):
    lane = lax.broadcasted_iota(jnp.int32, q.shape, 1)
    zero = jnp.zeros_like(q)
    return jnp.where(lane < ATTN_DH, q, zero), jnp.where(lane >= ATTN_DH, q, zero)


def _merge_pair(acc_a, acc_b):
    lane = lax.broadcasted_iota(jnp.int32, acc_a.shape, 1)
    return jnp.where(lane < ATTN_DH, acc_a, acc_b)


def _moba_select(gate, cur):
    nb = gate.shape[0]
    blk = lax.broadcasted_iota(jnp.int32, gate.shape, 0)
    valid = blk < cur
    g = jnp.where(valid, gate, -jnp.inf)
    sel = jnp.zeros(gate.shape, F32)
    for _ in range(MOBA_TOPK):
        mx = jnp.max(g, axis=0, keepdims=True)
        first = jnp.min(jnp.where(g == mx, blk, nb), axis=0, keepdims=True)
        hit = (blk == first) & valid
        sel = jnp.where(hit, 1.0, sel)
        g = jnp.where(blk == first, -jnp.inf, g)
    return sel


def _moba_attn_kernel(q_ref, k_ref, vt_ref, kmean_ref, out_ref,
                      sel_ref, s_ref, cm_ref, p_ref, alpha_ref, m_ref, acc_ref):
    T = ATTN_TILE
    H = ATTN_DH
    R = VT_ROWS
    TQ = q_ref.shape[0]
    first_own = pl.program_id(1) * (TQ // T)

    @pl.when(first_own == 0)
    def _():
        p_ref[...] = jnp.zeros(p_ref.shape, BF16)
        alpha_ref[...] = jnp.ones(alpha_ref.shape, F32)

    qt = q_ref[...].astype(F32).T
    dim = lax.broadcasted_iota(jnp.int32, qt.shape, 0)
    qts = [jnp.where(dim < H, qt, 0.0).astype(BF16), jnp.where(dim >= H, qt, 0.0).astype(BF16)]
    km = kmean_ref[...].astype(BF16)
    cur = first_own + lax.broadcasted_iota(jnp.int32, (1, TQ), 1) // T
    key = lax.broadcasted_iota(jnp.int32, (T, T), 0)
    query = lax.broadcasted_iota(jnp.int32, (T, T), 1)
    causal = key <= query

    G = MOBA_GROUP
    nb = sel_ref.shape[1]

    def select(x):
        sel_ref[x] = _moba_select(_dot(km, qts[x]), cur)

    def own_block(x, h):
        own = first_own + h
        k_own = k_ref[pl.ds(pl.multiple_of(own * T, T), T), :]
        s = jnp.where(causal, _dot(k_own, qts[x][:, h * T:(h + 1) * T]), NEG)
        m = jnp.max(s, axis=0, keepdims=True)
        m_ref[x, :, h * T:(h + 1) * T] = m
        acc_ref[x, :, h * T:(h + 1) * T] = _dot(vt_ref[own, x * R:(x + 1) * R, :], jnp.exp2(s - m).astype(BF16))

    def score_block(t, slot, g, x):
        k_b = k_ref[pl.ds(pl.multiple_of((t * G + g) * T, T), T), :]
        s = _dot(k_b, qts[x])
        s_ref[slot, x, g * T:(g + 1) * T, :] = s
        cm_ref[slot, x, g:g + 1, :] = jnp.max(s, axis=0, keepdims=True)

    vpu_work = [lambda x=x: select(x) for x in range(2)]
    vpu_work += [lambda x=x, h=h: own_block(x, h) for x in range(2) for h in range(TQ // T)]
    mxu_work = [lambda g=g, x=x: score_block(0, 0, g, x) for g in range(G) for x in range(2)]
    for n in range(max(len(vpu_work), len(mxu_work))):
        if n < len(mxu_work):
            mxu_work[n]()
        if n < len(vpu_work):
            vpu_work[n]()

    def fold(t, slot):
        for x in range(2):
            part = _dot(vt_ref[t * G, x * R:(x + 1) * R, :], p_ref[slot, x, 0:T, :])
            for g in range(1, G):
                part = part + _dot(vt_ref[t * G + g, x * R:(x + 1) * R, :], p_ref[slot, x, g * T:(g + 1) * T, :])
            acc_ref[x] = alpha_ref[slot, x] * acc_ref[x] + part

    def trip(t, slot):
        prev = 1 - slot
        live = t > 0
        gate = jnp.where(live, 1.0, 0.0)
        t_prev = jnp.maximum(t - 1, 0)
        t_next = jnp.minimum(t + 1, nb // G - 1)
        shifts = []
        for x in range(2):
            chosen = [sel_ref[x, pl.ds(t * G + g, 1), :] > 0.0 for g in range(G)]
            m = m_ref[x]
            m_new = m
            for g in range(G):
                m_new = jnp.maximum(m_new, jnp.where(chosen[g], cm_ref[slot, x, g:g + 1, :], NEG))
            m_ref[x] = m_new
            alpha_ref[slot, x] = jnp.exp2(m - m_new)
            shifts.append([jnp.where(chosen[g], m_new, BIG) for g in range(G)])
            acc_ref[x] = jnp.where(live, alpha_ref[prev, x], 1.0) * acc_ref[x]
        for g in range(G):
            rows = slice(g * T, (g + 1) * T)
            for x in range(2):
                score_block(t_next, prev, g, x)
                p_ref[slot, x, rows, :] = jnp.exp2(s_ref[slot, x, rows, :] - shifts[x][g]).astype(BF16)
                acc_ref[x] += gate * _dot(vt_ref[t_prev * G + g, x * R:(x + 1) * R, :], p_ref[prev, x, rows, :])

    trips = (first_own + TQ // T - 1 + G - 1) // G

    def two_trips(u, _):
        trip(2 * u, 0)

        @pl.when(2 * u + 1 < trips)
        def _():
            trip(2 * u + 1, 1)
        return 0

    lax.fori_loop(0, (trips + 1) // 2, two_trips, 0)
    for slot in range(2):
        @pl.when(jnp.logical_and(trips > 0, (trips - 1) % 2 == slot))
        def _():
            fold(trips - 1, slot)
    out_t = jnp.concatenate([acc_ref[x][:H, :] / acc_ref[x][H:H + 1, :] for x in range(2)], axis=0)
    out_ref[...] = out_t.T.astype(BF16)


def _attn_specs(S):
    tile = pl.BlockSpec((ATTN_TILE, LANES), lambda p, i: (i, p))
    resident = pl.BlockSpec((S, LANES), lambda p, i: (0, p))
    return tile, resident


def _moba_attn(q, k, v, kmean):
    S, D = q.shape
    assert MOBA_BLOCK == ATTN_TILE and ATTN_TILE % Q_BLOCK == 0 and S % ATTN_TILE == 0
    nb = S // MOBA_BLOCK
    assert nb % MOBA_GROUP == 0
    tq = min(MOBA_QUERIES, S)
    assert tq % ATTN_TILE == 0 and S % tq == 0
    tile = pl.BlockSpec((tq, LANES), lambda p, i: (i, p))
    _, resident = _attn_specs(S)
    return pl.pallas_call(
        _moba_attn_kernel,
        grid=(D // LANES, S // tq),
        in_specs=[tile, resident, pl.BlockSpec((nb, 2 * VT_ROWS, MOBA_BLOCK), lambda p, i: (0, p, 0)),
                  pl.BlockSpec((nb, LANES), lambda p, i: (0, p))],
        out_specs=tile,
        out_shape=jax.ShapeDtypeStruct((S, D), BF16),
        scratch_shapes=[pltpu.VMEM((2, nb, tq), F32),
                        pltpu.VMEM((2, 2, MOBA_GROUP * ATTN_TILE, tq), F32),
                        pltpu.VMEM((2, 2, MOBA_GROUP, tq), F32),
                        pltpu.VMEM((2, 2, MOBA_GROUP * ATTN_TILE, tq), BF16),
                        pltpu.VMEM((2, 2, 1, tq), F32),
                        pltpu.VMEM((2, 1, tq), F32), pltpu.VMEM((2, VT_ROWS, tq), F32)],
        compiler_params=_cparams("arbitrary", "arbitrary"),
        name="moba_attn",
    )(q, k, v, kmean)


def _sb_attn_kernel(q_ref, k_ref, v_ref, out_ref):
    T = ATTN_TILE
    i = pl.program_id(1)
    qs = _split_pair(q_ref[...])
    key_row = lax.broadcasted_iota(jnp.int32, (T, T), 0)
    key_col = lax.broadcasted_iota(jnp.int32, (T, T), 1)
    later_keys = (key_row > key_col).astype(BF16)
    strict = key_col < key_row

    def walk(tiles, carry):
        heads = range(len(qs))
        k_t = [k_ref[pl.ds(pl.multiple_of(j * T, T), T), :] for j, _, _ in tiles]
        v_t = [v_ref[pl.ds(pl.multiple_of(j * T, T), T), :] for j, _, _ in tiles]
        z2 = [[_dot_nt(qs[x], k_j) for x in heads] for k_j in k_t]
        sp = [[jnp.maximum(z, 0.0) + jnp.log2(1.0 + jnp.exp2(-jnp.abs(z))) for z in zs] for zs in z2]
        sp = [[jnp.where(strict, s, 0.0) if masked else s for s in ss] for ss, (_, masked, _) in zip(sp, tiles)]
        hi = [[s.astype(BF16) for s in ss] for ss in sp]
        lo = [[(s - h.astype(F32)).astype(BF16) for s, h in zip(ss, hs)] for ss, hs in zip(sp, hi)]
        log_sig = [[z - s for z, s in zip(zs, ss)] for zs, ss in zip(z2, sp)]
        sp_sums = [[jnp.sum(s, axis=1, keepdims=True) for s in ss] for ss in sp]
        later = [[_dot(h, later_keys) + _dot(l, later_keys) for h, l in zip(hs, ls)] for hs, ls in zip(hi, lo)]
        run = [carry[2 * x] for x in heads]
        acc = [carry[2 * x + 1] for x in heads]
        for n, (_, masked, weight) in enumerate(tiles):
            for x in heads:
                a = jnp.exp2(log_sig[n][x] - (later[n][x] + run[x]))
                if masked:
                    a = jnp.where(strict, a, 0.0)
                sp_sum = sp_sums[n][x]
                av = _dot(a.astype(BF16), v_t[n])
                if weight is not None:
                    sp_sum, av = weight * sp_sum, weight * av
                run[x] = run[x] + sp_sum
                acc[x] = acc[x] + av
        return tuple(val for x in heads for val in (run[x], acc[x]))

    def least_run(carry):
        return jnp.min(jnp.minimum(carry[0], carry[2]))

    def more(state):
        j, least = state[0], state[1]
        return jnp.logical_and(j >= 0, least < SB_RUN_CUTOFF)

    def step(state):
        carry = walk([(state[0], False, None)], state[2:])
        return (state[0] - 1, least_run(carry)) + carry

    zero = (jnp.zeros((T, 1), F32), jnp.zeros((T, LANES), F32))
    carry = walk([(i, True, None), (jnp.maximum(i - 1, 0), False, jnp.where(i > 0, 1.0, 0.0))], zero + zero)
    state = lax.while_loop(more, step, (i - 2, least_run(carry)) + carry)
    out_ref[...] = _merge_pair(state[3], state[5]).astype(BF16)


def _sb_attn(q, k, v):
    S, D = q.shape
    assert ATTN_TILE % Q_BLOCK == 0 and S % ATTN_TILE == 0
    tile, resident = _attn_specs(S)
    return pl.pallas_call(
        _sb_attn_kernel,
        grid=(D // LANES, S // ATTN_TILE),
        in_specs=[tile, resident, resident],
        out_specs=tile,
        out_shape=jax.ShapeDtypeStruct((S, D), BF16),
        compiler_params=_cparams("arbitrary", "arbitrary"),
        name="sb_attn",
    )(q, k, v)


def _conv_proj_kernel(h_ref, g_ref, wb_ref, wc_ref, wu_ref, cw_ref, out_ref, tail_ref):
    tm = h_ref.shape[0]

    @pl.when(pl.program_id(0) == 0)
    def _():
        tail_ref[...] = jnp.zeros_like(tail_ref)

    a = _rms(h_ref[...], g_ref[...]).astype(BF16)
    z = _dot(a, wc_ref[...]) * _dot(a, wu_ref[...])
    row = lax.broadcasted_iota(jnp.int32, z.shape, 0)
    prev1 = tail_ref[7:8, :]
    prev2 = tail_ref[6:7, :]
    z1 = jnp.where(row == 0, prev1, pltpu.roll(z, 1, axis=0))
    z2 = jnp.where(row == 0, prev2, jnp.where(row == 1, prev1, pltpu.roll(z, 2, axis=0)))
    y = cw_ref[0:1, :] * z2 + cw_ref[1:2, :] * z1 + cw_ref[2:3, :] * z
    tail_ref[...] = z[tm - 8:, :]
    out_ref[...] = (_dot(a, wb_ref[...]) * y).astype(BF16)


def _conv_mixer_pre(h, gain, w_in, conv_w):
    S, D = h.shape
    assert conv_w.shape[0] == CONV_WIDTH == 3
    wb = w_in[:, :D].astype(BF16)
    wc = w_in[:, D:2 * D].astype(BF16)
    wu = w_in[:, 2 * D:].astype(BF16)
    cw = jnp.pad(conv_w.astype(F32), ((0, 8 - CONV_WIDTH), (0, 0)))
    tm = min(ROW_TILE, S)
    row = pl.BlockSpec((tm, D), lambda i: (i, 0))
    return pl.pallas_call(
        _conv_proj_kernel,
        grid=(S // tm,),
        in_specs=[row, _const_spec((1, D)), _const_spec((D, D)), _const_spec((D, D)), _const_spec((D, D)),
                  _const_spec((8, D))],
        out_specs=row,
        out_shape=jax.ShapeDtypeStruct((S, D), BF16),
        scratch_shapes=[pltpu.VMEM((8, D), F32)],
        compiler_params=_cparams("arbitrary"),
        name="conv_proj",
    )(h, gain.reshape(1, D), wb, wc, wu, cw)


def _post_kernel(*refs, final):
    if final:
        h_ref, x_ref, wo_ref, g_ref, wg_ref, wu_ref, wd_ref, fg_ref, out_ref = refs
    else:
        h_ref, x_ref, wo_ref, g_ref, wg_ref, wu_ref, wd_ref, out_ref = refs
    h1 = h_ref[...] + _dot(x_ref[...], wo_ref[...])
    a = _rms(h1, g_ref[...]).astype(BF16)
    F = wg_ref.shape[1]
    y = h1
    for lo in range(0, F, FFN_CHUNK):
        hi = min(lo + FFN_CHUNK, F)
        gate = _dot(a, wg_ref[:, lo:hi])
        up = _dot(a, wu_ref[:, lo:hi])
        act = (gate * jax.nn.sigmoid(gate) * up).astype(BF16)
        y = y + _dot(act, wd_ref[lo:hi, :])
    if final:
        y = _rms(y, fg_ref[...])
    out_ref[...] = y


def _post(h, x, w_out, gain, w_gate, w_up, w_down, final_gain=None):
    S, D = h.shape
    K = x.shape[1]
    F = w_gate.shape[1]
    assert F % LANES == 0
    final = final_gain is not None
    tm = min(ROW_TILE, S)
    in_specs = [pl.BlockSpec((tm, D), lambda i: (i, 0)), pl.BlockSpec((tm, K), lambda i: (i, 0)),
                _const_spec((K, D)), _const_spec((1, D)), _const_spec((D, F)), _const_spec((D, F)),
                _const_spec((F, D))]
    args = [h, x, w_out.astype(BF16), gain.reshape(1, D), w_gate.astype(BF16), w_up.astype(BF16),
            w_down.astype(BF16)]
    if final:
        in_specs.append(_const_spec((1, D)))
        args.append(final_gain.reshape(1, D))
    return pl.pallas_call(
        functools.partial(_post_kernel, final=final),
        grid=(S // tm,),
        in_specs=in_specs,
        out_specs=pl.BlockSpec((tm, D), lambda i: (i, 0)),
        out_shape=jax.ShapeDtypeStruct((S, D), F32),
        compiler_params=_cparams("arbitrary"),
        name="post_final" if final else "post",
    )(*args)


def _mixer(kind, j, h, gain, positions, p):
    if kind == 0:
        q, k, v, o, gates = _mlstm_proj(h, gain, p["mlstm_w_in"][j])
        return _mlstm_core(q, k, v, o, gates, p["mlstm_b_gate"][j], p["mlstm_head_gain"][j]), p["mlstm_w_out"][j]
    if kind == 1:
        q, k, v, kmean = _qkv_proj(h, gain, p["moba_w_qkv"][j], positions)
        return _moba_attn(q, k, v, kmean), p["moba_w_out"][j]
    if kind == 2:
        return _conv_mixer_pre(h, gain, p["conv_w_in"][j], p["conv_w"][j]), p["conv_w_out"][j]
    q, k, v = _qkv_proj(h, gain, p["sb_w_qkv"][j])
    return _sb_attn(q, k, v), p["sb_w_out"][j]


def kernel(x, positions, norm_gains, mlstm_w_in, mlstm_b_gate, mlstm_head_gain, mlstm_w_out, moba_w_qkv, moba_w_out, conv_w_in, conv_w, conv_w_out, sb_w_qkv, sb_w_out, ffn_w_gate, ffn_w_up, ffn_w_down, final_gain):
    B, S, D = x.shape
    assert D == D_MODEL
    depth = norm_gains.shape[0]
    p = dict(mlstm_w_in=mlstm_w_in, mlstm_b_gate=mlstm_b_gate, mlstm_head_gain=mlstm_head_gain,
             mlstm_w_out=mlstm_w_out, moba_w_qkv=moba_w_qkv, moba_w_out=moba_w_out,
             conv_w_in=conv_w_in, conv_w=conv_w, conv_w_out=conv_w_out,
             sb_w_qkv=sb_w_qkv, sb_w_out=sb_w_out)
    outs = []
    for b in range(B):
        h = x[b]
        for layer in range(depth):
            kind, j = layer % 4, layer // 4
            mix, w_out = _mixer(kind, j, h, norm_gains[layer, 0], positions[b], p)
            h = _post(h, mix, w_out, norm_gains[layer, 1], ffn_w_gate[layer], ffn_w_up[layer],
                      ffn_w_down[layer], final_gain if layer == depth - 1 else None)
        outs.append(h)
    return jnp.stack(outs)
```

```python
import functools

import jax
import jax.numpy as jnp
from jax import lax
from jax.experimental import pallas as pl
from jax.experimental.pallas import tpu as pltpu

F32 = jnp.float32
BF16 = jnp.bfloat16

EPS = 1e-6
NEG = -1e30
D_MODEL = 1024
MLSTM_HEADS = 4
MLSTM_DQK = 128
MLSTM_DV = 256
MLSTM_CHUNK = 128
ATTN_HEADS = 16
ATTN_DH = 64
ROPE_DIMS = 16
ROPE_THETA = 500000.0
MOBA_BLOCK = 256
MOBA_TOPK = 3
Q_BLOCK = 128
ATTN_TILE = 256
LOG2E = 1.4426950408889634
SB_TILE = 256
SB_NEAR_TILES = 1
SB_RUN_CUTOFF = 150.0
BF16_SUBLANES = 16
VT_ROWS = ATTN_DH + BF16_SUBLANES
BIG = 1e30
MOBA_QUERIES = 512
MOBA_GROUP = 4
CONV_WIDTH = 3

LANES = 128
ROW_TILE = 512
MXU_WIDTH = 256
FFN_CHUNK = 6 * MXU_WIDTH
VMEM_LIMIT = 56 * 1024 * 1024


def _cparams(*sem):
    return pltpu.CompilerParams(dimension_semantics=sem, vmem_limit_bytes=VMEM_LIMIT)


def _const_spec(shape):
    nd = len(shape)
    return pl.BlockSpec(shape, lambda *_: (0,) * nd, pipeline_mode=pl.Buffered(1))


def _rms(x, g):
    return x * lax.rsqrt(jnp.mean(x * x, axis=-1, keepdims=True) + EPS) * g


def _dot(a, b):
    return jnp.dot(a, b, preferred_element_type=F32)


def _dot_nt(a, b):
    return lax.dot_general(a, b, (((1,), (1,)), ((), ())), preferred_element_type=F32)


def _dot_tn(a, b):
    return lax.dot_general(a, b, (((0,), (0,)), ((), ())), preferred_element_type=F32)


def _mlstm_proj_kernel(h_ref, g_ref, wq_ref, wk_ref, wv_ref, wo_ref, wg_ref,
                       q_ref, k_ref, v_ref, o_ref, gate_ref):
    a = _rms(h_ref[...], g_ref[...]).astype(BF16)
    q_ref[...] = (_dot(a, wq_ref[...]) * (MLSTM_DQK ** -0.5)).astype(BF16)
    k_ref[...] = _dot(a, wk_ref[...]).astype(BF16)
    v_ref[...] = _dot(a, wv_ref[...]).astype(BF16)
    o_ref[...] = _dot(a, wo_ref[...])
    gate_ref[...] = _dot(a, wg_ref[...])


def _mlstm_proj(h, gain, w_in):
    S, D = h.shape
    NH, DK, DV = MLSTM_HEADS, MLSTM_DQK, MLSTM_DV
    nq, nv = NH * DK, NH * DV
    wq = w_in[:, :nq].astype(BF16)
    wk = w_in[:, nq:2 * nq].astype(BF16)
    wv = w_in[:, 2 * nq:2 * nq + nv].astype(BF16)
    wo = w_in[:, 2 * nq + nv:2 * nq + 2 * nv].astype(BF16)
    wg = jnp.pad(w_in[:, 2 * nq + 2 * nv:], ((0, 0), (0, LANES - 2 * NH))).astype(BF16)
    tm = min(ROW_TILE, S)
    row = lambda n: pl.BlockSpec((tm, n), lambda i: (i, 0))
    return pl.pallas_call(
        _mlstm_proj_kernel,
        grid=(S // tm,),
        in_specs=[row(D), _const_spec((1, D)), _const_spec((D, nq)), _const_spec((D, nq)),
                  _const_spec((D, nv)), _const_spec((D, nv)), _const_spec((D, LANES))],
        out_specs=[row(nq), row(nq), row(nv), row(nv), row(LANES)],
        out_shape=[jax.ShapeDtypeStruct((S, nq), BF16), jax.ShapeDtypeStruct((S, nq), BF16),
                   jax.ShapeDtypeStruct((S, nv), BF16), jax.ShapeDtypeStruct((S, nv), F32),
                   jax.ShapeDtypeStruct((S, LANES), F32)],
        compiler_params=_cparams("arbitrary"),
        name="mlstm_proj",
    )(h, gain.reshape(1, D), wq, wk, wv, wo, wg)


def _mlstm_core_kernel(q_ref, k_ref, v_ref, o_ref, gate_ref, bg_ref, hg_ref, out_ref,
                       c_ref, n_ref, m_ref):
    NH, DK, DV, L = MLSTM_HEADS, MLSTM_DQK, MLSTM_DV, MLSTM_CHUNK

    @pl.when(pl.program_id(0) == 0)
    def _():
        c_ref[...] = jnp.zeros_like(c_ref)
        n_ref[...] = jnp.zeros_like(n_ref)
        m_ref[...] = jnp.zeros_like(m_ref)

    g = gate_ref[...] + bg_ref[...]
    lane = lax.broadcasted_iota(jnp.int32, (L, LANES), 1)
    log_sig = jnp.minimum(g, 0.0) - jnp.log(1.0 + jnp.exp(-jnp.abs(g)))
    gl = jnp.where(lane >= NH, log_sig, g)
    gl_t = gl.T
    t_idx = lax.broadcasted_iota(jnp.int32, (L, L), 0)
    s_idx = lax.broadcasted_iota(jnp.int32, (L, L), 1)
    causal = s_idx <= t_idx

    heads = range(NH)
    i_col = [gl[:, hd:hd + 1] for hd in heads]
    f_col = [gl[:, NH + hd:NH + hd + 1] for hd in heads]
    i_row = [gl_t[hd:hd + 1, :] for hd in heads]
    f_row = [gl_t[NH + hd:NH + hd + 1, :] for hd in heads]
    b_col = [jnp.sum(jnp.where(causal, f_row[hd], 0.0), axis=1, keepdims=True) for hd in heads]
    b_row = [jnp.sum(jnp.where(t_idx <= s_idx, f_col[hd], 0.0), axis=0, keepdims=True) for hd in heads]
    m_prev = [m_ref[hd] for hd in heads]
    dmat = [jnp.where(causal, b_col[hd] - b_row[hd] + i_row[hd], -jnp.inf) for hd in heads]
    inter = [b_col[hd] + m_prev[hd] for hd in heads]
    m_t = [jnp.maximum(inter[hd], jnp.max(dmat[hd], axis=1, keepdims=True)) for hd in heads]
    w_intra = [jnp.exp(dmat[hd] - m_t[hd]) for hd in heads]
    w_inter = [jnp.exp(inter[hd] - m_t[hd]) for hd in heads]

    q = [q_ref[:, hd * DK:(hd + 1) * DK] for hd in heads]
    k = [k_ref[:, hd * DK:(hd + 1) * DK] for hd in heads]
    v = [v_ref[:, hd * DV:(hd + 1) * DV] for hd in heads]
    c_old = [c_ref[hd] for hd in heads]
    n_old = [n_ref[hd] for hd in heads]
    s = [_dot_nt(q[hd], k[hd]) * w_intra[hd] for hd in heads]
    q_c = [_dot(q[hd], c_old[hd].astype(BF16)) for hd in heads]
    num = [_dot(s[hd].astype(BF16), v[hd]) + w_inter[hd] * q_c[hd] for hd in heads]
    qn = [jnp.sum(q[hd].astype(F32) * n_old[hd], axis=1, keepdims=True) for hd in heads]
    den = [jnp.sum(s[hd], axis=1, keepdims=True) + w_inter[hd] * qn[hd] for hd in heads]
    den = [jnp.maximum(jnp.abs(den[hd]), jnp.exp(-m_t[hd])) for hd in heads]
    h_out = [num[hd] / den[hd] for hd in heads]

    g_tot = [b_col[hd][L - 1:L, :] for hd in heads]
    a_col = [g_tot[hd] - b_col[hd] + i_col[hd] for hd in heads]
    m_new = [jnp.maximum(g_tot[hd] + m_prev[hd], jnp.max(a_col[hd], axis=0, keepdims=True)) for hd in heads]
    decay = [jnp.exp(g_tot[hd] + m_prev[hd] - m_new[hd]) for hd in heads]
    kw = [k[hd].astype(F32) * jnp.exp(a_col[hd] - m_new[hd]) for hd in heads]
    for hd in heads:
        c_ref[hd] = decay[hd] * c_old[hd] + _dot_tn(kw[hd].astype(BF16), v[hd])
        n_ref[hd] = decay[hd] * n_old[hd] + jnp.sum(kw[hd], axis=0, keepdims=True)
        m_ref[hd] = m_new[hd]

    for hd in heads:
        hn = h_out[hd] * lax.rsqrt(jnp.mean(h_out[hd] * h_out[hd], axis=1, keepdims=True) + EPS)
        hn = hn * hg_ref[:, hd * DV:(hd + 1) * DV]
        out = hn * jax.nn.sigmoid(o_ref[:, hd * DV:(hd + 1) * DV])
        out_ref[:, hd * DV:(hd + 1) * DV] = out.astype(BF16)


def _mlstm_core(q, k, v, o, gates, b_gate, head_gain):
    S = q.shape[0]
    NH, DK, DV, L = MLSTM_HEADS, MLSTM_DQK, MLSTM_DV, MLSTM_CHUNK
    bg = jnp.pad(b_gate.astype(F32), (0, LANES - 2 * NH)).reshape(1, LANES)
    row = lambda n: pl.BlockSpec((L, n), lambda c: (c, 0))
    return pl.pallas_call(
        _mlstm_core_kernel,
        grid=(S // L,),
        in_specs=[row(NH * DK), row(NH * DK), row(NH * DV), row(NH * DV), row(LANES),
                  _const_spec((1, LANES)), _const_spec((1, NH * DV))],
        out_specs=row(NH * DV),
        out_shape=jax.ShapeDtypeStruct((S, NH * DV), BF16),
        scratch_shapes=[pltpu.VMEM((NH, DK, DV), F32), pltpu.VMEM((NH, 1, DK), F32),
                        pltpu.VMEM((NH, 1, 1), F32)],
        compiler_params=_cparams("arbitrary"),
        name="mlstm_core",
    )(q, k, v, o, gates, bg, head_gain.astype(F32).reshape(1, NH * DV))


def _rope_tile(x, cos, sin_lo, sin_hi):
    half = ROPE_DIMS // 2
    cols = []
    for c in range(x.shape[1] // LANES):
        xc = x[:, c * LANES:(c + 1) * LANES]
        up = pltpu.roll(xc, LANES - half, axis=1)
        down = pltpu.roll(xc, half, axis=1)
        cols.append(xc * cos + up * sin_lo + down * sin_hi)
    return jnp.concatenate(cols, axis=1)


def _qkv_proj_kernel(*refs, rope):
    if rope:
        (h_ref, g_ref, wq_ref, wk_ref, wv_ref, pos_ref, inv_ref,
         q_ref, k_ref, v_ref, kmean_ref) = refs
    else:
        h_ref, g_ref, wq_ref, wk_ref, wv_ref, q_ref, k_ref, v_ref = refs
    a = _rms(h_ref[...], g_ref[...]).astype(BF16)
    q = _dot(a, wq_ref[...])
    k = _dot(a, wk_ref[...])
    v = _dot(a, wv_ref[...])
    if not rope:
        v_ref[...] = v.astype(BF16)
    q = q * (ATTN_DH ** -0.5 * LOG2E)
    if rope:
        tm = q.shape[0]
        row = lax.broadcasted_iota(jnp.int32, (VT_ROWS - ATTN_DH, MOBA_BLOCK), 0)
        ones_pad = jnp.where(row == 0, 1.0, 0.0).astype(BF16)
        for b in range(tm // MOBA_BLOCK):
            vt = v[b * MOBA_BLOCK:(b + 1) * MOBA_BLOCK, :].T.astype(BF16)
            for hd in range(ATTN_HEADS):
                v_ref[b, hd * VT_ROWS:hd * VT_ROWS + ATTN_DH, :] = vt[hd * ATTN_DH:(hd + 1) * ATTN_DH, :]
                v_ref[b, hd * VT_ROWS + ATTN_DH:(hd + 1) * VT_ROWS, :] = ones_pad
        half = ROPE_DIMS // 2
        ang = pos_ref[...].astype(F32) * inv_ref[...]
        cos = jnp.cos(ang)
        sin = jnp.sin(ang)
        dim = lax.broadcasted_iota(jnp.int32, (tm, LANES), 1) % ATTN_DH
        sin_lo = jnp.where(dim < half, -sin, 0.0)
        sin_hi = jnp.where((dim >= half) & (dim < ROPE_DIMS), sin, 0.0)
        q = _rope_tile(q, cos, sin_lo, sin_hi)
        k = _rope_tile(k, cos, sin_lo, sin_hi)
        nblk = tm // MOBA_BLOCK
        kmean_ref[0] = jnp.sum(k.reshape(nblk, MOBA_BLOCK, k.shape[1]), axis=1) * (1.0 / MOBA_BLOCK)
    q_ref[...] = q.astype(BF16)
    k_ref[...] = k.astype(BF16)


def _qkv_proj(h, gain, w_qkv, positions=None):
    S, D = h.shape
    rope = positions is not None
    wq = w_qkv[:, :D].astype(BF16)
    wk = w_qkv[:, D:2 * D].astype(BF16)
    wv = w_qkv[:, 2 * D:].astype(BF16)
    tm = min(ROW_TILE, S)
    row = lambda n: pl.BlockSpec((tm, n), lambda i: (i, 0))
    in_specs = [row(D), _const_spec((1, D)), _const_spec((D, D)), _const_spec((D, D)), _const_spec((D, D))]
    args = [h, gain.reshape(1, D), wq, wk, wv]
    out_specs = [row(D), row(D), row(D)]
    out_shape = [jax.ShapeDtypeStruct((S, D), BF16)] * 3
    if rope:
        assert S % MOBA_BLOCK == 0 and tm % MOBA_BLOCK == 0
        half = ROPE_DIMS // 2
        inv = ROPE_THETA ** (-jnp.arange(half, dtype=F32) / half)
        dim = jnp.arange(LANES) % ATTN_DH
        inv_lane = jnp.where(dim < ROPE_DIMS, inv[dim % half], 0.0).astype(F32).reshape(1, LANES)
        in_specs += [row(1), _const_spec((1, LANES))]
        args += [positions.reshape(S, 1), inv_lane]
        nblk = tm // MOBA_BLOCK
        vt_rows = ATTN_HEADS * VT_ROWS
        out_specs[2] = pl.BlockSpec((nblk, vt_rows, MOBA_BLOCK), lambda i: (i, 0, 0))
        out_shape[2] = jax.ShapeDtypeStruct((S // MOBA_BLOCK, vt_rows, MOBA_BLOCK), BF16)
        out_specs.append(pl.BlockSpec((1, nblk, D), lambda i: (i, 0, 0)))
        out_shape.append(jax.ShapeDtypeStruct((S // tm, nblk, D), F32))
    outs = pl.pallas_call(
        functools.partial(_qkv_proj_kernel, rope=rope),
        grid=(S // tm,),
        in_specs=in_specs,
        out_specs=out_specs,
        out_shape=out_shape,
        compiler_params=_cparams("arbitrary"),
        name="moba_proj" if rope else "sb_proj",
    )(*args)
    if rope:
        q, k, v, kmean = outs
        return q, k, v, kmean.reshape(S // MOBA_BLOCK, D)
    return outs


def _split_pair(q):
    lane = lax.broadcasted_iota(jnp.int32, q.shape, 1)
    zero = jnp.zeros_like(q)
    return jnp.where(lane < ATTN_DH, q, zero), jnp.where(lane >= ATTN_DH, q, zero)


def _merge_pair(acc_a, acc_b):
    lane = lax.broadcasted_iota(jnp.int32, acc_a.shape, 1)
    return jnp.where(lane < ATTN_DH, acc_a, acc_b)


def _moba_select(gate, cur):
    nb = gate.shape[0]
    blk = lax.broadcasted_iota(jnp.int32, gate.shape, 0)
    valid = blk < cur
    g = jnp.where(valid, gate, -jnp.inf)
    sel = jnp.zeros(gate.shape, F32)
    for _ in range(MOBA_TOPK):
        mx = jnp.max(g, axis=0, keepdims=True)
        first = jnp.min(jnp.where(g == mx, blk, nb), axis=0, keepdims=True)
        hit = (blk == first) & valid
        sel = jnp.where(hit, 1.0, sel)
        g = jnp.where(blk == first, -jnp.inf, g)
    return sel


def _moba_attn_kernel(q_ref, k_ref, vt_ref, kmean_ref, out_ref,
                      sel_ref, s_ref, cm_ref, p_ref, alpha_ref, m_ref, acc_ref):
    T = ATTN_TILE
    H = ATTN_DH
    R = VT_ROWS
    TQ = q_ref.shape[0]
    first_own = pl.program_id(1) * (TQ // T)

    @pl.when(first_own == 0)
    def _():
        p_ref[...] = jnp.zeros(p_ref.shape, BF16)
        alpha_ref[...] = jnp.ones(alpha_ref.shape, F32)

    qt = q_ref[...].astype(F32).T
    dim = lax.broadcasted_iota(jnp.int32, qt.shape, 0)
    qts = [jnp.where(dim < H, qt, 0.0).astype(BF16), jnp.where(dim >= H, qt, 0.0).astype(BF16)]
    km = kmean_ref[...].astype(BF16)
    cur = first_own + lax.broadcasted_iota(jnp.int32, (1, TQ), 1) // T
    key = lax.broadcasted_iota(jnp.int32, (T, T), 0)
    query = lax.broadcasted_iota(jnp.int32, (T, T), 1)
    causal = key <= query

    G = MOBA_GROUP
    nb = sel_ref.shape[1]

    def select(x):
        sel_ref[x] = _moba_select(_dot(km, qts[x]), cur)

    def own_block(x, h):
        own = first_own + h
        k_own = k_ref[pl.ds(pl.multiple_of(own * T, T), T), :]
        s = jnp.where(causal, _dot(k_own, qts[x][:, h * T:(h + 1) * T]), NEG)
        m = jnp.max(s, axis=0, keepdims=True)
        m_ref[x, :, h * T:(h + 1) * T] = m
        acc_ref[x, :, h * T:(h + 1) * T] = _dot(vt_ref[own, x * R:(x + 1) * R, :], jnp.exp2(s - m).astype(BF16))

    def score_block(t, slot, g, x):
        k_b = k_ref[pl.ds(pl.multiple_of((t * G + g) * T, T), T), :]
        s = _dot(k_b, qts[x])
        s_ref[slot, x, g * T:(g + 1) * T, :] = s
        cm_ref[slot, x, g:g + 1, :] = jnp.max(s, axis=0, keepdims=True)

    vpu_work = [lambda x=x: select(x) for x in range(2)]
    vpu_work += [lambda x=x, h=h: own_block(x, h) for x in range(2) for h in range(TQ // T)]
    mxu_work = [lambda g=g, x=x: score_block(0, 0, g, x) for g in range(G) for x in range(2)]
    for n in range(max(len(vpu_work), len(mxu_work))):
        if n < len(mxu_work):
            mxu_work[n]()
        if n < len(vpu_work):
            vpu_work[n]()

    def fold(t, slot):
        for x in range(2):
            part = _dot(vt_ref[t * G, x * R:(x + 1) * R, :], p_ref[slot, x, 0:T, :])
            for g in range(1, G):
                part = part + _dot(vt_ref[t * G + g, x * R:(x + 1) * R, :], p_ref[slot, x, g * T:(g + 1) * T, :])
            acc_ref[x] = alpha_ref[slot, x] * acc_ref[x] + part

    def trip(t, slot):
        prev = 1 - slot
        live = t > 0
        gate = jnp.where(live, 1.0, 0.0)
        t_prev = jnp.maximum(t - 1, 0)
        t_next = jnp.minimum(t + 1, nb // G - 1)
        shifts = []
        for x in range(2):
            chosen = [sel_ref[x, pl.ds(t * G + g, 1), :] > 0.0 for g in range(G)]
            m = m_ref[x]
            m_new = m
            for g in range(G):
                m_new = jnp.maximum(m_new, jnp.where(chosen[g], cm_ref[slot, x, g:g + 1, :], NEG))
            m_ref[x] = m_new
            alpha_ref[slot, x] = jnp.exp2(m - m_new)
            shifts.append([jnp.where(chosen[g], m_new, BIG) for g in range(G)])
            acc_ref[x] = jnp.where(live, alpha_ref[prev, x], 1.0) * acc_ref[x]
        for g in range(G):
            rows = slice(g * T, (g + 1) * T)
            for x in range(2):
                score_block(t_next, prev, g, x)
                p_ref[slot, x, rows, :] = jnp.exp2(s_ref[slot, x, rows, :] - shifts[x][g]).astype(BF16)
                acc_ref[x] += gate * _dot(vt_ref[t_prev * G + g, x * R:(x + 1) * R, :], p_ref[prev, x, rows, :])

    trips = (first_own + TQ // T - 1 + G - 1) // G

    def two_trips(u, _):
        trip(2 * u, 0)

        @pl.when(2 * u + 1 < trips)
        def _():
            trip(2 * u + 1, 1)
        return 0

    lax.fori_loop(0, (trips + 1) // 2, two_trips, 0)
    for slot in range(2):
        @pl.when(jnp.logical_and(trips > 0, (trips - 1) % 2 == slot))
        def _():
            fold(trips - 1, slot)
    out_t = jnp.concatenate([acc_ref[x][:H, :] / acc_ref[x][H:H + 1, :] for x in range(2)], axis=0)
    out_ref[...] = out_t.T.astype(BF16)


def _attn_specs(S):
    tile = pl.BlockSpec((ATTN_TILE, LANES), lambda p, i: (i, p))
    resident = pl.BlockSpec((S, LANES), lambda p, i: (0, p))
    return tile, resident


def _moba_attn(q, k, v, kmean):
    S, D = q.shape
    assert MOBA_BLOCK == ATTN_TILE and ATTN_TILE % Q_BLOCK == 0 and S % ATTN_TILE == 0
    nb = S // MOBA_BLOCK
    assert nb % MOBA_GROUP == 0
    tq = min(MOBA_QUERIES, S)
    assert tq % ATTN_TILE == 0 and S % tq == 0
    tile = pl.BlockSpec((tq, LANES), lambda p, i: (i, p))
    _, resident = _attn_specs(S)
    return pl.pallas_call(
        _moba_attn_kernel,
        grid=(D // LANES, S // tq),
        in_specs=[tile, resident, pl.BlockSpec((nb, 2 * VT_ROWS, MOBA_BLOCK), lambda p, i: (0, p, 0)),
                  pl.BlockSpec((nb, LANES), lambda p, i: (0, p))],
        out_specs=tile,
        out_shape=jax.ShapeDtypeStruct((S, D), BF16),
        scratch_shapes=[pltpu.VMEM((2, nb, tq), F32),
                        pltpu.VMEM((2, 2, MOBA_GROUP * ATTN_TILE, tq), F32),
                        pltpu.VMEM((2, 2, MOBA_GROUP, tq), F32),
                        pltpu.VMEM((2, 2, MOBA_GROUP * ATTN_TILE, tq), BF16),
                        pltpu.VMEM((2, 2, 1, tq), F32),
                        pltpu.VMEM((2, 1, tq), F32), pltpu.VMEM((2, VT_ROWS, tq), F32)],
        compiler_params=_cparams("arbitrary", "arbitrary"),
        name="moba_attn",
    )(q, k, v, kmean)


def _sb_attn_kernel(q_ref, k_ref, v_ref, out_ref):
    T = SB_TILE
    i = pl.program_id(1)
    qs = _split_pair(q_ref[...])
    key_row = lax.broadcasted_iota(jnp.int32, (T, T), 0)
    key_col = lax.broadcasted_iota(jnp.int32, (T, T), 1)
    later_keys = (key_row > key_col).astype(BF16)
    strict = key_col < key_row

    def walk(tiles, carry):
        heads = range(len(qs))
        k_t = [k_ref[pl.ds(pl.multiple_of(j * T, T), T), :] for j, _, _ in tiles]
        v_t = [v_ref[pl.ds(pl.multiple_of(j * T, T), T), :] for j, _, _ in tiles]
        z2 = [[_dot_nt(qs[x], k_j) for x in heads] for k_j in k_t]
        sp = [[jnp.maximum(z, 0.0) + jnp.log2(1.0 + jnp.exp2(-jnp.abs(z))) for z in zs] for zs in z2]
        sp = [[jnp.where(strict, s, 0.0) if masked else s for s in ss] for ss, (_, masked, _) in zip(sp, tiles)]
        hi = [[s.astype(BF16) for s in ss] for ss in sp]
        lo = [[(s - h.astype(F32)).astype(BF16) for s, h in zip(ss, hs)] for ss, hs in zip(sp, hi)]
        log_sig = [[z - s for z, s in zip(zs, ss)] for zs, ss in zip(z2, sp)]
        sp_sums = [[jnp.sum(s, axis=1, keepdims=True) for s in ss] for ss in sp]
        later = [[_dot(h, later_keys) + _dot(l, later_keys) for h, l in zip(hs, ls)] for hs, ls in zip(hi, lo)]
        run = [carry[2 * x] for x in heads]
        acc = [carry[2 * x + 1] for x in heads]
        for n, (_, masked, weight) in enumerate(tiles):
            for x in heads:
                a = jnp.exp2(log_sig[n][x] - (later[n][x] + run[x]))
                if masked:
                    a = jnp.where(strict, a, 0.0)
                sp_sum = sp_sums[n][x]
                av = _dot(a.astype(BF16), v_t[n])
                if weight is not None:
                    sp_sum, av = weight * sp_sum, weight * av
                run[x] = run[x] + sp_sum
                acc[x] = acc[x] + av
        return tuple(val for x in heads for val in (run[x], acc[x]))

    def least_run(carry):
        return jnp.min(jnp.minimum(carry[0], carry[2]))

    def more(state):
        j, least = state[0], state[1]
        return jnp.logical_and(j >= 0, least < SB_RUN_CUTOFF)

    def step(state):
        carry = walk([(state[0], False, None)], state[2:])
        return (state[0] - 1, least_run(carry)) + carry

    zero = (jnp.zeros((T, 1), F32), jnp.zeros((T, LANES), F32))
    near = [(i, True, None)]
    near += [(jnp.maximum(i - n, 0), False, jnp.where(i >= n, 1.0, 0.0)) for n in range(1, SB_NEAR_TILES + 1)]
    carry = walk(near, zero + zero)
    state = lax.while_loop(more, step, (i - 1 - SB_NEAR_TILES, least_run(carry)) + carry)
    out_ref[...] = _merge_pair(state[3], state[5]).astype(BF16)


def _sb_attn(q, k, v):
    S, D = q.shape
    assert S % SB_TILE == 0
    _, resident = _attn_specs(S)
    tile = pl.BlockSpec((SB_TILE, LANES), lambda p, i: (i, p))
    return pl.pallas_call(
        _sb_attn_kernel,
        grid=(D // LANES, S // SB_TILE),
        in_specs=[tile, resident, resident],
        out_specs=tile,
        out_shape=jax.ShapeDtypeStruct((S, D), BF16),
        compiler_params=_cparams("arbitrary", "arbitrary"),
        name="sb_attn",
    )(q, k, v)


def _conv_proj_kernel(h_ref, g_ref, wb_ref, wc_ref, wu_ref, cw_ref, out_ref, tail_ref):
    tm = h_ref.shape[0]

    @pl.when(pl.program_id(0) == 0)
    def _():
        tail_ref[...] = jnp.zeros_like(tail_ref)

    a = _rms(h_ref[...], g_ref[...]).astype(BF16)
    z = _dot(a, wc_ref[...]) * _dot(a, wu_ref[...])
    row = lax.broadcasted_iota(jnp.int32, z.shape, 0)
    prev1 = tail_ref[7:8, :]
    prev2 = tail_ref[6:7, :]
    z1 = jnp.where(row == 0, prev1, pltpu.roll(z, 1, axis=0))
    z2 = jnp.where(row == 0, prev2, jnp.where(row == 1, prev1, pltpu.roll(z, 2, axis=0)))
    y = cw_ref[0:1, :] * z2 + cw_ref[1:2, :] * z1 + cw_ref[2:3, :] * z
    tail_ref[...] = z[tm - 8:, :]
    out_ref[...] = (_dot(a, wb_ref[...]) * y).astype(BF16)


def _conv_mixer_pre(h, gain, w_in, conv_w):
    S, D = h.shape
    assert conv_w.shape[0] == CONV_WIDTH == 3
    wb = w_in[:, :D].astype(BF16)
    wc = w_in[:, D:2 * D].astype(BF16)
    wu = w_in[:, 2 * D:].astype(BF16)
    cw = jnp.pad(conv_w.astype(F32), ((0, 8 - CONV_WIDTH), (0, 0)))
    tm = min(ROW_TILE, S)
    row = pl.BlockSpec((tm, D), lambda i: (i, 0))
    return pl.pallas_call(
        _conv_proj_kernel,
        grid=(S // tm,),
        in_specs=[row, _const_spec((1, D)), _const_spec((D, D)), _const_spec((D, D)), _const_spec((D, D)),
                  _const_spec((8, D))],
        out_specs=row,
        out_shape=jax.ShapeDtypeStruct((S, D), BF16),
        scratch_shapes=[pltpu.VMEM((8, D), F32)],
        compiler_params=_cparams("arbitrary"),
        name="conv_proj",
    )(h, gain.reshape(1, D), wb, wc, wu, cw)


def _post_kernel(*refs, final):
    if final:
        h_ref, x_ref, wo_ref, g_ref, wg_ref, wu_ref, wd_ref, fg_ref, out_ref = refs
    else:
        h_ref, x_ref, wo_ref, g_ref, wg_ref, wu_ref, wd_ref, out_ref = refs
    h1 = h_ref[...] + _dot(x_ref[...], wo_ref[...])
    a = _rms(h1, g_ref[...]).astype(BF16)
    F = wg_ref.shape[1]
    y = h1
    for lo in range(0, F, FFN_CHUNK):
        hi = min(lo + FFN_CHUNK, F)
        gate = _dot(a, wg_ref[:, lo:hi])
        up = _dot(a, wu_ref[:, lo:hi])
        act = (gate * jax.nn.sigmoid(gate) * up).astype(BF16)
        y = y + _dot(act, wd_ref[lo:hi, :])
    if final:
        y = _rms(y, fg_ref[...])
    out_ref[...] = y


def _post(h, x, w_out, gain, w_gate, w_up, w_down, final_gain=None):
    S, D = h.shape
    K = x.shape[1]
    F = w_gate.shape[1]
    assert F % LANES == 0
    final = final_gain is not None
    tm = min(ROW_TILE, S)
    in_specs = [pl.BlockSpec((tm, D), lambda i: (i, 0)), pl.BlockSpec((tm, K), lambda i: (i, 0)),
                _const_spec((K, D)), _const_spec((1, D)), _const_spec((D, F)), _const_spec((D, F)),
                _const_spec((F, D))]
    args = [h, x, w_out.astype(BF16), gain.reshape(1, D), w_gate.astype(BF16), w_up.astype(BF16),
            w_down.astype(BF16)]
    if final:
        in_specs.append(_const_spec((1, D)))
        args.append(final_gain.reshape(1, D))
    return pl.pallas_call(
        functools.partial(_post_kernel, final=final),
        grid=(S // tm,),
        in_specs=in_specs,
        out_specs=pl.BlockSpec((tm, D), lambda i: (i, 0)),
        out_shape=jax.ShapeDtypeStruct((S, D), F32),
        compiler_params=_cparams("arbitrary"),
        name="post_final" if final else "post",
    )(*args)


def _mixer(kind, j, h, gain, positions, p):
    if kind == 0:
        q, k, v, o, gates = _mlstm_proj(h, gain, p["mlstm_w_in"][j])
        return _mlstm_core(q, k, v, o, gates, p["mlstm_b_gate"][j], p["mlstm_head_gain"][j]), p["mlstm_w_out"][j]
    if kind == 1:
        q, k, v, kmean = _qkv_proj(h, gain, p["moba_w_qkv"][j], positions)
        return _moba_attn(q, k, v, kmean), p["moba_w_out"][j]
    if kind == 2:
        return _conv_mixer_pre(h, gain, p["conv_w_in"][j], p["conv_w"][j]), p["conv_w_out"][j]
    q, k, v = _qkv_proj(h, gain, p["sb_w_qkv"][j])
    return _sb_attn(q, k, v), p["sb_w_out"][j]


def kernel(x, positions, norm_gains, mlstm_w_in, mlstm_b_gate, mlstm_head_gain, mlstm_w_out, moba_w_qkv, moba_w_out, conv_w_in, conv_w, conv_w_out, sb_w_qkv, sb_w_out, ffn_w_gate, ffn_w_up, ffn_w_down, final_gain):
    B, S, D = x.shape
    assert D == D_MODEL
    depth = norm_gains.shape[0]
    p = dict(mlstm_w_in=mlstm_w_in, mlstm_b_gate=mlstm_b_gate, mlstm_head_gain=mlstm_head_gain,
             mlstm_w_out=mlstm_w_out, moba_w_qkv=moba_w_qkv, moba_w_out=moba_w_out,
             conv_w_in=conv_w_in, conv_w=conv_w, conv_w_out=conv_w_out,
             sb_w_qkv=sb_w_qkv, sb_w_out=sb_w_out)
    outs = []
    for b in range(B):
        h = x[b]
        for layer in range(depth):
            kind, j = layer % 4, layer // 4
            mix, w_out = _mixer(kind, j, h, norm_gains[layer, 0], positions[b], p)
            h = _post(h, mix, w_out, norm_gains[layer, 1], ffn_w_gate[layer], ffn_w_up[layer],
                      ffn_w_down[layer], final_gain if layer == depth - 1 else None)
        outs.append(h)
    return jnp.stack(outs)
```

```python
import functools

import jax
import jax.numpy as jnp
from jax import lax
from jax.experimental import pallas as pl
from jax.experimental.pallas import tpu as pltpu

F32 = jnp.float32
BF16 = jnp.bfloat16

EPS = 1e-6
NEG = -1e30
D_MODEL = 1024
MLSTM_HEADS = 4
MLSTM_DQK = 128
MLSTM_DV = 256
MLSTM_CHUNK = 128
ATTN_HEADS = 16
ATTN_DH = 64
ROPE_DIMS = 16
ROPE_THETA = 500000.0
MOBA_BLOCK = 256
MOBA_TOPK = 3
Q_BLOCK = 128
ATTN_TILE = 256
LOG2E = 1.4426950408889634
SB_TILE = 256
SB_NEAR_TILES = 1
SB_RUN_CUTOFF = 150.0
BF16_SUBLANES = 16
VT_ROWS = ATTN_DH + BF16_SUBLANES
BIG = 1e30
MOBA_QUERIES = 512
MOBA_GROUP = 4
CONV_WIDTH = 3

LANES = 128
ROW_TILE = 512
MXU_WIDTH = 256
FFN_CHUNK = 6 * MXU_WIDTH
VMEM_LIMIT = 56 * 1024 * 1024


def _cparams(*sem):
    return pltpu.CompilerParams(dimension_semantics=sem, vmem_limit_bytes=VMEM_LIMIT)


def _const_spec(shape):
    nd = len(shape)
    return pl.BlockSpec(shape, lambda *_: (0,) * nd, pipeline_mode=pl.Buffered(1))


def _col_spec(rows, cols, c):
    return pl.BlockSpec((rows, cols), lambda *_: (0, c), pipeline_mode=pl.Buffered(1))


def _rms(x, g):
    return x * lax.rsqrt(jnp.mean(x * x, axis=-1, keepdims=True) + EPS) * g


def _dot(a, b):
    return jnp.dot(a, b, preferred_element_type=F32)


def _dot_nt(a, b):
    return lax.dot_general(a, b, (((1,), (1,)), ((), ())), preferred_element_type=F32)


def _dot_tn(a, b):
    return lax.dot_general(a, b, (((0,), (0,)), ((), ())), preferred_element_type=F32)


def _mlstm_proj_kernel(h_ref, g_ref, wq_ref, wk_ref, wv_ref, wo_ref, wg_ref,
                       q_ref, k_ref, v_ref, o_ref, gate_ref):
    a = _rms(h_ref[...], g_ref[...]).astype(BF16)
    q_ref[...] = (_dot(a, wq_ref[...]) * (MLSTM_DQK ** -0.5)).astype(BF16)
    k_ref[...] = _dot(a, wk_ref[...]).astype(BF16)
    v_ref[...] = _dot(a, wv_ref[...]).astype(BF16)
    o_ref[...] = _dot(a, wo_ref[...])
    gate_ref[...] = _dot(a, wg_ref[...])


def _mlstm_proj(h, gain, w_in):
    S, D = h.shape
    NH, DK, DV = MLSTM_HEADS, MLSTM_DQK, MLSTM_DV
    nq, nv = NH * DK, NH * DV
    assert 2 * nq == nv
    w = w_in.astype(BF16)
    wg = jnp.pad(w_in[:, 2 * nq + 2 * nv:], ((0, 0), (0, LANES - 2 * NH))).astype(BF16)
    tm = min(ROW_TILE, S)
    row = lambda n: pl.BlockSpec((tm, n), lambda i: (i, 0))
    return pl.pallas_call(
        _mlstm_proj_kernel,
        grid=(S // tm,),
        in_specs=[row(D), _const_spec((1, D)), _col_spec(D, nq, 0), _col_spec(D, nq, 1),
                  _col_spec(D, nv, 1), _col_spec(D, nv, 2), _const_spec((D, LANES))],
        out_specs=[row(nq), row(nq), row(nv), row(nv), row(LANES)],
        out_shape=[jax.ShapeDtypeStruct((S, nq), BF16), jax.ShapeDtypeStruct((S, nq), BF16),
                   jax.ShapeDtypeStruct((S, nv), BF16), jax.ShapeDtypeStruct((S, nv), F32),
                   jax.ShapeDtypeStruct((S, LANES), F32)],
        compiler_params=_cparams("arbitrary"),
        name="mlstm_proj",
    )(h, gain.reshape(1, D), w, w, w, w, wg)


def _mlstm_core_kernel(q_ref, k_ref, v_ref, o_ref, gate_ref, bg_ref, hg_ref, out_ref,
                       c_ref, n_ref, m_ref):
    NH, DK, DV, L = MLSTM_HEADS, MLSTM_DQK, MLSTM_DV, MLSTM_CHUNK

    @pl.when(pl.program_id(0) == 0)
    def _():
        c_ref[...] = jnp.zeros_like(c_ref)
        n_ref[...] = jnp.zeros_like(n_ref)
        m_ref[...] = jnp.zeros_like(m_ref)

    g = gate_ref[...] + bg_ref[...]
    lane = lax.broadcasted_iota(jnp.int32, (L, LANES), 1)
    log_sig = jnp.minimum(g, 0.0) - jnp.log(1.0 + jnp.exp(-jnp.abs(g)))
    gl = jnp.where(lane >= NH, log_sig, g)
    gl_t = gl.T
    t_idx = lax.broadcasted_iota(jnp.int32, (L, L), 0)
    s_idx = lax.broadcasted_iota(jnp.int32, (L, L), 1)
    causal = s_idx <= t_idx

    heads = range(NH)
    i_col = [gl[:, hd:hd + 1] for hd in heads]
    f_col = [gl[:, NH + hd:NH + hd + 1] for hd in heads]
    i_row = [gl_t[hd:hd + 1, :] for hd in heads]
    f_row = [gl_t[NH + hd:NH + hd + 1, :] for hd in heads]
    b_col = [jnp.sum(jnp.where(causal, f_row[hd], 0.0), axis=1, keepdims=True) for hd in heads]
    b_row = [jnp.sum(jnp.where(t_idx <= s_idx, f_col[hd], 0.0), axis=0, keepdims=True) for hd in heads]
    m_prev = [m_ref[hd] for hd in heads]
    dmat = [jnp.where(causal, b_col[hd] - b_row[hd] + i_row[hd], -jnp.inf) for hd in heads]
    inter = [b_col[hd] + m_prev[hd] for hd in heads]
    m_t = [jnp.maximum(inter[hd], jnp.max(dmat[hd], axis=1, keepdims=True)) for hd in heads]
    w_intra = [jnp.exp(dmat[hd] - m_t[hd]) for hd in heads]
    w_inter = [jnp.exp(inter[hd] - m_t[hd]) for hd in heads]

    q = [q_ref[:, hd * DK:(hd + 1) * DK] for hd in heads]
    k = [k_ref[:, hd * DK:(hd + 1) * DK] for hd in heads]
    v = [v_ref[:, hd * DV:(hd + 1) * DV] for hd in heads]
    c_old = [c_ref[hd] for hd in heads]
    n_old = [n_ref[hd] for hd in heads]
    s = [_dot_nt(q[hd], k[hd]) * w_intra[hd] for hd in heads]
    q_c = [_dot(q[hd], c_old[hd].astype(BF16)) for hd in heads]
    num = [_dot(s[hd].astype(BF16), v[hd]) + w_inter[hd] * q_c[hd] for hd in heads]
    qn = [jnp.sum(q[hd].astype(F32) * n_old[hd], axis=1, keepdims=True) for hd in heads]
    den = [jnp.sum(s[hd], axis=1, keepdims=True) + w_inter[hd] * qn[hd] for hd in heads]
    den = [jnp.maximum(jnp.abs(den[hd]), jnp.exp(-m_t[hd])) for hd in heads]
    h_out = [num[hd] / den[hd] for hd in heads]

    g_tot = [b_col[hd][L - 1:L, :] for hd in heads]
    a_col = [g_tot[hd] - b_col[hd] + i_col[hd] for hd in heads]
    m_new = [jnp.maximum(g_tot[hd] + m_prev[hd], jnp.max(a_col[hd], axis=0, keepdims=True)) for hd in heads]
    decay = [jnp.exp(g_tot[hd] + m_prev[hd] - m_new[hd]) for hd in heads]
    kw = [k[hd].astype(F32) * jnp.exp(a_col[hd] - m_new[hd]) for hd in heads]
    for hd in heads:
        c_ref[hd] = decay[hd] * c_old[hd] + _dot_tn(kw[hd].astype(BF16), v[hd])
        n_ref[hd] = decay[hd] * n_old[hd] + jnp.sum(kw[hd], axis=0, keepdims=True)
        m_ref[hd] = m_new[hd]

    for hd in heads:
        hn = h_out[hd] * lax.rsqrt(jnp.mean(h_out[hd] * h_out[hd], axis=1, keepdims=True) + EPS)
        hn = hn * hg_ref[:, hd * DV:(hd + 1) * DV]
        out = hn * jax.nn.sigmoid(o_ref[:, hd * DV:(hd + 1) * DV])
        out_ref[:, hd * DV:(hd + 1) * DV] = out.astype(BF16)


def _mlstm_core(q, k, v, o, gates, b_gate, head_gain):
    S = q.shape[0]
    NH, DK, DV, L = MLSTM_HEADS, MLSTM_DQK, MLSTM_DV, MLSTM_CHUNK
    bg = jnp.pad(b_gate.astype(F32), (0, LANES - 2 * NH)).reshape(1, LANES)
    row = lambda n: pl.BlockSpec((L, n), lambda c: (c, 0))
    return pl.pallas_call(
        _mlstm_core_kernel,
        grid=(S // L,),
        in_specs=[row(NH * DK), row(NH * DK), row(NH * DV), row(NH * DV), row(LANES),
                  _const_spec((1, LANES)), _const_spec((1, NH * DV))],
        out_specs=row(NH * DV),
        out_shape=jax.ShapeDtypeStruct((S, NH * DV), BF16),
        scratch_shapes=[pltpu.VMEM((NH, DK, DV), F32), pltpu.VMEM((NH, 1, DK), F32),
                        pltpu.VMEM((NH, 1, 1), F32)],
        compiler_params=_cparams("arbitrary"),
        name="mlstm_core",
    )(q, k, v, o, gates, bg, head_gain.astype(F32).reshape(1, NH * DV))


def _rope_tile(x, cos, sin_lo, sin_hi):
    half = ROPE_DIMS // 2
    cols = []
    for c in range(x.shape[1] // LANES):
        xc = x[:, c * LANES:(c + 1) * LANES]
        up = pltpu.roll(xc, LANES - half, axis=1)
        down = pltpu.roll(xc, half, axis=1)
        cols.append(xc * cos + up * sin_lo + down * sin_hi)
    return jnp.concatenate(cols, axis=1)


def _qkv_proj_kernel(*refs, rope):
    if rope:
        (h_ref, g_ref, wq_ref, wk_ref, wv_ref, pos_ref, inv_ref,
         q_ref, k_ref, v_ref, kmean_ref) = refs
    else:
        h_ref, g_ref, wq_ref, wk_ref, wv_ref, q_ref, k_ref, v_ref = refs
    a = _rms(h_ref[...], g_ref[...]).astype(BF16)
    q = _dot(a, wq_ref[...])
    k = _dot(a, wk_ref[...])
    v = _dot(a, wv_ref[...])
    if not rope:
        v_ref[...] = v.astype(BF16)
    q = q * (ATTN_DH ** -0.5 * LOG2E)
    if rope:
        tm = q.shape[0]
        row = lax.broadcasted_iota(jnp.int32, (VT_ROWS - ATTN_DH, MOBA_BLOCK), 0)
        ones_pad = jnp.where(row == 0, 1.0, 0.0).astype(BF16)
        for b in range(tm // MOBA_BLOCK):
            vt = v[b * MOBA_BLOCK:(b + 1) * MOBA_BLOCK, :].T.astype(BF16)
            for hd in range(ATTN_HEADS):
                v_ref[b, hd * VT_ROWS:hd * VT_ROWS + ATTN_DH, :] = vt[hd * ATTN_DH:(hd + 1) * ATTN_DH, :]
                v_ref[b, hd * VT_ROWS + ATTN_DH:(hd + 1) * VT_ROWS, :] = ones_pad
        half = ROPE_DIMS // 2
        ang = pos_ref[...].astype(F32) * inv_ref[...]
        cos = jnp.cos(ang)
        sin = jnp.sin(ang)
        dim = lax.broadcasted_iota(jnp.int32, (tm, LANES), 1) % ATTN_DH
        sin_lo = jnp.where(dim < half, -sin, 0.0)
        sin_hi = jnp.where((dim >= half) & (dim < ROPE_DIMS), sin, 0.0)
        q = _rope_tile(q, cos, sin_lo, sin_hi)
        k = _rope_tile(k, cos, sin_lo, sin_hi)
        nblk = tm // MOBA_BLOCK
        kmean_ref[0] = jnp.sum(k.reshape(nblk, MOBA_BLOCK, k.shape[1]), axis=1) * (1.0 / MOBA_BLOCK)
    q_ref[...] = q.astype(BF16)
    k_ref[...] = k.astype(BF16)


def _qkv_proj(h, gain, w_qkv, positions=None):
    S, D = h.shape
    rope = positions is not None
    w = w_qkv.astype(BF16)
    tm = min(ROW_TILE, S)
    row = lambda n: pl.BlockSpec((tm, n), lambda i: (i, 0))
    in_specs = [row(D), _const_spec((1, D)), _col_spec(D, D, 0), _col_spec(D, D, 1), _col_spec(D, D, 2)]
    args = [h, gain.reshape(1, D), w, w, w]
    out_specs = [row(D), row(D), row(D)]
    out_shape = [jax.ShapeDtypeStruct((S, D), BF16)] * 3
    if rope:
        assert S % MOBA_BLOCK == 0 and tm % MOBA_BLOCK == 0
        half = ROPE_DIMS // 2
        inv = ROPE_THETA ** (-jnp.arange(half, dtype=F32) / half)
        dim = jnp.arange(LANES) % ATTN_DH
        inv_lane = jnp.where(dim < ROPE_DIMS, inv[dim % half], 0.0).astype(F32).reshape(1, LANES)
        in_specs += [row(1), _const_spec((1, LANES))]
        args += [positions.reshape(S, 1), inv_lane]
        nblk = tm // MOBA_BLOCK
        vt_rows = ATTN_HEADS * VT_ROWS
        out_specs[2] = pl.BlockSpec((nblk, vt_rows, MOBA_BLOCK), lambda i: (i, 0, 0))
        out_shape[2] = jax.ShapeDtypeStruct((S // MOBA_BLOCK, vt_rows, MOBA_BLOCK), BF16)
        out_specs.append(pl.BlockSpec((1, nblk, D), lambda i: (i, 0, 0)))
        out_shape.append(jax.ShapeDtypeStruct((S // tm, nblk, D), F32))
    outs = pl.pallas_call(
        functools.partial(_qkv_proj_kernel, rope=rope),
        grid=(S // tm,),
        in_specs=in_specs,
        out_specs=out_specs,
        out_shape=out_shape,
        compiler_params=_cparams("arbitrary"),
        name="moba_proj" if rope else "sb_proj",
    )(*args)
    if rope:
        q, k, v, kmean = outs
        return q, k, v, kmean.reshape(S // MOBA_BLOCK, D)
    return outs


def _split_pair(q):
    lane = lax.broadcasted_iota(jnp.int32, q.shape, 1)
    zero = jnp.zeros_like(q)
    return jnp.where(lane < ATTN_DH, q, zero), jnp.where(lane >= ATTN_DH, q, zero)


def _merge_pair(acc_a, acc_b):
    lane = lax.broadcasted_iota(jnp.int32, acc_a.shape, 1)
    return jnp.where(lane < ATTN_DH, acc_a, acc_b)


def _moba_select(gate, cur):
    nb = gate.shape[0]
    blk = lax.broadcasted_iota(jnp.int32, gate.shape, 0)
    valid = blk < cur
    g = jnp.where(valid, gate, -jnp.inf)
    sel = jnp.zeros(gate.shape, F32)
    for _ in range(MOBA_TOPK):
        mx = jnp.max(g, axis=0, keepdims=True)
        first = jnp.min(jnp.where(g == mx, blk, nb), axis=0, keepdims=True)
        hit = (blk == first) & valid
        sel = jnp.where(hit, 1.0, sel)
        g = jnp.where(blk == first, -jnp.inf, g)
    return sel


def _moba_attn_kernel(q_ref, k_ref, vt_ref, kmean_ref, out_ref,
                      sel_ref, s_ref, cm_ref, p_ref, alpha_ref, m_ref, acc_ref):
    T = ATTN_TILE
    H = ATTN_DH
    R = VT_ROWS
    TQ = q_ref.shape[0]
    first_own = pl.program_id(1) * (TQ // T)

    @pl.when(first_own == 0)
    def _():
        p_ref[...] = jnp.zeros(p_ref.shape, BF16)
        alpha_ref[...] = jnp.ones(alpha_ref.shape, F32)

    qt = q_ref[...].astype(F32).T
    dim = lax.broadcasted_iota(jnp.int32, qt.shape, 0)
    qts = [jnp.where(dim < H, qt, 0.0).astype(BF16), jnp.where(dim >= H, qt, 0.0).astype(BF16)]
    km = kmean_ref[...].astype(BF16)
    cur = first_own + lax.broadcasted_iota(jnp.int32, (1, TQ), 1) // T
    key = lax.broadcasted_iota(jnp.int32, (T, T), 0)
    query = lax.broadcasted_iota(jnp.int32, (T, T), 1)
    causal = key <= query

    G = MOBA_GROUP
    nb = sel_ref.shape[1]

    def select(x):
        sel_ref[x] = _moba_select(_dot(km, qts[x]), cur)

    def own_block(x, h):
        own = first_own + h
        k_own = k_ref[pl.ds(pl.multiple_of(own * T, T), T), :]
        s = jnp.where(causal, _dot(k_own, qts[x][:, h * T:(h + 1) * T]), NEG)
        m = jnp.max(s, axis=0, keepdims=True)
        m_ref[x, :, h * T:(h + 1) * T] = m
        acc_ref[x, :, h * T:(h + 1) * T] = _dot(vt_ref[own, x * R:(x + 1) * R, :], jnp.exp2(s - m).astype(BF16))

    def score_block(t, slot, g, x):
        k_b = k_ref[pl.ds(pl.multiple_of((t * G + g) * T, T), T), :]
        s = _dot(k_b, qts[x])
        s_ref[slot, x, g * T:(g + 1) * T, :] = s
        cm_ref[slot, x, g:g + 1, :] = jnp.max(s, axis=0, keepdims=True)

    vpu_work = [lambda x=x: select(x) for x in range(2)]
    vpu_work += [lambda x=x, h=h: own_block(x, h) for x in range(2) for h in range(TQ // T)]
    mxu_work = [lambda g=g, x=x: score_block(0, 0, g, x) for g in range(G) for x in range(2)]
    for n in range(max(len(vpu_work), len(mxu_work))):
        if n < len(mxu_work):
            mxu_work[n]()
        if n < len(vpu_work):
            vpu_work[n]()

    def fold(t, slot):
        for x in range(2):
            part = _dot(vt_ref[t * G, x * R:(x + 1) * R, :], p_ref[slot, x, 0:T, :])
            for g in range(1, G):
                part = part + _dot(vt_ref[t * G + g, x * R:(x + 1) * R, :], p_ref[slot, x, g * T:(g + 1) * T, :])
            acc_ref[x] = alpha_ref[slot, x] * acc_ref[x] + part

    def trip(t, slot):
        prev = 1 - slot
        live = t > 0
        gate = jnp.where(live, 1.0, 0.0)
        t_prev = jnp.maximum(t - 1, 0)
        t_next = jnp.minimum(t + 1, nb // G - 1)
        shifts = []
        for x in range(2):
            chosen = [sel_ref[x, pl.ds(t * G + g, 1), :] > 0.0 for g in range(G)]
            m = m_ref[x]
            m_new = m
            for g in range(G):
                m_new = jnp.maximum(m_new, jnp.where(chosen[g], cm_ref[slot, x, g:g + 1, :], NEG))
            m_ref[x] = m_new
            alpha_ref[slot, x] = jnp.exp2(m - m_new)
            shifts.append([jnp.where(chosen[g], m_new, BIG) for g in range(G)])
            acc_ref[x] = jnp.where(live, alpha_ref[prev, x], 1.0) * acc_ref[x]
        for g in range(G):
            rows = slice(g * T, (g + 1) * T)
            for x in range(2):
                score_block(t_next, prev, g, x)
                p_ref[slot, x, rows, :] = jnp.exp2(s_ref[slot, x, rows, :] - shifts[x][g]).astype(BF16)
                acc_ref[x] += gate * _dot(vt_ref[t_prev * G + g, x * R:(x + 1) * R, :], p_ref[prev, x, rows, :])

    trips = (first_own + TQ // T - 1 + G - 1) // G

    def two_trips(u, _):
        trip(2 * u, 0)

        @pl.when(2 * u + 1 < trips)
        def _():
            trip(2 * u + 1, 1)
        return 0

    lax.fori_loop(0, (trips + 1) // 2, two_trips, 0)
    for slot in range(2):
        @pl.when(jnp.logical_and(trips > 0, (trips - 1) % 2 == slot))
        def _():
            fold(trips - 1, slot)
    out_t = jnp.concatenate([acc_ref[x][:H, :] / acc_ref[x][H:H + 1, :] for x in range(2)], axis=0)
    out_ref[...] = out_t.T.astype(BF16)


def _attn_specs(S):
    tile = pl.BlockSpec((ATTN_TILE, LANES), lambda p, i: (i, p))
    resident = pl.BlockSpec((S, LANES), lambda p, i: (0, p))
    return tile, resident


def _moba_attn(q, k, v, kmean):
    S, D = q.shape
    assert MOBA_BLOCK == ATTN_TILE and ATTN_TILE % Q_BLOCK == 0 and S % ATTN_TILE == 0
    nb = S // MOBA_BLOCK
    assert nb % MOBA_GROUP == 0
    tq = min(MOBA_QUERIES, S)
    assert tq % ATTN_TILE == 0 and S % tq == 0
    tile = pl.BlockSpec((tq, LANES), lambda p, i: (i, p))
    _, resident = _attn_specs(S)
    return pl.pallas_call(
        _moba_attn_kernel,
        grid=(D // LANES, S // tq),
        in_specs=[tile, resident, pl.BlockSpec((nb, 2 * VT_ROWS, MOBA_BLOCK), lambda p, i: (0, p, 0)),
                  pl.BlockSpec((nb, LANES), lambda p, i: (0, p))],
        out_specs=tile,
        out_shape=jax.ShapeDtypeStruct((S, D), BF16),
        scratch_shapes=[pltpu.VMEM((2, nb, tq), F32),
                        pltpu.VMEM((2, 2, MOBA_GROUP * ATTN_TILE, tq), F32),
                        pltpu.VMEM((2, 2, MOBA_GROUP, tq), F32),
                        pltpu.VMEM((2, 2, MOBA_GROUP * ATTN_TILE, tq), BF16),
                        pltpu.VMEM((2, 2, 1, tq), F32),
                        pltpu.VMEM((2, 1, tq), F32), pltpu.VMEM((2, VT_ROWS, tq), F32)],
        compiler_params=_cparams("arbitrary", "arbitrary"),
        name="moba_attn",
    )(q, k, v, kmean)


def _sb_attn_kernel(q_ref, k_ref, v_ref, out_ref):
    T = SB_TILE
    i = pl.program_id(1)
    qs = _split_pair(q_ref[...])
    key_row = lax.broadcasted_iota(jnp.int32, (T, T), 0)
    key_col = lax.broadcasted_iota(jnp.int32, (T, T), 1)
    later_keys = (key_row > key_col).astype(BF16)
    strict = key_col < key_row

    def walk(tiles, carry):
        heads = range(len(qs))
        k_t = [k_ref[pl.ds(pl.multiple_of(j * T, T), T), :] for j, _, _ in tiles]
        v_t = [v_ref[pl.ds(pl.multiple_of(j * T, T), T), :] for j, _, _ in tiles]
        z2 = [[_dot_nt(qs[x], k_j) for x in heads] for k_j in k_t]
        sp = [[jnp.maximum(z, 0.0) + jnp.log2(1.0 + jnp.exp2(-jnp.abs(z))) for z in zs] for zs in z2]
        sp = [[jnp.where(strict, s, 0.0) if masked else s for s in ss] for ss, (_, masked, _) in zip(sp, tiles)]
        hi = [[s.astype(BF16) for s in ss] for ss in sp]
        lo = [[(s - h.astype(F32)).astype(BF16) for s, h in zip(ss, hs)] for ss, hs in zip(sp, hi)]
        log_sig = [[z - s for z, s in zip(zs, ss)] for zs, ss in zip(z2, sp)]
        sp_sums = [[jnp.sum(s, axis=1, keepdims=True) for s in ss] for ss in sp]
        later = [[_dot(h, later_keys) + _dot(l, later_keys) for h, l in zip(hs, ls)] for hs, ls in zip(hi, lo)]
        run = [carry[2 * x] for x in heads]
        acc = [carry[2 * x + 1] for x in heads]
        for n, (_, masked, weight) in enumerate(tiles):
            for x in heads:
                a = jnp.exp2(log_sig[n][x] - (later[n][x] + run[x]))
                if masked:
                    a = jnp.where(strict, a, 0.0)
                sp_sum = sp_sums[n][x]
                av = _dot(a.astype(BF16), v_t[n])
                if weight is not None:
                    sp_sum, av = weight * sp_sum, weight * av
                run[x] = run[x] + sp_sum
                acc[x] = acc[x] + av
        return tuple(val for x in heads for val in (run[x], acc[x]))

    def least_run(carry):
        return jnp.min(jnp.minimum(carry[0], carry[2]))

    def more(state):
        j, least = state[0], state[1]
        return jnp.logical_and(j >= 0, least < SB_RUN_CUTOFF)

    def step(state):
        carry = walk([(state[0], False, None)], state[2:])
        return (state[0] - 1, least_run(carry)) + carry

    zero = (jnp.zeros((T, 1), F32), jnp.zeros((T, LANES), F32))
    near = [(i, True, None)]
    near += [(jnp.maximum(i - n, 0), False, jnp.where(i >= n, 1.0, 0.0)) for n in range(1, SB_NEAR_TILES + 1)]
    carry = walk(near, zero + zero)
    state = lax.while_loop(more, step, (i - 1 - SB_NEAR_TILES, least_run(carry)) + carry)
    out_ref[...] = _merge_pair(state[3], state[5]).astype(BF16)


def _sb_attn(q, k, v):
    S, D = q.shape
    assert S % SB_TILE == 0
    _, resident = _attn_specs(S)
    tile = pl.BlockSpec((SB_TILE, LANES), lambda p, i: (i, p))
    return pl.pallas_call(
        _sb_attn_kernel,
        grid=(D // LANES, S // SB_TILE),
        in_specs=[tile, resident, resident],
        out_specs=tile,
        out_shape=jax.ShapeDtypeStruct((S, D), BF16),
        compiler_params=_cparams("arbitrary", "arbitrary"),
        name="sb_attn",
    )(q, k, v)


def _conv_proj_kernel(h_ref, g_ref, wb_ref, wc_ref, wu_ref, cw_ref, out_ref, tail_ref):
    tm = h_ref.shape[0]

    @pl.when(pl.program_id(0) == 0)
    def _():
        tail_ref[...] = jnp.zeros_like(tail_ref)

    a = _rms(h_ref[...], g_ref[...]).astype(BF16)
    z = _dot(a, wc_ref[...]) * _dot(a, wu_ref[...])
    row = lax.broadcasted_iota(jnp.int32, z.shape, 0)
    prev1 = tail_ref[7:8, :]
    prev2 = tail_ref[6:7, :]
    z1 = jnp.where(row == 0, prev1, pltpu.roll(z, 1, axis=0))
    z2 = jnp.where(row == 0, prev2, jnp.where(row == 1, prev1, pltpu.roll(z, 2, axis=0)))
    y = cw_ref[0:1, :] * z2 + cw_ref[1:2, :] * z1 + cw_ref[2:3, :] * z
    tail_ref[...] = z[tm - 8:, :]
    out_ref[...] = (_dot(a, wb_ref[...]) * y).astype(BF16)


def _conv_mixer_pre(h, gain, w_in, conv_w):
    S, D = h.shape
    assert conv_w.shape[0] == CONV_WIDTH == 3
    w = w_in.astype(BF16)
    cw = jnp.pad(conv_w.astype(F32), ((0, 8 - CONV_WIDTH), (0, 0)))
    tm = min(ROW_TILE, S)
    row = pl.BlockSpec((tm, D), lambda i: (i, 0))
    return pl.pallas_call(
        _conv_proj_kernel,
        grid=(S // tm,),
        in_specs=[row, _const_spec((1, D)), _col_spec(D, D, 0), _col_spec(D, D, 1), _col_spec(D, D, 2),
                  _const_spec((8, D))],
        out_specs=row,
        out_shape=jax.ShapeDtypeStruct((S, D), BF16),
        scratch_shapes=[pltpu.VMEM((8, D), F32)],
        compiler_params=_cparams("arbitrary"),
        name="conv_proj",
    )(h, gain.reshape(1, D), w, w, w, cw)


def _post_kernel(*refs, final):
    if final:
        h_ref, x_ref, wo_ref, g_ref, wg_ref, wu_ref, wd_ref, fg_ref, out_ref = refs
    else:
        h_ref, x_ref, wo_ref, g_ref, wg_ref, wu_ref, wd_ref, out_ref = refs
    h1 = h_ref[...] + _dot(x_ref[...], wo_ref[...])
    a = _rms(h1, g_ref[...]).astype(BF16)
    F = wg_ref.shape[1]
    y = h1
    for lo in range(0, F, FFN_CHUNK):
        hi = min(lo + FFN_CHUNK, F)
        gate = _dot(a, wg_ref[:, lo:hi])
        up = _dot(a, wu_ref[:, lo:hi])
        act = (gate * jax.nn.sigmoid(gate) * up).astype(BF16)
        y = y + _dot(act, wd_ref[lo:hi, :])
    if final:
        y = _rms(y, fg_ref[...])
    out_ref[...] = y


def _post(h, x, w_out, gain, w_gate, w_up, w_down, final_gain=None):
    S, D = h.shape
    K = x.shape[1]
    F = w_gate.shape[1]
    assert F % LANES == 0
    final = final_gain is not None
    tm = min(ROW_TILE, S)
    in_specs = [pl.BlockSpec((tm, D), lambda i: (i, 0)), pl.BlockSpec((tm, K), lambda i: (i, 0)),
                _const_spec((K, D)), _const_spec((1, D)), _const_spec((D, F)), _const_spec((D, F)),
                _const_spec((F, D))]
    args = [h, x, w_out.astype(BF16), gain.reshape(1, D), w_gate.astype(BF16), w_up.astype(BF16),
            w_down.astype(BF16)]
    if final:
        in_specs.append(_const_spec((1, D)))
        args.append(final_gain.reshape(1, D))
    return pl.pallas_call(
        functools.partial(_post_kernel, final=final),
        grid=(S // tm,),
        in_specs=in_specs,
        out_specs=pl.BlockSpec((tm, D), lambda i: (i, 0)),
        out_shape=jax.ShapeDtypeStruct((S, D), F32),
        compiler_params=_cparams("arbitrary"),
        name="post_final" if final else "post",
    )(*args)


def _mixer(kind, j, h, gain, positions, p):
    if kind == 0:
        q, k, v, o, gates = _mlstm_proj(h, gain, p["mlstm_w_in"][j])
        return _mlstm_core(q, k, v, o, gates, p["mlstm_b_gate"][j], p["mlstm_head_gain"][j]), p["mlstm_w_out"][j]
    if kind == 1:
        q, k, v, kmean = _qkv_proj(h, gain, p["moba_w_qkv"][j], positions)
        return _moba_attn(q, k, v, kmean), p["moba_w_out"][j]
    if kind == 2:
        return _conv_mixer_pre(h, gain, p["conv_w_in"][j], p["conv_w"][j]), p["conv_w_out"][j]
    q, k, v = _qkv_proj(h, gain, p["sb_w_qkv"][j])
    return _sb_attn(q, k, v), p["sb_w_out"][j]


def kernel(x, positions, norm_gains, mlstm_w_in, mlstm_b_gate, mlstm_head_gain, mlstm_w_out, moba_w_qkv, moba_w_out, conv_w_in, conv_w, conv_w_out, sb_w_qkv, sb_w_out, ffn_w_gate, ffn_w_up, ffn_w_down, final_gain):
    B, S, D = x.shape
    assert D == D_MODEL
    depth = norm_gains.shape[0]
    p = dict(mlstm_w_in=mlstm_w_in, mlstm_b_gate=mlstm_b_gate, mlstm_head_gain=mlstm_head_gain,
             mlstm_w_out=mlstm_w_out, moba_w_qkv=moba_w_qkv, moba_w_out=moba_w_out,
             conv_w_in=conv_w_in, conv_w=conv_w, conv_w_out=conv_w_out,
             sb_w_qkv=sb_w_qkv, sb_w_out=sb_w_out)
    outs = []
    for b in range(B):
        h = x[b]
        for layer in range(depth):
            kind, j = layer % 4, layer // 4
            mix, w_out = _mixer(kind, j, h, norm_gains[layer, 0], positions[b], p)
            h = _post(h, mix, w_out, norm_gains[layer, 1], ffn_w_gate[layer], ffn_w_up[layer],
                      ffn_w_down[layer], final_gain if layer == depth - 1 else None)
        outs.append(h)
    return jnp.stack(outs)
```

```python
import functools

import jax
import jax.numpy as jnp
from jax import lax
from jax.experimental import pallas as pl
from jax.experimental.pallas import tpu as pltpu

F32 = jnp.float32
BF16 = jnp.bfloat16

EPS = 1e-6
NEG = -1e30
D_MODEL = 1024
MLSTM_HEADS = 4
MLSTM_DQK = 128
MLSTM_DV = 256
MLSTM_CHUNK = 128
ATTN_HEADS = 16
ATTN_DH = 64
ROPE_DIMS = 16
ROPE_THETA = 500000.0
MOBA_BLOCK = 256
MOBA_TOPK = 3
Q_BLOCK = 128
ATTN_TILE = 256
LOG2E = 1.4426950408889634
SB_TILE = 256
SB_NEAR_TILES = 1
SB_RUN_CUTOFF = 150.0
BF16_SUBLANES = 16
VT_ROWS = ATTN_DH + BF16_SUBLANES
BIG = 1e30
MOBA_QUERIES = 512
MOBA_GROUP = 4
CONV_WIDTH = 3

LANES = 128
ROW_TILE = 512
MXU_WIDTH = 256
FFN_CHUNK = 6 * MXU_WIDTH
VMEM_LIMIT = 56 * 1024 * 1024


def _cparams(*sem):
    return pltpu.CompilerParams(dimension_semantics=sem, vmem_limit_bytes=VMEM_LIMIT)


def _const_spec(shape):
    nd = len(shape)
    return pl.BlockSpec(shape, lambda *_: (0,) * nd, pipeline_mode=pl.Buffered(1))


def _col_spec(rows, cols, c):
    return pl.BlockSpec((rows, cols), lambda *_: (0, c), pipeline_mode=pl.Buffered(1))


def _rms(x, g):
    return x * lax.rsqrt(jnp.mean(x * x, axis=-1, keepdims=True) + EPS) * g


def _dot(a, b):
    return jnp.dot(a, b, preferred_element_type=F32)


def _dot_nt(a, b):
    return lax.dot_general(a, b, (((1,), (1,)), ((), ())), preferred_element_type=F32)


def _dot_tn(a, b):
    return lax.dot_general(a, b, (((0,), (0,)), ((), ())), preferred_element_type=F32)


def _mlstm_proj_kernel(h_ref, g_ref, wq_ref, wk_ref, wv_ref, wo_ref, wg_ref,
                       q_ref, k_ref, v_ref, o_ref, gate_ref):
    a = _rms(h_ref[...], g_ref[...]).astype(BF16)
    q_ref[...] = (_dot(a, wq_ref[...]) * (MLSTM_DQK ** -0.5)).astype(BF16)
    k_ref[...] = _dot(a, wk_ref[...]).astype(BF16)
    v_ref[...] = _dot(a, wv_ref[...]).astype(BF16)
    o_ref[...] = _dot(a, wo_ref[...])
    gate_ref[...] = _dot(a, wg_ref[...])


def _mlstm_proj(h, gain, w_in):
    S, D = h.shape
    NH, DK, DV = MLSTM_HEADS, MLSTM_DQK, MLSTM_DV
    nq, nv = NH * DK, NH * DV
    assert 2 * nq == nv
    w = w_in.astype(BF16)
    wg = jnp.pad(w_in[:, 2 * nq + 2 * nv:], ((0, 0), (0, LANES - 2 * NH))).astype(BF16)
    tm = min(ROW_TILE, S)
    row = lambda n: pl.BlockSpec((tm, n), lambda i: (i, 0))
    return pl.pallas_call(
        _mlstm_proj_kernel,
        grid=(S // tm,),
        in_specs=[row(D), _const_spec((1, D)), _col_spec(D, nq, 0), _col_spec(D, nq, 1),
                  _col_spec(D, nv, 1), _col_spec(D, nv, 2), _const_spec((D, LANES))],
        out_specs=[row(nq), row(nq), row(nv), row(nv), row(LANES)],
        out_shape=[jax.ShapeDtypeStruct((S, nq), BF16), jax.ShapeDtypeStruct((S, nq), BF16),
                   jax.ShapeDtypeStruct((S, nv), BF16), jax.ShapeDtypeStruct((S, nv), F32),
                   jax.ShapeDtypeStruct((S, LANES), F32)],
        compiler_params=_cparams("arbitrary"),
        name="mlstm_proj",
    )(h, gain.reshape(1, D), w, w, w, w, wg)


def _mlstm_core_kernel(q_ref, k_ref, v_ref, o_ref, gate_ref, bg_ref, hg_ref, out_ref,
                       c_ref, n_ref, m_ref):
    NH, DK, DV, L = MLSTM_HEADS, MLSTM_DQK, MLSTM_DV, MLSTM_CHUNK

    @pl.when(pl.program_id(0) == 0)
    def _():
        c_ref[...] = jnp.zeros_like(c_ref)
        n_ref[...] = jnp.zeros_like(n_ref)
        m_ref[...] = jnp.zeros_like(m_ref)

    g = gate_ref[...] + bg_ref[...]
    lane = lax.broadcasted_iota(jnp.int32, (L, LANES), 1)
    log_sig = jnp.minimum(g, 0.0) - jnp.log(1.0 + jnp.exp(-jnp.abs(g)))
    gl = jnp.where(lane >= NH, log_sig, g)
    gl_t = gl.T
    t_idx = lax.broadcasted_iota(jnp.int32, (L, L), 0)
    s_idx = lax.broadcasted_iota(jnp.int32, (L, L), 1)
    causal = s_idx <= t_idx

    heads = range(NH)
    i_col = [gl[:, hd:hd + 1] for hd in heads]
    f_col = [gl[:, NH + hd:NH + hd + 1] for hd in heads]
    i_row = [gl_t[hd:hd + 1, :] for hd in heads]
    f_row = [gl_t[NH + hd:NH + hd + 1, :] for hd in heads]
    b_col = [jnp.sum(jnp.where(causal, f_row[hd], 0.0), axis=1, keepdims=True) for hd in heads]
    b_row = [jnp.sum(jnp.where(t_idx <= s_idx, f_col[hd], 0.0), axis=0, keepdims=True) for hd in heads]
    m_prev = [m_ref[hd] for hd in heads]
    dmat = [jnp.where(causal, b_col[hd] - b_row[hd] + i_row[hd], -jnp.inf) for hd in heads]
    inter = [b_col[hd] + m_prev[hd] for hd in heads]
    m_t = [jnp.maximum(inter[hd], jnp.max(dmat[hd], axis=1, keepdims=True)) for hd in heads]
    w_intra = [jnp.exp(dmat[hd] - m_t[hd]) for hd in heads]
    w_inter = [jnp.exp(inter[hd] - m_t[hd]) for hd in heads]

    q = [q_ref[:, hd * DK:(hd + 1) * DK] for hd in heads]
    k = [k_ref[:, hd * DK:(hd + 1) * DK] for hd in heads]
    v = [v_ref[:, hd * DV:(hd + 1) * DV] for hd in heads]
    c_old = [c_ref[hd] for hd in heads]
    n_old = [n_ref[hd] for hd in heads]
    s = [_dot_nt(q[hd], k[hd]) * w_intra[hd] for hd in heads]
    q_c = [_dot(q[hd], c_old[hd].astype(BF16)) for hd in heads]
    num = [_dot(s[hd].astype(BF16), v[hd]) + w_inter[hd] * q_c[hd] for hd in heads]
    qn = [jnp.sum(q[hd].astype(F32) * n_old[hd], axis=1, keepdims=True) for hd in heads]
    den = [jnp.sum(s[hd], axis=1, keepdims=True) + w_inter[hd] * qn[hd] for hd in heads]
    den = [jnp.maximum(jnp.abs(den[hd]), jnp.exp(-m_t[hd])) for hd in heads]
    h_out = [num[hd] / den[hd] for hd in heads]

    g_tot = [b_col[hd][L - 1:L, :] for hd in heads]
    a_col = [g_tot[hd] - b_col[hd] + i_col[hd] for hd in heads]
    m_new = [jnp.maximum(g_tot[hd] + m_prev[hd], jnp.max(a_col[hd], axis=0, keepdims=True)) for hd in heads]
    decay = [jnp.exp(g_tot[hd] + m_prev[hd] - m_new[hd]) for hd in heads]
    kw = [k[hd].astype(F32) * jnp.exp(a_col[hd] - m_new[hd]) for hd in heads]
    for hd in heads:
        c_ref[hd] = decay[hd] * c_old[hd] + _dot_tn(kw[hd].astype(BF16), v[hd])
        n_ref[hd] = decay[hd] * n_old[hd] + jnp.sum(kw[hd], axis=0, keepdims=True)
        m_ref[hd] = m_new[hd]

    for hd in heads:
        hn = h_out[hd] * lax.rsqrt(jnp.mean(h_out[hd] * h_out[hd], axis=1, keepdims=True) + EPS)
        hn = hn * hg_ref[:, hd * DV:(hd + 1) * DV]
        out = hn * jax.nn.sigmoid(o_ref[:, hd * DV:(hd + 1) * DV])
        out_ref[:, hd * DV:(hd + 1) * DV] = out.astype(BF16)


def _mlstm_core(q, k, v, o, gates, b_gate, head_gain):
    S = q.shape[0]
    NH, DK, DV, L = MLSTM_HEADS, MLSTM_DQK, MLSTM_DV, MLSTM_CHUNK
    bg = jnp.pad(b_gate.astype(F32), (0, LANES - 2 * NH)).reshape(1, LANES)
    row = lambda n: pl.BlockSpec((L, n), lambda c: (c, 0))
    return pl.pallas_call(
        _mlstm_core_kernel,
        grid=(S // L,),
        in_specs=[row(NH * DK), row(NH * DK), row(NH * DV), row(NH * DV), row(LANES),
                  _const_spec((1, LANES)), _const_spec((1, NH * DV))],
        out_specs=row(NH * DV),
        out_shape=jax.ShapeDtypeStruct((S, NH * DV), BF16),
        scratch_shapes=[pltpu.VMEM((NH, DK, DV), F32), pltpu.VMEM((NH, 1, DK), F32),
                        pltpu.VMEM((NH, 1, 1), F32)],
        compiler_params=_cparams("arbitrary"),
        name="mlstm_core",
    )(q, k, v, o, gates, bg, head_gain.astype(F32).reshape(1, NH * DV))


def _rope_tile(x, cos, sin_lo, sin_hi):
    half = ROPE_DIMS // 2
    cols = []
    for c in range(x.shape[1] // LANES):
        xc = x[:, c * LANES:(c + 1) * LANES]
        up = pltpu.roll(xc, LANES - half, axis=1)
        down = pltpu.roll(xc, half, axis=1)
        cols.append(xc * cos + up * sin_lo + down * sin_hi)
    return jnp.concatenate(cols, axis=1)


def _qkv_proj_kernel(*refs, rope):
    if rope:
        (h_ref, g_ref, wq_ref, wk_ref, wv_ref, pos_ref, inv_ref,
         q_ref, k_ref, v_ref, kmean_ref) = refs
    else:
        h_ref, g_ref, wq_ref, wk_ref, wv_ref, q_ref, k_ref, v_ref = refs
    a = _rms(h_ref[...], g_ref[...]).astype(BF16)
    q = _dot(a, wq_ref[...])
    k = _dot(a, wk_ref[...])
    v = _dot(a, wv_ref[...])
    if not rope:
        v_ref[...] = v.astype(BF16)
    q = q * (ATTN_DH ** -0.5 * LOG2E)
    if rope:
        tm = q.shape[0]
        row = lax.broadcasted_iota(jnp.int32, (VT_ROWS - ATTN_DH, MOBA_BLOCK), 0)
        ones_pad = jnp.where(row == 0, 1.0, 0.0).astype(BF16)
        for b in range(tm // MOBA_BLOCK):
            vt = v[b * MOBA_BLOCK:(b + 1) * MOBA_BLOCK, :].T.astype(BF16)
            for hd in range(ATTN_HEADS):
                v_ref[b, hd * VT_ROWS:hd * VT_ROWS + ATTN_DH, :] = vt[hd * ATTN_DH:(hd + 1) * ATTN_DH, :]
                v_ref[b, hd * VT_ROWS + ATTN_DH:(hd + 1) * VT_ROWS, :] = ones_pad
        half = ROPE_DIMS // 2
        ang = pos_ref[...].astype(F32) * inv_ref[...]
        cos = jnp.cos(ang)
        sin = jnp.sin(ang)
        dim = lax.broadcasted_iota(jnp.int32, (tm, LANES), 1) % ATTN_DH
        sin_lo = jnp.where(dim < half, -sin, 0.0)
        sin_hi = jnp.where((dim >= half) & (dim < ROPE_DIMS), sin, 0.0)
        q = _rope_tile(q, cos, sin_lo, sin_hi)
        k = _rope_tile(k, cos, sin_lo, sin_hi)
        nblk = tm // MOBA_BLOCK
        kmean_ref[0] = jnp.sum(k.reshape(nblk, MOBA_BLOCK, k.shape[1]), axis=1) * (1.0 / MOBA_BLOCK)
    q_ref[...] = q.astype(BF16)
    k_ref[...] = k.astype(BF16)


def _qkv_proj(h, gain, w_qkv, positions=None):
    S, D = h.shape
    rope = positions is not None
    w = w_qkv.astype(BF16)
    tm = min(ROW_TILE, S)
    row = lambda n: pl.BlockSpec((tm, n), lambda i: (i, 0))
    in_specs = [row(D), _const_spec((1, D)), _col_spec(D, D, 0), _col_spec(D, D, 1), _col_spec(D, D, 2)]
    args = [h, gain.reshape(1, D), w, w, w]
    out_specs = [row(D), row(D), row(D)]
    out_shape = [jax.ShapeDtypeStruct((S, D), BF16)] * 3
    if rope:
        assert S % MOBA_BLOCK == 0 and tm % MOBA_BLOCK == 0
        half = ROPE_DIMS // 2
        inv = ROPE_THETA ** (-jnp.arange(half, dtype=F32) / half)
        dim = jnp.arange(LANES) % ATTN_DH
        inv_lane = jnp.where(dim < ROPE_DIMS, inv[dim % half], 0.0).astype(F32).reshape(1, LANES)
        in_specs += [row(1), _const_spec((1, LANES))]
        args += [positions.reshape(S, 1), inv_lane]
        nblk = tm // MOBA_BLOCK
        vt_rows = ATTN_HEADS * VT_ROWS
        out_specs[2] = pl.BlockSpec((nblk, vt_rows, MOBA_BLOCK), lambda i: (i, 0, 0))
        out_shape[2] = jax.ShapeDtypeStruct((S // MOBA_BLOCK, vt_rows, MOBA_BLOCK), BF16)
        out_specs.append(pl.BlockSpec((1, nblk, D), lambda i: (i, 0, 0)))
        out_shape.append(jax.ShapeDtypeStruct((S // tm, nblk, D), F32))
    outs = pl.pallas_call(
        functools.partial(_qkv_proj_kernel, rope=rope),
        grid=(S // tm,),
        in_specs=in_specs,
        out_specs=out_specs,
        out_shape=out_shape,
        compiler_params=_cparams("arbitrary"),
        name="moba_proj" if rope else "sb_proj",
    )(*args)
    if rope:
        q, k, v, kmean = outs
        return q, k, v, kmean.reshape(S // MOBA_BLOCK, D)
    return outs


def _split_pair(q):
    lane = lax.broadcasted_iota(jnp.int32, q.shape, 1)
    zero = jnp.zeros_like(q)
    return jnp.where(lane < ATTN_DH, q, zero), jnp.where(lane >= ATTN_DH, q, zero)


def _merge_pair(acc_a, acc_b):
    lane = lax.broadcasted_iota(jnp.int32, acc_a.shape, 1)
    return jnp.where(lane < ATTN_DH, acc_a, acc_b)


def _moba_select(gate, cur):
    nb = gate.shape[0]
    blk = lax.broadcasted_iota(jnp.int32, gate.shape, 0)
    valid = blk < cur
    g = jnp.where(valid, gate, -jnp.inf)
    sel = jnp.zeros(gate.shape, F32)
    for _ in range(MOBA_TOPK):
        mx = jnp.max(g, axis=0, keepdims=True)
        first = jnp.min(jnp.where(g == mx, blk, nb), axis=0, keepdims=True)
        hit = (blk == first) & valid
        sel = jnp.where(hit, 1.0, sel)
        g = jnp.where(blk == first, -jnp.inf, g)
    return sel


def _moba_attn_kernel(q_ref, k_ref, vt_ref, kmean_ref, out_ref,
                      sel_ref, s_ref, cm_ref, p_ref, alpha_ref, m_ref, acc_ref):
    T = ATTN_TILE
    H = ATTN_DH
    R = VT_ROWS
    TQ = q_ref.shape[0]
    first_own = pl.program_id(1) * (TQ // T)

    @pl.when(first_own == 0)
    def _():
        p_ref[...] = jnp.zeros(p_ref.shape, BF16)
        alpha_ref[...] = jnp.ones(alpha_ref.shape, F32)

    qt = q_ref[...].astype(F32).T
    dim = lax.broadcasted_iota(jnp.int32, qt.shape, 0)
    qts = [jnp.where(dim < H, qt, 0.0).astype(BF16), jnp.where(dim >= H, qt, 0.0).astype(BF16)]
    km = kmean_ref[...].astype(BF16)
    cur = first_own + lax.broadcasted_iota(jnp.int32, (1, TQ), 1) // T
    key = lax.broadcasted_iota(jnp.int32, (T, T), 0)
    query = lax.broadcasted_iota(jnp.int32, (T, T), 1)
    causal = key <= query

    G = MOBA_GROUP
    nb = sel_ref.shape[1]

    def select(x):
        sel_ref[x] = _moba_select(_dot(km, qts[x]), cur)

    def own_block(x, h):
        own = first_own + h
        k_own = k_ref[pl.ds(pl.multiple_of(own * T, T), T), :]
        s = jnp.where(causal, _dot(k_own, qts[x][:, h * T:(h + 1) * T]), NEG)
        m = jnp.max(s, axis=0, keepdims=True)
        m_ref[x, :, h * T:(h + 1) * T] = m
        acc_ref[x, :, h * T:(h + 1) * T] = _dot(vt_ref[own, x * R:(x + 1) * R, :], jnp.exp2(s - m).astype(BF16))

    def score_block(t, slot, g, x):
        k_b = k_ref[pl.ds(pl.multiple_of((t * G + g) * T, T), T), :]
        s = _dot(k_b, qts[x])
        s_ref[slot, x, g * T:(g + 1) * T, :] = s
        cm_ref[slot, x, g:g + 1, :] = jnp.max(s, axis=0, keepdims=True)

    vpu_work = [lambda x=x: select(x) for x in range(2)]
    vpu_work += [lambda x=x, h=h: own_block(x, h) for x in range(2) for h in range(TQ // T)]
    mxu_work = [lambda g=g, x=x: score_block(0, 0, g, x) for g in range(G) for x in range(2)]
    for n in range(max(len(vpu_work), len(mxu_work))):
        if n < len(mxu_work):
            mxu_work[n]()
        if n < len(vpu_work):
            vpu_work[n]()

    def fold(t, slot):
        for x in range(2):
            part = _dot(vt_ref[t * G, x * R:(x + 1) * R, :], p_ref[slot, x, 0:T, :])
            for g in range(1, G):
                part = part + _dot(vt_ref[t * G + g, x * R:(x + 1) * R, :], p_ref[slot, x, g * T:(g + 1) * T, :])
            acc_ref[x] = alpha_ref[slot, x] * acc_ref[x] + part

    def trip(t, slot):
        prev = 1 - slot
        live = t > 0
        gate = jnp.where(live, 1.0, 0.0)
        t_prev = jnp.maximum(t - 1, 0)
        t_next = jnp.minimum(t + 1, nb // G - 1)
        shifts = []
        for x in range(2):
            chosen = [sel_ref[x, pl.ds(t * G + g, 1), :] > 0.0 for g in range(G)]
            m = m_ref[x]
            m_new = m
            for g in range(G):
                m_new = jnp.maximum(m_new, jnp.where(chosen[g], cm_ref[slot, x, g:g + 1, :], NEG))
            m_ref[x] = m_new
            alpha_ref[slot, x] = jnp.exp2(m - m_new)
            shifts.append([jnp.where(chosen[g], m_new, BIG) for g in range(G)])
            acc_ref[x] = jnp.where(live, alpha_ref[prev, x], 1.0) * acc_ref[x]
        for g in range(G):
            rows = slice(g * T, (g + 1) * T)
            for x in range(2):
                score_block(t_next, prev, g, x)
                p_ref[slot, x, rows, :] = jnp.exp2(s_ref[slot, x, rows, :] - shifts[x][g]).astype(BF16)
                acc_ref[x] += gate * _dot(vt_ref[t_prev * G + g, x * R:(x + 1) * R, :], p_ref[prev, x, rows, :])

    trips = (first_own + TQ // T - 1 + G - 1) // G

    def two_trips(u, _):
        trip(2 * u, 0)

        @pl.when(2 * u + 1 < trips)
        def _():
            trip(2 * u + 1, 1)
        return 0

    lax.fori_loop(0, (trips + 1) // 2, two_trips, 0)
    for slot in range(2):
        @pl.when(jnp.logical_and(trips > 0, (trips - 1) % 2 == slot))
        def _():
            fold(trips - 1, slot)
    out_t = jnp.concatenate([acc_ref[x][:H, :] / acc_ref[x][H:H + 1, :] for x in range(2)], axis=0)
    out_ref[...] = out_t.T.astype(BF16)


def _attn_specs(S):
    tile = pl.BlockSpec((ATTN_TILE, LANES), lambda p, i: (i, p))
    resident = pl.BlockSpec((S, LANES), lambda p, i: (0, p))
    return tile, resident


def _moba_attn(q, k, v, kmean):
    S, D = q.shape
    assert MOBA_BLOCK == ATTN_TILE and ATTN_TILE % Q_BLOCK == 0 and S % ATTN_TILE == 0
    nb = S // MOBA_BLOCK
    assert nb % MOBA_GROUP == 0
    tq = min(MOBA_QUERIES, S)
    assert tq % ATTN_TILE == 0 and S % tq == 0
    tile = pl.BlockSpec((tq, LANES), lambda p, i: (i, p))
    _, resident = _attn_specs(S)
    return pl.pallas_call(
        _moba_attn_kernel,
        grid=(D // LANES, S // tq),
        in_specs=[tile, resident, pl.BlockSpec((nb, 2 * VT_ROWS, MOBA_BLOCK), lambda p, i: (0, p, 0)),
                  pl.BlockSpec((nb, LANES), lambda p, i: (0, p))],
        out_specs=tile,
        out_shape=jax.ShapeDtypeStruct((S, D), BF16),
        scratch_shapes=[pltpu.VMEM((2, nb, tq), F32),
                        pltpu.VMEM((2, 2, MOBA_GROUP * ATTN_TILE, tq), F32),
                        pltpu.VMEM((2, 2, MOBA_GROUP, tq), F32),
                        pltpu.VMEM((2, 2, MOBA_GROUP * ATTN_TILE, tq), BF16),
                        pltpu.VMEM((2, 2, 1, tq), F32),
                        pltpu.VMEM((2, 1, tq), F32), pltpu.VMEM((2, VT_ROWS, tq), F32)],
        compiler_params=_cparams("arbitrary", "arbitrary"),
        name="moba_attn",
    )(q, k, v, kmean)


def _sb_attn_kernel(q_ref, k_ref, v_ref, out_ref):
    T = SB_TILE
    i = pl.program_id(1)
    qs = _split_pair(q_ref[...])
    key_row = lax.broadcasted_iota(jnp.int32, (T, T), 0)
    key_col = lax.broadcasted_iota(jnp.int32, (T, T), 1)
    later_keys = (key_row > key_col).astype(BF16)
    strict = key_col < key_row

    def walk(tiles, carry):
        heads = range(len(qs))
        k_t = [k_ref[pl.ds(pl.multiple_of(j * T, T), T), :] for j, _, _ in tiles]
        v_t = [v_ref[pl.ds(pl.multiple_of(j * T, T), T), :] for j, _, _ in tiles]
        z2 = [[_dot_nt(qs[x], k_j) for x in heads] for k_j in k_t]
        sp = [[jnp.maximum(z, 0.0) + jnp.log2(1.0 + jnp.exp2(-jnp.abs(z))) for z in zs] for zs in z2]
        sp = [[jnp.where(strict, s, 0.0) if masked else s for s in ss] for ss, (_, masked, _) in zip(sp, tiles)]
        hi = [[s.astype(BF16) for s in ss] for ss in sp]
        lo = [[(s - h.astype(F32)).astype(BF16) for s, h in zip(ss, hs)] for ss, hs in zip(sp, hi)]
        log_sig = [[z - s for z, s in zip(zs, ss)] for zs, ss in zip(z2, sp)]
        sp_sums = [[jnp.sum(s, axis=1, keepdims=True) for s in ss] for ss in sp]
        later = [[_dot(h, later_keys) + _dot(l, later_keys) for h, l in zip(hs, ls)] for hs, ls in zip(hi, lo)]
        run = [carry[2 * x] for x in heads]
        acc = [carry[2 * x + 1] for x in heads]
        for n, (_, masked, weight) in enumerate(tiles):
            for x in heads:
                a = jnp.exp2(log_sig[n][x] - (later[n][x] + run[x]))
                if masked:
                    a = jnp.where(strict, a, 0.0)
                sp_sum = sp_sums[n][x]
                av = _dot(a.astype(BF16), v_t[n])
                if weight is not None:
                    sp_sum, av = weight * sp_sum, weight * av
                run[x] = run[x] + sp_sum
                acc[x] = acc[x] + av
        return tuple(val for x in heads for val in (run[x], acc[x]))

    def least_run(carry):
        return jnp.min(jnp.minimum(carry[0], carry[2]))

    def more(state):
        j, least = state[0], state[1]
        return jnp.logical_and(j >= 0, least < SB_RUN_CUTOFF)

    def step(state):
        carry = walk([(state[0], False, None)], state[2:])
        return (state[0] - 1, least_run(carry)) + carry

    zero = (jnp.zeros((T, 1), F32), jnp.zeros((T, LANES), F32))
    near = [(i, True, None)]
    near += [(jnp.maximum(i - n, 0), False, jnp.where(i >= n, 1.0, 0.0)) for n in range(1, SB_NEAR_TILES + 1)]
    carry = walk(near, zero + zero)
    state = lax.while_loop(more, step, (i - 1 - SB_NEAR_TILES, least_run(carry)) + carry)
    out_ref[...] = _merge_pair(state[3], state[5]).astype(BF16)


def _sb_attn(q, k, v):
    S, D = q.shape
    assert S % SB_TILE == 0
    _, resident = _attn_specs(S)
    tile = pl.BlockSpec((SB_TILE, LANES), lambda p, i: (i, p))
    return pl.pallas_call(
        _sb_attn_kernel,
        grid=(D // LANES, S // SB_TILE),
        in_specs=[tile, resident, resident],
        out_specs=tile,
        out_shape=jax.ShapeDtypeStruct((S, D), BF16),
        compiler_params=_cparams("arbitrary", "arbitrary"),
        name="sb_attn",
    )(q, k, v)


def _conv_proj_kernel(h_ref, g_ref, wb_ref, wc_ref, wu_ref, cw_ref, out_ref, tail_ref):
    tm = h_ref.shape[0]

    @pl.when(pl.program_id(0) == 0)
    def _():
        tail_ref[...] = jnp.zeros_like(tail_ref)

    a = _rms(h_ref[...], g_ref[...]).astype(BF16)
    z = _dot(a, wc_ref[...]) * _dot(a, wu_ref[...])
    row = lax.broadcasted_iota(jnp.int32, z.shape, 0)
    prev1 = tail_ref[7:8, :]
    prev2 = tail_ref[6:7, :]
    z1 = jnp.where(row == 0, prev1, pltpu.roll(z, 1, axis=0))
    z2 = jnp.where(row == 0, prev2, jnp.where(row == 1, prev1, pltpu.roll(z, 2, axis=0)))
    y = cw_ref[0:1, :] * z2 + cw_ref[1:2, :] * z1 + cw_ref[2:3, :] * z
    tail_ref[...] = z[tm - 8:, :]
    out_ref[...] = (_dot(a, wb_ref[...]) * y).astype(BF16)


def _conv_mixer_pre(h, gain, w_in, conv_w):
    S, D = h.shape
    assert conv_w.shape[0] == CONV_WIDTH == 3
    w = w_in.astype(BF16)
    cw = jnp.pad(conv_w.astype(F32), ((0, 8 - CONV_WIDTH), (0, 0)))
    tm = min(ROW_TILE, S)
    row = pl.BlockSpec((tm, D), lambda i: (i, 0))
    return pl.pallas_call(
        _conv_proj_kernel,
        grid=(S // tm,),
        in_specs=[row, _const_spec((1, D)), _col_spec(D, D, 0), _col_spec(D, D, 1), _col_spec(D, D, 2),
                  _const_spec((8, D))],
        out_specs=row,
        out_shape=jax.ShapeDtypeStruct((S, D), BF16),
        scratch_shapes=[pltpu.VMEM((8, D), F32)],
        compiler_params=_cparams("arbitrary"),
        name="conv_proj",
    )(h, gain.reshape(1, D), w, w, w, cw)


def _post_kernel(*refs, final):
    if final:
        h_ref, x_ref, wo_ref, g_ref, wg_ref, wu_ref, wd_ref, fg_ref, out_ref = refs
    else:
        h_ref, x_ref, wo_ref, g_ref, wg_ref, wu_ref, wd_ref, out_ref = refs
    h1 = h_ref[...] + _dot(x_ref[...], wo_ref[...])
    a = _rms(h1, g_ref[...]).astype(BF16)
    F = wg_ref.shape[1]
    y = h1
    for lo in range(0, F, FFN_CHUNK):
        hi = min(lo + FFN_CHUNK, F)
        gate = _dot(a, wg_ref[:, lo:hi])
        up = _dot(a, wu_ref[:, lo:hi])
        act = (gate * jax.nn.sigmoid(gate) * up).astype(BF16)
        y = y + _dot(act, wd_ref[lo:hi, :])
    if final:
        y = _rms(y, fg_ref[...])
    out_ref[...] = y


def _post(h, x, w_out, gain, layer, w_gate, w_up, w_down, final_gain=None):
    S, D = h.shape
    K = x.shape[1]
    F = w_gate.shape[2]
    assert F % LANES == 0
    final = final_gain is not None
    tm = min(ROW_TILE, S)

    def layer_spec(rows, cols):
        return pl.BlockSpec((None, rows, cols), lambda i: (layer, 0, 0), pipeline_mode=pl.Buffered(1))

    in_specs = [pl.BlockSpec((tm, D), lambda i: (i, 0)), pl.BlockSpec((tm, K), lambda i: (i, 0)),
                _const_spec((K, D)), _const_spec((1, D)), layer_spec(D, F), layer_spec(D, F), layer_spec(F, D)]
    args = [h, x, w_out.astype(BF16), gain.reshape(1, D), w_gate.astype(BF16), w_up.astype(BF16),
            w_down.astype(BF16)]
    if final:
        in_specs.append(_const_spec((1, D)))
        args.append(final_gain.reshape(1, D))
    return pl.pallas_call(
        functools.partial(_post_kernel, final=final),
        grid=(S // tm,),
        in_specs=in_specs,
        out_specs=pl.BlockSpec((tm, D), lambda i: (i, 0)),
        out_shape=jax.ShapeDtypeStruct((S, D), F32),
        compiler_params=_cparams("arbitrary"),
        name="post_final" if final else "post",
    )(*args)


def _mixer(kind, j, h, gain, positions, p):
    if kind == 0:
        q, k, v, o, gates = _mlstm_proj(h, gain, p["mlstm_w_in"][j])
        return _mlstm_core(q, k, v, o, gates, p["mlstm_b_gate"][j], p["mlstm_head_gain"][j]), p["mlstm_w_out"][j]
    if kind == 1:
        q, k, v, kmean = _qkv_proj(h, gain, p["moba_w_qkv"][j], positions)
        return _moba_attn(q, k, v, kmean), p["moba_w_out"][j]
    if kind == 2:
        return _conv_mixer_pre(h, gain, p["conv_w_in"][j], p["conv_w"][j]), p["conv_w_out"][j]
    q, k, v = _qkv_proj(h, gain, p["sb_w_qkv"][j])
    return _sb_attn(q, k, v), p["sb_w_out"][j]


def kernel(x, positions, norm_gains, mlstm_w_in, mlstm_b_gate, mlstm_head_gain, mlstm_w_out, moba_w_qkv, moba_w_out, conv_w_in, conv_w, conv_w_out, sb_w_qkv, sb_w_out, ffn_w_gate, ffn_w_up, ffn_w_down, final_gain):
    B, S, D = x.shape
    assert D == D_MODEL
    depth = norm_gains.shape[0]
    p = dict(mlstm_w_in=mlstm_w_in, mlstm_b_gate=mlstm_b_gate, mlstm_head_gain=mlstm_head_gain,
             mlstm_w_out=mlstm_w_out, moba_w_qkv=moba_w_qkv, moba_w_out=moba_w_out,
             conv_w_in=conv_w_in, conv_w=conv_w, conv_w_out=conv_w_out,
             sb_w_qkv=sb_w_qkv, sb_w_out=sb_w_out)
    outs = []
    for b in range(B):
        h = x[b]
        for layer in range(depth):
            kind, j = layer % 4, layer // 4
            mix, w_out = _mixer(kind, j, h, norm_gains[layer, 0], positions[b], p)
            h = _post(h, mix, w_out, norm_gains[layer, 1], layer, ffn_w_gate, ffn_w_up, ffn_w_down,
                      final_gain if layer == depth - 1 else None)
        outs.append(h)
    return jnp.stack(outs)
```

```python
import functools

import jax
import jax.numpy as jnp
from jax import lax
from jax.experimental import pallas as pl
from jax.experimental.pallas import tpu as pltpu

F32 = jnp.float32
BF16 = jnp.bfloat16

EPS = 1e-6
NEG = -1e30
D_MODEL = 1024
MLSTM_HEADS = 4
MLSTM_DQK = 128
MLSTM_DV = 256
MLSTM_CHUNK = 128
MLSTM_STEP_CHUNKS = 4
ATTN_HEADS = 16
ATTN_DH = 64
ROPE_DIMS = 16
ROPE_THETA = 500000.0
MOBA_BLOCK = 256
MOBA_TOPK = 3
Q_BLOCK = 128
ATTN_TILE = 256
LOG2E = 1.4426950408889634
SB_TILE = 256
SB_NEAR_TILES = 1
SB_RUN_CUTOFF = 150.0
BF16_SUBLANES = 16
VT_ROWS = ATTN_DH + BF16_SUBLANES
BIG = 1e30
MOBA_QUERIES = 512
MOBA_GROUP = 4
CONV_WIDTH = 3

LANES = 128
ROW_TILE = 512
MXU_WIDTH = 256
FFN_CHUNK = 6 * MXU_WIDTH
VMEM_LIMIT = 56 * 1024 * 1024


def _cparams(*sem):
    return pltpu.CompilerParams(dimension_semantics=sem, vmem_limit_bytes=VMEM_LIMIT)


def _const_spec(shape):
    nd = len(shape)
    return pl.BlockSpec(shape, lambda *_: (0,) * nd, pipeline_mode=pl.Buffered(1))


def _col_spec(rows, cols, c):
    return pl.BlockSpec((rows, cols), lambda *_: (0, c), pipeline_mode=pl.Buffered(1))


def _rms(x, g):
    return x * lax.rsqrt(jnp.mean(x * x, axis=-1, keepdims=True) + EPS) * g


def _dot(a, b):
    return jnp.dot(a, b, preferred_element_type=F32)


def _dot_nt(a, b):
    return lax.dot_general(a, b, (((1,), (1,)), ((), ())), preferred_element_type=F32)


def _dot_tn(a, b):
    return lax.dot_general(a, b, (((0,), (0,)), ((), ())), preferred_element_type=F32)


def _mlstm_proj_kernel(h_ref, g_ref, wq_ref, wk_ref, wv_ref, wo_ref, wg_ref,
                       q_ref, k_ref, v_ref, o_ref, gate_ref):
    a = _rms(h_ref[...], g_ref[...]).astype(BF16)
    q_ref[...] = (_dot(a, wq_ref[...]) * (MLSTM_DQK ** -0.5)).astype(BF16)
    k_ref[...] = _dot(a, wk_ref[...]).astype(BF16)
    v_ref[...] = _dot(a, wv_ref[...]).astype(BF16)
    o_ref[...] = _dot(a, wo_ref[...])
    gate_ref[...] = _dot(a, wg_ref[...])


def _mlstm_proj(h, gain, w_in):
    S, D = h.shape
    NH, DK, DV = MLSTM_HEADS, MLSTM_DQK, MLSTM_DV
    nq, nv = NH * DK, NH * DV
    assert 2 * nq == nv
    w = w_in.astype(BF16)
    wg = jnp.pad(w_in[:, 2 * nq + 2 * nv:], ((0, 0), (0, LANES - 2 * NH))).astype(BF16)
    tm = min(ROW_TILE, S)
    row = lambda n: pl.BlockSpec((tm, n), lambda i: (i, 0))
    return pl.pallas_call(
        _mlstm_proj_kernel,
        grid=(S // tm,),
        in_specs=[row(D), _const_spec((1, D)), _col_spec(D, nq, 0), _col_spec(D, nq, 1),
                  _col_spec(D, nv, 1), _col_spec(D, nv, 2), _const_spec((D, LANES))],
        out_specs=[row(nq), row(nq), row(nv), row(nv), row(LANES)],
        out_shape=[jax.ShapeDtypeStruct((S, nq), BF16), jax.ShapeDtypeStruct((S, nq), BF16),
                   jax.ShapeDtypeStruct((S, nv), BF16), jax.ShapeDtypeStruct((S, nv), F32),
                   jax.ShapeDtypeStruct((S, LANES), F32)],
        compiler_params=_cparams("arbitrary"),
        name="mlstm_proj",
    )(h, gain.reshape(1, D), w, w, w, w, wg)


def _mlstm_core_kernel(q_ref, k_ref, v_ref, o_ref, gate_ref, bg_ref, hg_ref, out_ref,
                       c_ref, n_ref, m_ref):
    NH, DK, DV, L = MLSTM_HEADS, MLSTM_DQK, MLSTM_DV, MLSTM_CHUNK

    @pl.when(pl.program_id(0) == 0)
    def _():
        c_ref[...] = jnp.zeros_like(c_ref)
        n_ref[...] = jnp.zeros_like(n_ref)
        m_ref[...] = jnp.zeros_like(m_ref)

    for c in range(q_ref.shape[0] // L):
        _mlstm_chunk(slice(c * L, (c + 1) * L), q_ref, k_ref, v_ref, o_ref, gate_ref, bg_ref, hg_ref, out_ref,
                     c_ref, n_ref, m_ref)


def _mlstm_chunk(rows, q_ref, k_ref, v_ref, o_ref, gate_ref, bg_ref, hg_ref, out_ref, c_ref, n_ref, m_ref):
    NH, DK, DV, L = MLSTM_HEADS, MLSTM_DQK, MLSTM_DV, MLSTM_CHUNK
    g = gate_ref[rows, :] + bg_ref[...]
    lane = lax.broadcasted_iota(jnp.int32, (L, LANES), 1)
    log_sig = jnp.minimum(g, 0.0) - jnp.log(1.0 + jnp.exp(-jnp.abs(g)))
    gl = jnp.where(lane >= NH, log_sig, g)
    gl_t = gl.T
    t_idx = lax.broadcasted_iota(jnp.int32, (L, L), 0)
    s_idx = lax.broadcasted_iota(jnp.int32, (L, L), 1)
    causal = s_idx <= t_idx

    heads = range(NH)
    i_col = [gl[:, hd:hd + 1] for hd in heads]
    f_col = [gl[:, NH + hd:NH + hd + 1] for hd in heads]
    i_row = [gl_t[hd:hd + 1, :] for hd in heads]
    f_row = [gl_t[NH + hd:NH + hd + 1, :] for hd in heads]
    b_col = [jnp.sum(jnp.where(causal, f_row[hd], 0.0), axis=1, keepdims=True) for hd in heads]
    b_row = [jnp.sum(jnp.where(t_idx <= s_idx, f_col[hd], 0.0), axis=0, keepdims=True) for hd in heads]
    m_prev = [m_ref[hd] for hd in heads]
    dmat = [jnp.where(causal, b_col[hd] - b_row[hd] + i_row[hd], -jnp.inf) for hd in heads]
    inter = [b_col[hd] + m_prev[hd] for hd in heads]
    m_t = [jnp.maximum(inter[hd], jnp.max(dmat[hd], axis=1, keepdims=True)) for hd in heads]
    w_intra = [jnp.exp(dmat[hd] - m_t[hd]) for hd in heads]
    w_inter = [jnp.exp(inter[hd] - m_t[hd]) for hd in heads]

    q = [q_ref[rows, hd * DK:(hd + 1) * DK] for hd in heads]
    k = [k_ref[rows, hd * DK:(hd + 1) * DK] for hd in heads]
    v = [v_ref[rows, hd * DV:(hd + 1) * DV] for hd in heads]
    c_old = [c_ref[hd] for hd in heads]
    n_old = [n_ref[hd] for hd in heads]
    s = [_dot_nt(q[hd], k[hd]) * w_intra[hd] for hd in heads]
    q_c = [_dot(q[hd], c_old[hd].astype(BF16)) for hd in heads]
    num = [_dot(s[hd].astype(BF16), v[hd]) + w_inter[hd] * q_c[hd] for hd in heads]
    qn = [jnp.sum(q[hd].astype(F32) * n_old[hd], axis=1, keepdims=True) for hd in heads]
    den = [jnp.sum(s[hd], axis=1, keepdims=True) + w_inter[hd] * qn[hd] for hd in heads]
    den = [jnp.maximum(jnp.abs(den[hd]), jnp.exp(-m_t[hd])) for hd in heads]
    h_out = [num[hd] / den[hd] for hd in heads]

    g_tot = [b_col[hd][L - 1:L, :] for hd in heads]
    a_col = [g_tot[hd] - b_col[hd] + i_col[hd] for hd in heads]
    m_new = [jnp.maximum(g_tot[hd] + m_prev[hd], jnp.max(a_col[hd], axis=0, keepdims=True)) for hd in heads]
    decay = [jnp.exp(g_tot[hd] + m_prev[hd] - m_new[hd]) for hd in heads]
    kw = [k[hd].astype(F32) * jnp.exp(a_col[hd] - m_new[hd]) for hd in heads]
    for hd in heads:
        c_ref[hd] = decay[hd] * c_old[hd] + _dot_tn(kw[hd].astype(BF16), v[hd])
        n_ref[hd] = decay[hd] * n_old[hd] + jnp.sum(kw[hd], axis=0, keepdims=True)
        m_ref[hd] = m_new[hd]

    for hd in heads:
        hn = h_out[hd] * lax.rsqrt(jnp.mean(h_out[hd] * h_out[hd], axis=1, keepdims=True) + EPS)
        hn = hn * hg_ref[:, hd * DV:(hd + 1) * DV]
        out = hn * jax.nn.sigmoid(o_ref[rows, hd * DV:(hd + 1) * DV])
        out_ref[rows, hd * DV:(hd + 1) * DV] = out.astype(BF16)


def _mlstm_core(q, k, v, o, gates, b_gate, head_gain):
    S = q.shape[0]
    NH, DK, DV, L = MLSTM_HEADS, MLSTM_DQK, MLSTM_DV, MLSTM_CHUNK
    bg = jnp.pad(b_gate.astype(F32), (0, LANES - 2 * NH)).reshape(1, LANES)
    rows = MLSTM_STEP_CHUNKS * L
    assert S % rows == 0
    row = lambda n: pl.BlockSpec((rows, n), lambda c: (c, 0))
    return pl.pallas_call(
        _mlstm_core_kernel,
        grid=(S // rows,),
        in_specs=[row(NH * DK), row(NH * DK), row(NH * DV), row(NH * DV), row(LANES),
                  _const_spec((1, LANES)), _const_spec((1, NH * DV))],
        out_specs=row(NH * DV),
        out_shape=jax.ShapeDtypeStruct((S, NH * DV), BF16),
        scratch_shapes=[pltpu.VMEM((NH, DK, DV), F32), pltpu.VMEM((NH, 1, DK), F32),
                        pltpu.VMEM((NH, 1, 1), F32)],
        compiler_params=_cparams("arbitrary"),
        name="mlstm_core",
    )(q, k, v, o, gates, bg, head_gain.astype(F32).reshape(1, NH * DV))


def _rope_tile(x, cos, sin_lo, sin_hi):
    half = ROPE_DIMS // 2
    cols = []
    for c in range(x.shape[1] // LANES):
        xc = x[:, c * LANES:(c + 1) * LANES]
        up = pltpu.roll(xc, LANES - half, axis=1)
        down = pltpu.roll(xc, half, axis=1)
        cols.append(xc * cos + up * sin_lo + down * sin_hi)
    return jnp.concatenate(cols, axis=1)


def _qkv_proj_kernel(*refs, rope):
    if rope:
        (h_ref, g_ref, wq_ref, wk_ref, wv_ref, pos_ref, inv_ref,
         q_ref, k_ref, v_ref, kmean_ref) = refs
    else:
        h_ref, g_ref, wq_ref, wk_ref, wv_ref, q_ref, k_ref, v_ref = refs
    a = _rms(h_ref[...], g_ref[...]).astype(BF16)
    q = _dot(a, wq_ref[...])
    k = _dot(a, wk_ref[...])
    v = _dot(a, wv_ref[...])
    if not rope:
        v_ref[...] = v.astype(BF16)
    q = q * (ATTN_DH ** -0.5 * LOG2E)
    if rope:
        tm = q.shape[0]
        row = lax.broadcasted_iota(jnp.int32, (VT_ROWS - ATTN_DH, MOBA_BLOCK), 0)
        ones_pad = jnp.where(row == 0, 1.0, 0.0).astype(BF16)
        for b in range(tm // MOBA_BLOCK):
            vt = v[b * MOBA_BLOCK:(b + 1) * MOBA_BLOCK, :].T.astype(BF16)
            for hd in range(ATTN_HEADS):
                v_ref[b, hd * VT_ROWS:hd * VT_ROWS + ATTN_DH, :] = vt[hd * ATTN_DH:(hd + 1) * ATTN_DH, :]
                v_ref[b, hd * VT_ROWS + ATTN_DH:(hd + 1) * VT_ROWS, :] = ones_pad
        half = ROPE_DIMS // 2
        ang = pos_ref[...].astype(F32) * inv_ref[...]
        cos = jnp.cos(ang)
        sin = jnp.sin(ang)
        dim = lax.broadcasted_iota(jnp.int32, (tm, LANES), 1) % ATTN_DH
        sin_lo = jnp.where(dim < half, -sin, 0.0)
        sin_hi = jnp.where((dim >= half) & (dim < ROPE_DIMS), sin, 0.0)
        q = _rope_tile(q, cos, sin_lo, sin_hi)
        k = _rope_tile(k, cos, sin_lo, sin_hi)
        nblk = tm // MOBA_BLOCK
        kmean_ref[0] = jnp.sum(k.reshape(nblk, MOBA_BLOCK, k.shape[1]), axis=1) * (1.0 / MOBA_BLOCK)
    q_ref[...] = q.astype(BF16)
    k_ref[...] = k.astype(BF16)


def _qkv_proj(h, gain, w_qkv, positions=None):
    S, D = h.shape
    rope = positions is not None
    w = w_qkv.astype(BF16)
    tm = min(ROW_TILE, S)
    row = lambda n: pl.BlockSpec((tm, n), lambda i: (i, 0))
    in_specs = [row(D), _const_spec((1, D)), _col_spec(D, D, 0), _col_spec(D, D, 1), _col_spec(D, D, 2)]
    args = [h, gain.reshape(1, D), w, w, w]
    out_specs = [row(D), row(D), row(D)]
    out_shape = [jax.ShapeDtypeStruct((S, D), BF16)] * 3
    if rope:
        assert S % MOBA_BLOCK == 0 and tm % MOBA_BLOCK == 0
        half = ROPE_DIMS // 2
        inv = ROPE_THETA ** (-jnp.arange(half, dtype=F32) / half)
        dim = jnp.arange(LANES) % ATTN_DH
        inv_lane = jnp.where(dim < ROPE_DIMS, inv[dim % half], 0.0).astype(F32).reshape(1, LANES)
        in_specs += [row(1), _const_spec((1, LANES))]
        args += [positions.reshape(S, 1), inv_lane]
        nblk = tm // MOBA_BLOCK
        vt_rows = ATTN_HEADS * VT_ROWS
        out_specs[2] = pl.BlockSpec((nblk, vt_rows, MOBA_BLOCK), lambda i: (i, 0, 0))
        out_shape[2] = jax.ShapeDtypeStruct((S // MOBA_BLOCK, vt_rows, MOBA_BLOCK), BF16)
        out_specs.append(pl.BlockSpec((1, nblk, D), lambda i: (i, 0, 0)))
        out_shape.append(jax.ShapeDtypeStruct((S // tm, nblk, D), F32))
    outs = pl.pallas_call(
        functools.partial(_qkv_proj_kernel, rope=rope),
        grid=(S // tm,),
        in_specs=in_specs,
        out_specs=out_specs,
        out_shape=out_shape,
        compiler_params=_cparams("arbitrary"),
        name="moba_proj" if rope else "sb_proj",
    )(*args)
    if rope:
        q, k, v, kmean = outs
        return q, k, v, kmean.reshape(S // MOBA_BLOCK, D)
    return outs


def _split_pair(q):
    lane = lax.broadcasted_iota(jnp.int32, q.shape, 1)
    zero = jnp.zeros_like(q)
    return jnp.where(lane < ATTN_DH, q, zero), jnp.where(lane >= ATTN_DH, q, zero)


def _merge_pair(acc_a, acc_b):
    lane = lax.broadcasted_iota(jnp.int32, acc_a.shape, 1)
    return jnp.where(lane < ATTN_DH, acc_a, acc_b)


def _moba_select(gate, cur):
    nb = gate.shape[0]
    blk = lax.broadcasted_iota(jnp.int32, gate.shape, 0)
    valid = blk < cur
    g = jnp.where(valid, gate, -jnp.inf)
    sel = jnp.zeros(gate.shape, F32)
    for _ in range(MOBA_TOPK):
        mx = jnp.max(g, axis=0, keepdims=True)
        first = jnp.min(jnp.where(g == mx, blk, nb), axis=0, keepdims=True)
        hit = (blk == first) & valid
        sel = jnp.where(hit, 1.0, sel)
        g = jnp.where(blk == first, -jnp.inf, g)
    return sel


def _moba_attn_kernel(q_ref, k_ref, vt_ref, kmean_ref, out_ref,
                      sel_ref, s_ref, cm_ref, p_ref, alpha_ref, m_ref, acc_ref):
    T = ATTN_TILE
    H = ATTN_DH
    R = VT_ROWS
    TQ = q_ref.shape[0]
    first_own = pl.program_id(1) * (TQ // T)

    @pl.when(first_own == 0)
    def _():
        p_ref[...] = jnp.zeros(p_ref.shape, BF16)
        alpha_ref[...] = jnp.ones(alpha_ref.shape, F32)

    qt = q_ref[...].astype(F32).T
    dim = lax.broadcasted_iota(jnp.int32, qt.shape, 0)
    qts = [jnp.where(dim < H, qt, 0.0).astype(BF16), jnp.where(dim >= H, qt, 0.0).astype(BF16)]
    km = kmean_ref[...].astype(BF16)
    cur = first_own + lax.broadcasted_iota(jnp.int32, (1, TQ), 1) // T
    key = lax.broadcasted_iota(jnp.int32, (T, T), 0)
    query = lax.broadcasted_iota(jnp.int32, (T, T), 1)
    causal = key <= query

    G = MOBA_GROUP
    nb = sel_ref.shape[1]

    def select(x):
        sel_ref[x] = _moba_select(_dot(km, qts[x]), cur)

    def own_block(x, h):
        own = first_own + h
        k_own = k_ref[pl.ds(pl.multiple_of(own * T, T), T), :]
        s = jnp.where(causal, _dot(k_own, qts[x][:, h * T:(h + 1) * T]), NEG)
        m = jnp.max(s, axis=0, keepdims=True)
        m_ref[x, :, h * T:(h + 1) * T] = m
        acc_ref[x, :, h * T:(h + 1) * T] = _dot(vt_ref[own, x * R:(x + 1) * R, :], jnp.exp2(s - m).astype(BF16))

    def score_block(t, slot, g, x):
        k_b = k_ref[pl.ds(pl.multiple_of((t * G + g) * T, T), T), :]
        s = _dot(k_b, qts[x])
        s_ref[slot, x, g * T:(g + 1) * T, :] = s
        cm_ref[slot, x, g:g + 1, :] = jnp.max(s, axis=0, keepdims=True)

    vpu_work = [lambda x=x: select(x) for x in range(2)]
    vpu_work += [lambda x=x, h=h: own_block(x, h) for x in range(2) for h in range(TQ // T)]
    mxu_work = [lambda g=g, x=x: score_block(0, 0, g, x) for g in range(G) for x in range(2)]
    for n in range(max(len(vpu_work), len(mxu_work))):
        if n < len(mxu_work):
            mxu_work[n]()
        if n < len(vpu_work):
            vpu_work[n]()

    def fold(t, slot):
        for x in range(2):
            part = _dot(vt_ref[t * G, x * R:(x + 1) * R, :], p_ref[slot, x, 0:T, :])
            for g in range(1, G):
                part = part + _dot(vt_ref[t * G + g, x * R:(x + 1) * R, :], p_ref[slot, x, g * T:(g + 1) * T, :])
            acc_ref[x] = alpha_ref[slot, x] * acc_ref[x] + part

    def trip(t, slot):
        prev = 1 - slot
        live = t > 0
        gate = jnp.where(live, 1.0, 0.0)
        t_prev = jnp.maximum(t - 1, 0)
        t_next = jnp.minimum(t + 1, nb // G - 1)
        shifts = []
        for x in range(2):
            chosen = [sel_ref[x, pl.ds(t * G + g, 1), :] > 0.0 for g in range(G)]
            m = m_ref[x]
            m_new = m
            for g in range(G):
                m_new = jnp.maximum(m_new, jnp.where(chosen[g], cm_ref[slot, x, g:g + 1, :], NEG))
            m_ref[x] = m_new
            alpha_ref[slot, x] = jnp.exp2(m - m_new)
            shifts.append([jnp.where(chosen[g], m_new, BIG) for g in range(G)])
            acc_ref[x] = jnp.where(live, alpha_ref[prev, x], 1.0) * acc_ref[x]
        for g in range(G):
            rows = slice(g * T, (g + 1) * T)
            for x in range(2):
                score_block(t_next, prev, g, x)
                p_ref[slot, x, rows, :] = jnp.exp2(s_ref[slot, x, rows, :] - shifts[x][g]).astype(BF16)
                acc_ref[x] += gate * _dot(vt_ref[t_prev * G + g, x * R:(x + 1) * R, :], p_ref[prev, x, rows, :])

    trips = (first_own + TQ // T - 1 + G - 1) // G

    def two_trips(u, _):
        trip(2 * u, 0)

        @pl.when(2 * u + 1 < trips)
        def _():
            trip(2 * u + 1, 1)
        return 0

    lax.fori_loop(0, (trips + 1) // 2, two_trips, 0)
    for slot in range(2):
        @pl.when(jnp.logical_and(trips > 0, (trips - 1) % 2 == slot))
        def _():
            fold(trips - 1, slot)
    out_t = jnp.concatenate([acc_ref[x][:H, :] / acc_ref[x][H:H + 1, :] for x in range(2)], axis=0)
    out_ref[...] = out_t.T.astype(BF16)


def _attn_specs(S):
    tile = pl.BlockSpec((ATTN_TILE, LANES), lambda p, i: (i, p))
    resident = pl.BlockSpec((S, LANES), lambda p, i: (0, p))
    return tile, resident


def _moba_attn(q, k, v, kmean):
    S, D = q.shape
    assert MOBA_BLOCK == ATTN_TILE and ATTN_TILE % Q_BLOCK == 0 and S % ATTN_TILE == 0
    nb = S // MOBA_BLOCK
    assert nb % MOBA_GROUP == 0
    tq = min(MOBA_QUERIES, S)
    assert tq % ATTN_TILE == 0 and S % tq == 0
    tile = pl.BlockSpec((tq, LANES), lambda p, i: (i, p))
    _, resident = _attn_specs(S)
    return pl.pallas_call(
        _moba_attn_kernel,
        grid=(D // LANES, S // tq),
        in_specs=[tile, resident, pl.BlockSpec((nb, 2 * VT_ROWS, MOBA_BLOCK), lambda p, i: (0, p, 0)),
                  pl.BlockSpec((nb, LANES), lambda p, i: (0, p))],
        out_specs=tile,
        out_shape=jax.ShapeDtypeStruct((S, D), BF16),
        scratch_shapes=[pltpu.VMEM((2, nb, tq), F32),
                        pltpu.VMEM((2, 2, MOBA_GROUP * ATTN_TILE, tq), F32),
                        pltpu.VMEM((2, 2, MOBA_GROUP, tq), F32),
                        pltpu.VMEM((2, 2, MOBA_GROUP * ATTN_TILE, tq), BF16),
                        pltpu.VMEM((2, 2, 1, tq), F32),
                        pltpu.VMEM((2, 1, tq), F32), pltpu.VMEM((2, VT_ROWS, tq), F32)],
        compiler_params=_cparams("arbitrary", "arbitrary"),
        name="moba_attn",
    )(q, k, v, kmean)


def _sb_attn_kernel(q_ref, k_ref, v_ref, out_ref):
    T = SB_TILE
    i = pl.program_id(1)
    qs = _split_pair(q_ref[...])
    key_row = lax.broadcasted_iota(jnp.int32, (T, T), 0)
    key_col = lax.broadcasted_iota(jnp.int32, (T, T), 1)
    later_keys = (key_row > key_col).astype(BF16)
    strict = key_col < key_row

    def walk(tiles, carry):
        heads = range(len(qs))
        k_t = [k_ref[pl.ds(pl.multiple_of(j * T, T), T), :] for j, _, _ in tiles]
        v_t = [v_ref[pl.ds(pl.multiple_of(j * T, T), T), :] for j, _, _ in tiles]
        z2 = [[_dot_nt(qs[x], k_j) for x in heads] for k_j in k_t]
        sp = [[jnp.maximum(z, 0.0) + jnp.log2(1.0 + jnp.exp2(-jnp.abs(z))) for z in zs] for zs in z2]
        sp = [[jnp.where(strict, s, 0.0) if masked else s for s in ss] for ss, (_, masked, _) in zip(sp, tiles)]
        hi = [[s.astype(BF16) for s in ss] for ss in sp]
        lo = [[(s - h.astype(F32)).astype(BF16) for s, h in zip(ss, hs)] for ss, hs in zip(sp, hi)]
        log_sig = [[z - s for z, s in zip(zs, ss)] for zs, ss in zip(z2, sp)]
        sp_sums = [[jnp.sum(s, axis=1, keepdims=True) for s in ss] for ss in sp]
        later = [[_dot(h, later_keys) + _dot(l, later_keys) for h, l in zip(hs, ls)] for hs, ls in zip(hi, lo)]
        run = [carry[2 * x] for x in heads]
        acc = [carry[2 * x + 1] for x in heads]
        for n, (_, masked, weight) in enumerate(tiles):
            for x in heads:
                a = jnp.exp2(log_sig[n][x] - (later[n][x] + run[x]))
                if masked:
                    a = jnp.where(strict, a, 0.0)
                sp_sum = sp_sums[n][x]
                av = _dot(a.astype(BF16), v_t[n])
                if weight is not None:
                    sp_sum, av = weight * sp_sum, weight * av
                run[x] = run[x] + sp_sum
                acc[x] = acc[x] + av
        return tuple(val for x in heads for val in (run[x], acc[x]))

    def least_run(carry):
        return jnp.min(jnp.minimum(carry[0], carry[2]))

    def more(state):
        j, least = state[0], state[1]
        return jnp.logical_and(j >= 0, least < SB_RUN_CUTOFF)

    def step(state):
        carry = walk([(state[0], False, None)], state[2:])
        return (state[0] - 1, least_run(carry)) + carry

    zero = (jnp.zeros((T, 1), F32), jnp.zeros((T, LANES), F32))
    near = [(i, True, None)]
    near += [(jnp.maximum(i - n, 0), False, jnp.where(i >= n, 1.0, 0.0)) for n in range(1, SB_NEAR_TILES + 1)]
    carry = walk(near, zero + zero)
    state = lax.while_loop(more, step, (i - 1 - SB_NEAR_TILES, least_run(carry)) + carry)
    out_ref[...] = _merge_pair(state[3], state[5]).astype(BF16)


def _sb_attn(q, k, v):
    S, D = q.shape
    assert S % SB_TILE == 0
    _, resident = _attn_specs(S)
    tile = pl.BlockSpec((SB_TILE, LANES), lambda p, i: (i, p))
    return pl.pallas_call(
        _sb_attn_kernel,
        grid=(D // LANES, S // SB_TILE),
        in_specs=[tile, resident, resident],
        out_specs=tile,
        out_shape=jax.ShapeDtypeStruct((S, D), BF16),
        compiler_params=_cparams("arbitrary", "arbitrary"),
        name="sb_attn",
    )(q, k, v)


def _conv_proj_kernel(h_ref, g_ref, wb_ref, wc_ref, wu_ref, cw_ref, out_ref, tail_ref):
    tm = h_ref.shape[0]

    @pl.when(pl.program_id(0) == 0)
    def _():
        tail_ref[...] = jnp.zeros_like(tail_ref)

    a = _rms(h_ref[...], g_ref[...]).astype(BF16)
    z = _dot(a, wc_ref[...]) * _dot(a, wu_ref[...])
    row = lax.broadcasted_iota(jnp.int32, z.shape, 0)
    prev1 = tail_ref[7:8, :]
    prev2 = tail_ref[6:7, :]
    z1 = jnp.where(row == 0, prev1, pltpu.roll(z, 1, axis=0))
    z2 = jnp.where(row == 0, prev2, jnp.where(row == 1, prev1, pltpu.roll(z, 2, axis=0)))
    y = cw_ref[0:1, :] * z2 + cw_ref[1:2, :] * z1 + cw_ref[2:3, :] * z
    tail_ref[...] = z[tm - 8:, :]
    out_ref[...] = (_dot(a, wb_ref[...]) * y).astype(BF16)


def _conv_mixer_pre(h, gain, w_in, conv_w):
    S, D = h.shape
    assert conv_w.shape[0] == CONV_WIDTH == 3
    w = w_in.astype(BF16)
    cw = jnp.pad(conv_w.astype(F32), ((0, 8 - CONV_WIDTH), (0, 0)))
    tm = min(ROW_TILE, S)
    row = pl.BlockSpec((tm, D), lambda i: (i, 0))
    return pl.pallas_call(
        _conv_proj_kernel,
        grid=(S // tm,),
        in_specs=[row, _const_spec((1, D)), _col_spec(D, D, 0), _col_spec(D, D, 1), _col_spec(D, D, 2),
                  _const_spec((8, D))],
        out_specs=row,
        out_shape=jax.ShapeDtypeStruct((S, D), BF16),
        scratch_shapes=[pltpu.VMEM((8, D), F32)],
        compiler_params=_cparams("arbitrary"),
        name="conv_proj",
    )(h, gain.reshape(1, D), w, w, w, cw)


def _post_kernel(*refs, final):
    if final:
        h_ref, x_ref, wo_ref, g_ref, wg_ref, wu_ref, wd_ref, fg_ref, out_ref = refs
    else:
        h_ref, x_ref, wo_ref, g_ref, wg_ref, wu_ref, wd_ref, out_ref = refs
    h1 = h_ref[...] + _dot(x_ref[...], wo_ref[...])
    a = _rms(h1, g_ref[...]).astype(BF16)
    F = wg_ref.shape[1]
    y = h1
    for lo in range(0, F, FFN_CHUNK):
        hi = min(lo + FFN_CHUNK, F)
        gate = _dot(a, wg_ref[:, lo:hi])
        up = _dot(a, wu_ref[:, lo:hi])
        act = (gate * jax.nn.sigmoid(gate) * up).astype(BF16)
        y = y + _dot(act, wd_ref[lo:hi, :])
    if final:
        y = _rms(y, fg_ref[...])
    out_ref[...] = y


def _post(h, x, w_out, gain, layer, w_gate, w_up, w_down, final_gain=None):
    S, D = h.shape
    K = x.shape[1]
    F = w_gate.shape[2]
    assert F % LANES == 0
    final = final_gain is not None
    tm = min(ROW_TILE, S)

    def layer_spec(rows, cols):
        return pl.BlockSpec((None, rows, cols), lambda i: (layer, 0, 0), pipeline_mode=pl.Buffered(1))

    in_specs = [pl.BlockSpec((tm, D), lambda i: (i, 0)), pl.BlockSpec((tm, K), lambda i: (i, 0)),
                _const_spec((K, D)), _const_spec((1, D)), layer_spec(D, F), layer_spec(D, F), layer_spec(F, D)]
    args = [h, x, w_out.astype(BF16), gain.reshape(1, D), w_gate.astype(BF16), w_up.astype(BF16),
            w_down.astype(BF16)]
    if final:
        in_specs.append(_const_spec((1, D)))
        args.append(final_gain.reshape(1, D))
    return pl.pallas_call(
        functools.partial(_post_kernel, final=final),
        grid=(S // tm,),
        in_specs=in_specs,
        out_specs=pl.BlockSpec((tm, D), lambda i: (i, 0)),
        out_shape=jax.ShapeDtypeStruct((S, D), F32),
        compiler_params=_cparams("arbitrary"),
        name="post_final" if final else "post",
    )(*args)


def _mixer(kind, j, h, gain, positions, p):
    if kind == 0:
        q, k, v, o, gates = _mlstm_proj(h, gain, p["mlstm_w_in"][j])
        return _mlstm_core(q, k, v, o, gates, p["mlstm_b_gate"][j], p["mlstm_head_gain"][j]), p["mlstm_w_out"][j]
    if kind == 1:
        q, k, v, kmean = _qkv_proj(h, gain, p["moba_w_qkv"][j], positions)
        return _moba_attn(q, k, v, kmean), p["moba_w_out"][j]
    if kind == 2:
        return _conv_mixer_pre(h, gain, p["conv_w_in"][j], p["conv_w"][j]), p["conv_w_out"][j]
    q, k, v = _qkv_proj(h, gain, p["sb_w_qkv"][j])
    return _sb_attn(q, k, v), p["sb_w_out"][j]


def kernel(x, positions, norm_gains, mlstm_w_in, mlstm_b_gate, mlstm_head_gain, mlstm_w_out, moba_w_qkv, moba_w_out, conv_w_in, conv_w, conv_w_out, sb_w_qkv, sb_w_out, ffn_w_gate, ffn_w_up, ffn_w_down, final_gain):
    B, S, D = x.shape
    assert D == D_MODEL
    depth = norm_gains.shape[0]
    p = dict(mlstm_w_in=mlstm_w_in, mlstm_b_gate=mlstm_b_gate, mlstm_head_gain=mlstm_head_gain,
             mlstm_w_out=mlstm_w_out, moba_w_qkv=moba_w_qkv, moba_w_out=moba_w_out,
             conv_w_in=conv_w_in, conv_w=conv_w, conv_w_out=conv_w_out,
             sb_w_qkv=sb_w_qkv, sb_w_out=sb_w_out)
    outs = []
    for b in range(B):
        h = x[b]
        for layer in range(depth):
            kind, j = layer % 4, layer // 4
            mix, w_out = _mixer(kind, j, h, norm_gains[layer, 0], positions[b], p)
            h = _post(h, mix, w_out, norm_gains[layer, 1], layer, ffn_w_gate, ffn_w_up, ffn_w_down,
                      final_gain if layer == depth - 1 else None)
        outs.append(h)
    return jnp.stack(outs)
```

```python
import functools

import jax
import jax.numpy as jnp
from jax import lax
from jax.experimental import pallas as pl
from jax.experimental.pallas import tpu as pltpu

F32 = jnp.float32
BF16 = jnp.bfloat16

EPS = 1e-6
NEG = -1e30
D_MODEL = 1024
MLSTM_HEADS = 4
MLSTM_DQK = 128
MLSTM_DV = 256
MLSTM_CHUNK = 128
MLSTM_STEP_CHUNKS = 4
ATTN_HEADS = 16
ATTN_DH = 64
ROPE_DIMS = 16
ROPE_THETA = 500000.0
MOBA_BLOCK = 256
MOBA_TOPK = 3
Q_BLOCK = 128
ATTN_TILE = 256
LOG2E = 1.4426950408889634
SB_TILE = 256
SB_NEAR_TILES = 1
SB_RUN_CUTOFF = 150.0
BF16_SUBLANES = 16
VT_ROWS = ATTN_DH + BF16_SUBLANES
BIG = 1e30
MOBA_QUERIES = 512
MOBA_GROUP = 4
CONV_WIDTH = 3

LANES = 128
ROW_TILE = 512
MXU_WIDTH = 256
FFN_CHUNK = 6 * MXU_WIDTH
VMEM_LIMIT = 56 * 1024 * 1024


def _cparams(*sem):
    return pltpu.CompilerParams(dimension_semantics=sem, vmem_limit_bytes=VMEM_LIMIT)


def _const_spec(shape):
    nd = len(shape)
    return pl.BlockSpec(shape, lambda *_: (0,) * nd, pipeline_mode=pl.Buffered(1))


def _col_spec(rows, cols, c):
    return pl.BlockSpec((rows, cols), lambda *_: (0, c), pipeline_mode=pl.Buffered(1))


def _rms(x, g):
    return x * lax.rsqrt(jnp.mean(x * x, axis=-1, keepdims=True) + EPS) * g


def _dot(a, b):
    return jnp.dot(a, b, preferred_element_type=F32)


def _dot_nt(a, b):
    return lax.dot_general(a, b, (((1,), (1,)), ((), ())), preferred_element_type=F32)


def _dot_tn(a, b):
    return lax.dot_general(a, b, (((0,), (0,)), ((), ())), preferred_element_type=F32)


def _mlstm_proj_kernel(h_ref, g_ref, wq_ref, wk_ref, wv_ref, wo_ref, wg_ref,
                       q_ref, k_ref, v_ref, o_ref, gate_ref):
    a = _rms(h_ref[...], g_ref[...]).astype(BF16)
    q_ref[...] = (_dot(a, wq_ref[...]) * (MLSTM_DQK ** -0.5)).astype(BF16)
    k_ref[...] = _dot(a, wk_ref[...]).astype(BF16)
    v_ref[...] = _dot(a, wv_ref[...]).astype(BF16)
    o_ref[...] = _dot(a, wo_ref[...])
    gate_ref[...] = _dot(a, wg_ref[...])


def _mlstm_proj(h, gain, w_in):
    S, D = h.shape
    NH, DK, DV = MLSTM_HEADS, MLSTM_DQK, MLSTM_DV
    nq, nv = NH * DK, NH * DV
    assert 2 * nq == nv
    w = w_in.astype(BF16)
    wg = jnp.pad(w_in[:, 2 * nq + 2 * nv:], ((0, 0), (0, LANES - 2 * NH))).astype(BF16)
    tm = min(ROW_TILE, S)
    row = lambda n: pl.BlockSpec((tm, n), lambda i: (i, 0))
    return pl.pallas_call(
        _mlstm_proj_kernel,
        grid=(S // tm,),
        in_specs=[row(D), _const_spec((1, D)), _col_spec(D, nq, 0), _col_spec(D, nq, 1),
                  _col_spec(D, nv, 1), _col_spec(D, nv, 2), _const_spec((D, LANES))],
        out_specs=[row(nq), row(nq), row(nv), row(nv), row(LANES)],
        out_shape=[jax.ShapeDtypeStruct((S, nq), BF16), jax.ShapeDtypeStruct((S, nq), BF16),
                   jax.ShapeDtypeStruct((S, nv), BF16), jax.ShapeDtypeStruct((S, nv), F32),
                   jax.ShapeDtypeStruct((S, LANES), F32)],
        compiler_params=_cparams("arbitrary"),
        name="mlstm_proj",
    )(h, gain.reshape(1, D), w, w, w, w, wg)


def _mlstm_core_kernel(q_ref, k_ref, v_ref, o_ref, gate_ref, bg_ref, hg_ref, out_ref,
                       c_ref, n_ref, m_ref):
    NH, DK, DV, L = MLSTM_HEADS, MLSTM_DQK, MLSTM_DV, MLSTM_CHUNK

    @pl.when(pl.program_id(0) == 0)
    def _():
        c_ref[...] = jnp.zeros_like(c_ref)
        n_ref[...] = jnp.zeros_like(n_ref)
        m_ref[...] = jnp.zeros_like(m_ref)

    for c in range(q_ref.shape[0] // L):
        _mlstm_chunk(slice(c * L, (c + 1) * L), q_ref, k_ref, v_ref, o_ref, gate_ref, bg_ref, hg_ref, out_ref,
                     c_ref, n_ref, m_ref)


def _mlstm_chunk(rows, q_ref, k_ref, v_ref, o_ref, gate_ref, bg_ref, hg_ref, out_ref, c_ref, n_ref, m_ref):
    NH, DK, DV, L = MLSTM_HEADS, MLSTM_DQK, MLSTM_DV, MLSTM_CHUNK
    g = gate_ref[rows, :] + bg_ref[...]
    lane = lax.broadcasted_iota(jnp.int32, (L, LANES), 1)
    log_sig = jnp.minimum(g, 0.0) - jnp.log(1.0 + jnp.exp(-jnp.abs(g)))
    gl = jnp.where(lane >= NH, log_sig, g)
    gl_t = gl.T
    t_idx = lax.broadcasted_iota(jnp.int32, (L, L), 0)
    s_idx = lax.broadcasted_iota(jnp.int32, (L, L), 1)
    causal = s_idx <= t_idx

    heads = range(NH)
    i_col = [gl[:, hd:hd + 1] for hd in heads]
    i_row = [gl_t[hd:hd + 1, :] for hd in heads]
    gl_hi = gl.astype(BF16)
    gl_lo = (gl - gl_hi.astype(F32)).astype(BF16)
    lower = causal.astype(BF16)
    cum = _dot(lower, gl_hi) + _dot(lower, gl_lo)
    cum_t = cum.T
    b_col = [cum[:, NH + hd:NH + hd + 1] for hd in heads]
    b_row = [cum_t[NH + hd:NH + hd + 1, :] for hd in heads]
    m_prev = [m_ref[hd] for hd in heads]
    dmat = [jnp.where(causal, b_col[hd] - b_row[hd] + i_row[hd], -jnp.inf) for hd in heads]
    inter = [b_col[hd] + m_prev[hd] for hd in heads]
    m_t = [jnp.maximum(inter[hd], jnp.max(dmat[hd], axis=1, keepdims=True)) for hd in heads]
    w_intra = [jnp.exp(dmat[hd] - m_t[hd]) for hd in heads]
    w_inter = [jnp.exp(inter[hd] - m_t[hd]) for hd in heads]

    q = [q_ref[rows, hd * DK:(hd + 1) * DK] for hd in heads]
    k = [k_ref[rows, hd * DK:(hd + 1) * DK] for hd in heads]
    v = [v_ref[rows, hd * DV:(hd + 1) * DV] for hd in heads]
    c_old = [c_ref[hd] for hd in heads]
    n_old = [n_ref[hd] for hd in heads]
    s = [_dot_nt(q[hd], k[hd]) * w_intra[hd] for hd in heads]
    q_c = [_dot(q[hd], c_old[hd].astype(BF16)) for hd in heads]
    num = [_dot(s[hd].astype(BF16), v[hd]) + w_inter[hd] * q_c[hd] for hd in heads]
    qn = [jnp.sum(q[hd].astype(F32) * n_old[hd], axis=1, keepdims=True) for hd in heads]
    den = [jnp.sum(s[hd], axis=1, keepdims=True) + w_inter[hd] * qn[hd] for hd in heads]
    den = [jnp.maximum(jnp.abs(den[hd]), jnp.exp(-m_t[hd])) for hd in heads]
    h_out = [num[hd] / den[hd] for hd in heads]

    g_tot = [b_col[hd][L - 1:L, :] for hd in heads]
    a_col = [g_tot[hd] - b_col[hd] + i_col[hd] for hd in heads]
    m_new = [jnp.maximum(g_tot[hd] + m_prev[hd], jnp.max(a_col[hd], axis=0, keepdims=True)) for hd in heads]
    decay = [jnp.exp(g_tot[hd] + m_prev[hd] - m_new[hd]) for hd in heads]
    kw = [k[hd].astype(F32) * jnp.exp(a_col[hd] - m_new[hd]) for hd in heads]
    for hd in heads:
        c_ref[hd] = decay[hd] * c_old[hd] + _dot_tn(kw[hd].astype(BF16), v[hd])
        n_ref[hd] = decay[hd] * n_old[hd] + jnp.sum(kw[hd], axis=0, keepdims=True)
        m_ref[hd] = m_new[hd]

    for hd in heads:
        hn = h_out[hd] * lax.rsqrt(jnp.mean(h_out[hd] * h_out[hd], axis=1, keepdims=True) + EPS)
        hn = hn * hg_ref[:, hd * DV:(hd + 1) * DV]
        out = hn * jax.nn.sigmoid(o_ref[rows, hd * DV:(hd + 1) * DV])
        out_ref[rows, hd * DV:(hd + 1) * DV] = out.astype(BF16)


def _mlstm_core(q, k, v, o, gates, b_gate, head_gain):
    S = q.shape[0]
    NH, DK, DV, L = MLSTM_HEADS, MLSTM_DQK, MLSTM_DV, MLSTM_CHUNK
    bg = jnp.pad(b_gate.astype(F32), (0, LANES - 2 * NH)).reshape(1, LANES)
    rows = MLSTM_STEP_CHUNKS * L
    assert S % rows == 0
    row = lambda n: pl.BlockSpec((rows, n), lambda c: (c, 0))
    return pl.pallas_call(
        _mlstm_core_kernel,
        grid=(S // rows,),
        in_specs=[row(NH * DK), row(NH * DK), row(NH * DV), row(NH * DV), row(LANES),
                  _const_spec((1, LANES)), _const_spec((1, NH * DV))],
        out_specs=row(NH * DV),
        out_shape=jax.ShapeDtypeStruct((S, NH * DV), BF16),
        scratch_shapes=[pltpu.VMEM((NH, DK, DV), F32), pltpu.VMEM((NH, 1, DK), F32),
                        pltpu.VMEM((NH, 1, 1), F32)],
        compiler_params=_cparams("arbitrary"),
        name="mlstm_core",
    )(q, k, v, o, gates, bg, head_gain.astype(F32).reshape(1, NH * DV))


def _rope_tile(x, cos, sin_lo, sin_hi):
    half = ROPE_DIMS // 2
    cols = []
    for c in range(x.shape[1] // LANES):
        xc = x[:, c * LANES:(c + 1) * LANES]
        up = pltpu.roll(xc, LANES - half, axis=1)
        down = pltpu.roll(xc, half, axis=1)
        cols.append(xc * cos + up * sin_lo + down * sin_hi)
    return jnp.concatenate(cols, axis=1)


def _qkv_proj_kernel(*refs, rope):
    if rope:
        (h_ref, g_ref, wq_ref, wk_ref, wv_ref, pos_ref, inv_ref,
         q_ref, k_ref, v_ref, kmean_ref) = refs
    else:
        h_ref, g_ref, wq_ref, wk_ref, wv_ref, q_ref, k_ref, v_ref = refs
    a = _rms(h_ref[...], g_ref[...]).astype(BF16)
    q = _dot(a, wq_ref[...])
    k = _dot(a, wk_ref[...])
    v = _dot(a, wv_ref[...])
    if not rope:
        v_ref[...] = v.astype(BF16)
    q = q * (ATTN_DH ** -0.5 * LOG2E)
    if rope:
        tm = q.shape[0]
        row = lax.broadcasted_iota(jnp.int32, (VT_ROWS - ATTN_DH, MOBA_BLOCK), 0)
        ones_pad = jnp.where(row == 0, 1.0, 0.0).astype(BF16)
        for b in range(tm // MOBA_BLOCK):
            vt = v[b * MOBA_BLOCK:(b + 1) * MOBA_BLOCK, :].T.astype(BF16)
            for hd in range(ATTN_HEADS):
                v_ref[b, hd * VT_ROWS:hd * VT_ROWS + ATTN_DH, :] = vt[hd * ATTN_DH:(hd + 1) * ATTN_DH, :]
                v_ref[b, hd * VT_ROWS + ATTN_DH:(hd + 1) * VT_ROWS, :] = ones_pad
        half = ROPE_DIMS // 2
        ang = pos_ref[...].astype(F32) * inv_ref[...]
        cos = jnp.cos(ang)
        sin = jnp.sin(ang)
        dim = lax.broadcasted_iota(jnp.int32, (tm, LANES), 1) % ATTN_DH
        sin_lo = jnp.where(dim < half, -sin, 0.0)
        sin_hi = jnp.where((dim >= half) & (dim < ROPE_DIMS), sin, 0.0)
        q = _rope_tile(q, cos, sin_lo, sin_hi)
        k = _rope_tile(k, cos, sin_lo, sin_hi)
        nblk = tm // MOBA_BLOCK
        kmean_ref[0] = jnp.sum(k.reshape(nblk, MOBA_BLOCK, k.shape[1]), axis=1) * (1.0 / MOBA_BLOCK)
    q_ref[...] = q.astype(BF16)
    k_ref[...] = k.astype(BF16)


def _qkv_proj(h, gain, w_qkv, positions=None):
    S, D = h.shape
    rope = positions is not None
    w = w_qkv.astype(BF16)
    tm = min(ROW_TILE, S)
    row = lambda n: pl.BlockSpec((tm, n), lambda i: (i, 0))
    in_specs = [row(D), _const_spec((1, D)), _col_spec(D, D, 0), _col_spec(D, D, 1), _col_spec(D, D, 2)]
    args = [h, gain.reshape(1, D), w, w, w]
    out_specs = [row(D), row(D), row(D)]
    out_shape = [jax.ShapeDtypeStruct((S, D), BF16)] * 3
    if rope:
        assert S % MOBA_BLOCK == 0 and tm % MOBA_BLOCK == 0
        half = ROPE_DIMS // 2
        inv = ROPE_THETA ** (-jnp.arange(half, dtype=F32) / half)
        dim = jnp.arange(LANES) % ATTN_DH
        inv_lane = jnp.where(dim < ROPE_DIMS, inv[dim % half], 0.0).astype(F32).reshape(1, LANES)
        in_specs += [row(1), _const_spec((1, LANES))]
        args += [positions.reshape(S, 1), inv_lane]
        nblk = tm // MOBA_BLOCK
        vt_rows = ATTN_HEADS * VT_ROWS
        out_specs[2] = pl.BlockSpec((nblk, vt_rows, MOBA_BLOCK), lambda i: (i, 0, 0))
        out_shape[2] = jax.ShapeDtypeStruct((S // MOBA_BLOCK, vt_rows, MOBA_BLOCK), BF16)
        out_specs.append(pl.BlockSpec((1, nblk, D), lambda i: (i, 0, 0)))
        out_shape.append(jax.ShapeDtypeStruct((S // tm, nblk, D), F32))
    outs = pl.pallas_call(
        functools.partial(_qkv_proj_kernel, rope=rope),
        grid=(S // tm,),
        in_specs=in_specs,
        out_specs=out_specs,
        out_shape=out_shape,
        compiler_params=_cparams("arbitrary"),
        name="moba_proj" if rope else "sb_proj",
    )(*args)
    if rope:
        q, k, v, kmean = outs
        return q, k, v, kmean.reshape(S // MOBA_BLOCK, D)
    return outs


def _split_pair(q):
    lane = lax.broadcasted_iota(jnp.int32, q.shape, 1)
    zero = jnp.zeros_like(q)
    return jnp.where(lane < ATTN_DH, q, zero), jnp.where(lane >= ATTN_DH, q, zero)


def _merge_pair(acc_a, acc_b):
    lane = lax.broadcasted_iota(jnp.int32, acc_a.shape, 1)
    return jnp.where(lane < ATTN_DH, acc_a, acc_b)


def _moba_select(gate, cur):
    nb = gate.shape[0]
    blk = lax.broadcasted_iota(jnp.int32, gate.shape, 0)
    valid = blk < cur
    g = jnp.where(valid, gate, -jnp.inf)
    sel = jnp.zeros(gate.shape, F32)
    for _ in range(MOBA_TOPK):
        mx = jnp.max(g, axis=0, keepdims=True)
        first = jnp.min(jnp.where(g == mx, blk, nb), axis=0, keepdims=True)
        hit = (blk == first) & valid
        sel = jnp.where(hit, 1.0, sel)
        g = jnp.where(blk == first, -jnp.inf, g)
    return sel


def _moba_attn_kernel(q_ref, k_ref, vt_ref, kmean_ref, out_ref,
                      sel_ref, s_ref, cm_ref, p_ref, alpha_ref, m_ref, acc_ref):
    T = ATTN_TILE
    H = ATTN_DH
    R = VT_ROWS
    TQ = q_ref.shape[0]
    first_own = pl.program_id(1) * (TQ // T)

    @pl.when(first_own == 0)
    def _():
        p_ref[...] = jnp.zeros(p_ref.shape, BF16)
        alpha_ref[...] = jnp.ones(alpha_ref.shape, F32)

    qt = q_ref[...].astype(F32).T
    dim = lax.broadcasted_iota(jnp.int32, qt.shape, 0)
    qts = [jnp.where(dim < H, qt, 0.0).astype(BF16), jnp.where(dim >= H, qt, 0.0).astype(BF16)]
    km = kmean_ref[...].astype(BF16)
    cur = first_own + lax.broadcasted_iota(jnp.int32, (1, TQ), 1) // T
    key = lax.broadcasted_iota(jnp.int32, (T, T), 0)
    query = lax.broadcasted_iota(jnp.int32, (T, T), 1)
    causal = key <= query

    G = MOBA_GROUP
    nb = sel_ref.shape[1]

    def select(x):
        sel_ref[x] = _moba_select(_dot(km, qts[x]), cur)

    def own_block(x, h):
        own = first_own + h
        k_own = k_ref[pl.ds(pl.multiple_of(own * T, T), T), :]
        s = jnp.where(causal, _dot(k_own, qts[x][:, h * T:(h + 1) * T]), NEG)
        m = jnp.max(s, axis=0, keepdims=True)
        m_ref[x, :, h * T:(h + 1) * T] = m
        acc_ref[x, :, h * T:(h + 1) * T] = _dot(vt_ref[own, x * R:(x + 1) * R, :], jnp.exp2(s - m).astype(BF16))

    def score_block(t, slot, g, x):
        k_b = k_ref[pl.ds(pl.multiple_of((t * G + g) * T, T), T), :]
        s = _dot(k_b, qts[x])
        s_ref[slot, x, g * T:(g + 1) * T, :] = s
        cm_ref[slot, x, g:g + 1, :] = jnp.max(s, axis=0, keepdims=True)

    vpu_work = [lambda x=x: select(x) for x in range(2)]
    vpu_work += [lambda x=x, h=h: own_block(x, h) for x in range(2) for h in range(TQ // T)]
    mxu_work = [lambda g=g, x=x: score_block(0, 0, g, x) for g in range(G) for x in range(2)]
    for n in range(max(len(vpu_work), len(mxu_work))):
        if n < len(mxu_work):
            mxu_work[n]()
        if n < len(vpu_work):
            vpu_work[n]()

    def fold(t, slot):
        for x in range(2):
            part = _dot(vt_ref[t * G, x * R:(x + 1) * R, :], p_ref[slot, x, 0:T, :])
            for g in range(1, G):
                part = part + _dot(vt_ref[t * G + g, x * R:(x + 1) * R, :], p_ref[slot, x, g * T:(g + 1) * T, :])
            acc_ref[x] = alpha_ref[slot, x] * acc_ref[x] + part

    def trip(t, slot):
        prev = 1 - slot
        live = t > 0
        gate = jnp.where(live, 1.0, 0.0)
        t_prev = jnp.maximum(t - 1, 0)
        t_next = jnp.minimum(t + 1, nb // G - 1)
        shifts = []
        for x in range(2):
            chosen = [sel_ref[x, pl.ds(t * G + g, 1), :] > 0.0 for g in range(G)]
            m = m_ref[x]
            m_new = m
            for g in range(G):
                m_new = jnp.maximum(m_new, jnp.where(chosen[g], cm_ref[slot, x, g:g + 1, :], NEG))
            m_ref[x] = m_new
            alpha_ref[slot, x] = jnp.exp2(m - m_new)
            shifts.append([jnp.where(chosen[g], m_new, BIG) for g in range(G)])
            acc_ref[x] = jnp.where(live, alpha_ref[prev, x], 1.0) * acc_ref[x]
        for g in range(G):
            rows = slice(g * T, (g + 1) * T)
            for x in range(2):
                score_block(t_next, prev, g, x)
                p_ref[slot, x, rows, :] = jnp.exp2(s_ref[slot, x, rows, :] - shifts[x][g]).astype(BF16)
                acc_ref[x] += gate * _dot(vt_ref[t_prev * G + g, x * R:(x + 1) * R, :], p_ref[prev, x, rows, :])

    trips = (first_own + TQ // T - 1 + G - 1) // G

    def two_trips(u, _):
        trip(2 * u, 0)

        @pl.when(2 * u + 1 < trips)
        def _():
            trip(2 * u + 1, 1)
        return 0

    lax.fori_loop(0, (trips + 1) // 2, two_trips, 0)
    for slot in range(2):
        @pl.when(jnp.logical_and(trips > 0, (trips - 1) % 2 == slot))
        def _():
            fold(trips - 1, slot)
    out_t = jnp.concatenate([acc_ref[x][:H, :] / acc_ref[x][H:H + 1, :] for x in range(2)], axis=0)
    out_ref[...] = out_t.T.astype(BF16)


def _attn_specs(S):
    tile = pl.BlockSpec((ATTN_TILE, LANES), lambda p, i: (i, p))
    resident = pl.BlockSpec((S, LANES), lambda p, i: (0, p))
    return tile, resident


def _moba_attn(q, k, v, kmean):
    S, D = q.shape
    assert MOBA_BLOCK == ATTN_TILE and ATTN_TILE % Q_BLOCK == 0 and S % ATTN_TILE == 0
    nb = S // MOBA_BLOCK
    assert nb % MOBA_GROUP == 0
    tq = min(MOBA_QUERIES, S)
    assert tq % ATTN_TILE == 0 and S % tq == 0
    tile = pl.BlockSpec((tq, LANES), lambda p, i: (i, p))
    _, resident = _attn_specs(S)
    return pl.pallas_call(
        _moba_attn_kernel,
        grid=(D // LANES, S // tq),
        in_specs=[tile, resident, pl.BlockSpec((nb, 2 * VT_ROWS, MOBA_BLOCK), lambda p, i: (0, p, 0)),
                  pl.BlockSpec((nb, LANES), lambda p, i: (0, p))],
        out_specs=tile,
        out_shape=jax.ShapeDtypeStruct((S, D), BF16),
        scratch_shapes=[pltpu.VMEM((2, nb, tq), F32),
                        pltpu.VMEM((2, 2, MOBA_GROUP * ATTN_TILE, tq), F32),
                        pltpu.VMEM((2, 2, MOBA_GROUP, tq), F32),
                        pltpu.VMEM((2, 2, MOBA_GROUP * ATTN_TILE, tq), BF16),
                        pltpu.VMEM((2, 2, 1, tq), F32),
                        pltpu.VMEM((2, 1, tq), F32), pltpu.VMEM((2, VT_ROWS, tq), F32)],
        compiler_params=_cparams("arbitrary", "arbitrary"),
        name="moba_attn",
    )(q, k, v, kmean)


def _sb_attn_kernel(q_ref, k_ref, v_ref, out_ref):
    T = SB_TILE
    i = pl.program_id(1)
    qs = _split_pair(q_ref[...])
    key_row = lax.broadcasted_iota(jnp.int32, (T, T), 0)
    key_col = lax.broadcasted_iota(jnp.int32, (T, T), 1)
    later_keys = (key_row > key_col).astype(BF16)
    strict = key_col < key_row

    def walk(tiles, carry):
        heads = range(len(qs))
        k_t = [k_ref[pl.ds(pl.multiple_of(j * T, T), T), :] for j, _, _ in tiles]
        v_t = [v_ref[pl.ds(pl.multiple_of(j * T, T), T), :] for j, _, _ in tiles]
        z2 = [[_dot_nt(qs[x], k_j) for x in heads] for k_j in k_t]
        sp = [[jnp.maximum(z, 0.0) + jnp.log2(1.0 + jnp.exp2(-jnp.abs(z))) for z in zs] for zs in z2]
        sp = [[jnp.where(strict, s, 0.0) if masked else s for s in ss] for ss, (_, masked, _) in zip(sp, tiles)]
        hi = [[s.astype(BF16) for s in ss] for ss in sp]
        lo = [[(s - h.astype(F32)).astype(BF16) for s, h in zip(ss, hs)] for ss, hs in zip(sp, hi)]
        log_sig = [[z - s for z, s in zip(zs, ss)] for zs, ss in zip(z2, sp)]
        sp_sums = [[jnp.sum(s, axis=1, keepdims=True) for s in ss] for ss in sp]
        later = [[_dot(h, later_keys) + _dot(l, later_keys) for h, l in zip(hs, ls)] for hs, ls in zip(hi, lo)]
        run = [carry[2 * x] for x in heads]
        acc = [carry[2 * x + 1] for x in heads]
        for n, (_, masked, weight) in enumerate(tiles):
            for x in heads:
                a = jnp.exp2(log_sig[n][x] - (later[n][x] + run[x]))
                if masked:
                    a = jnp.where(strict, a, 0.0)
                sp_sum = sp_sums[n][x]
                av = _dot(a.astype(BF16), v_t[n])
                if weight is not None:
                    sp_sum, av = weight * sp_sum, weight * av
                run[x] = run[x] + sp_sum
                acc[x] = acc[x] + av
        return tuple(val for x in heads for val in (run[x], acc[x]))

    def least_run(carry):
        return jnp.min(jnp.minimum(carry[0], carry[2]))

    def more(state):
        j, least = state[0], state[1]
        return jnp.logical_and(j >= 0, least < SB_RUN_CUTOFF)

    def step(state):
        carry = walk([(state[0], False, None)], state[2:])
        return (state[0] - 1, least_run(carry)) + carry

    zero = (jnp.zeros((T, 1), F32), jnp.zeros((T, LANES), F32))
    near = [(i, True, None)]
    near += [(jnp.maximum(i - n, 0), False, jnp.where(i >= n, 1.0, 0.0)) for n in range(1, SB_NEAR_TILES + 1)]
    carry = walk(near, zero + zero)
    state = lax.while_loop(more, step, (i - 1 - SB_NEAR_TILES, least_run(carry)) + carry)
    out_ref[...] = _merge_pair(state[3], state[5]).astype(BF16)


def _sb_attn(q, k, v):
    S, D = q.shape
    assert S % SB_TILE == 0
    _, resident = _attn_specs(S)
    tile = pl.BlockSpec((SB_TILE, LANES), lambda p, i: (i, p))
    return pl.pallas_call(
        _sb_attn_kernel,
        grid=(D // LANES, S // SB_TILE),
        in_specs=[tile, resident, resident],
        out_specs=tile,
        out_shape=jax.ShapeDtypeStruct((S, D), BF16),
        compiler_params=_cparams("arbitrary", "arbitrary"),
        name="sb_attn",
    )(q, k, v)


def _conv_proj_kernel(h_ref, g_ref, wb_ref, wc_ref, wu_ref, cw_ref, out_ref, tail_ref):
    tm = h_ref.shape[0]

    @pl.when(pl.program_id(0) == 0)
    def _():
        tail_ref[...] = jnp.zeros_like(tail_ref)

    a = _rms(h_ref[...], g_ref[...]).astype(BF16)
    z = _dot(a, wc_ref[...]) * _dot(a, wu_ref[...])
    row = lax.broadcasted_iota(jnp.int32, z.shape, 0)
    prev1 = tail_ref[7:8, :]
    prev2 = tail_ref[6:7, :]
    z1 = jnp.where(row == 0, prev1, pltpu.roll(z, 1, axis=0))
    z2 = jnp.where(row == 0, prev2, jnp.where(row == 1, prev1, pltpu.roll(z, 2, axis=0)))
    y = cw_ref[0:1, :] * z2 + cw_ref[1:2, :] * z1 + cw_ref[2:3, :] * z
    tail_ref[...] = z[tm - 8:, :]
    out_ref[...] = (_dot(a, wb_ref[...]) * y).astype(BF16)


def _conv_mixer_pre(h, gain, w_in, conv_w):
    S, D = h.shape
    assert conv_w.shape[0] == CONV_WIDTH == 3
    w = w_in.astype(BF16)
    cw = jnp.pad(conv_w.astype(F32), ((0, 8 - CONV_WIDTH), (0, 0)))
    tm = min(ROW_TILE, S)
    row = pl.BlockSpec((tm, D), lambda i: (i, 0))
    return pl.pallas_call(
        _conv_proj_kernel,
        grid=(S // tm,),
        in_specs=[row, _const_spec((1, D)), _col_spec(D, D, 0), _col_spec(D, D, 1), _col_spec(D, D, 2),
                  _const_spec((8, D))],
        out_specs=row,
        out_shape=jax.ShapeDtypeStruct((S, D), BF16),
        scratch_shapes=[pltpu.VMEM((8, D), F32)],
        compiler_params=_cparams("arbitrary"),
        name="conv_proj",
    )(h, gain.reshape(1, D), w, w, w, cw)


def _post_kernel(*refs, final):
    if final:
        h_ref, x_ref, wo_ref, g_ref, wg_ref, wu_ref, wd_ref, fg_ref, out_ref = refs
    else:
        h_ref, x_ref, wo_ref, g_ref, wg_ref, wu_ref, wd_ref, out_ref = refs
    h1 = h_ref[...] + _dot(x_ref[...], wo_ref[...])
    a = _rms(h1, g_ref[...]).astype(BF16)
    F = wg_ref.shape[1]
    y = h1
    for lo in range(0, F, FFN_CHUNK):
        hi = min(lo + FFN_CHUNK, F)
        gate = _dot(a, wg_ref[:, lo:hi])
        up = _dot(a, wu_ref[:, lo:hi])
        act = (gate * jax.nn.sigmoid(gate) * up).astype(BF16)
        y = y + _dot(act, wd_ref[lo:hi, :])
    if final:
        y = _rms(y, fg_ref[...])
    out_ref[...] = y


def _post(h, x, w_out, gain, layer, w_gate, w_up, w_down, final_gain=None):
    S, D = h.shape
    K = x.shape[1]
    F = w_gate.shape[2]
    assert F % LANES == 0
    final = final_gain is not None
    tm = min(ROW_TILE, S)

    def layer_spec(rows, cols):
        return pl.BlockSpec((None, rows, cols), lambda i: (layer, 0, 0), pipeline_mode=pl.Buffered(1))

    in_specs = [pl.BlockSpec((tm, D), lambda i: (i, 0)), pl.BlockSpec((tm, K), lambda i: (i, 0)),
                _const_spec((K, D)), _const_spec((1, D)), layer_spec(D, F), layer_spec(D, F), layer_spec(F, D)]
    args = [h, x, w_out.astype(BF16), gain.reshape(1, D), w_gate.astype(BF16), w_up.astype(BF16),
            w_down.astype(BF16)]
    if final:
        in_specs.append(_const_spec((1, D)))
        args.append(final_gain.reshape(1, D))
    return pl.pallas_call(
        functools.partial(_post_kernel, final=final),
        grid=(S // tm,),
        in_specs=in_specs,
        out_specs=pl.BlockSpec((tm, D), lambda i: (i, 0)),
        out_shape=jax.ShapeDtypeStruct((S, D), F32),
        compiler_params=_cparams("arbitrary"),
        name="post_final" if final else "post",
    )(*args)


def _mixer(kind, j, h, gain, positions, p):
    if kind == 0:
        q, k, v, o, gates = _mlstm_proj(h, gain, p["mlstm_w_in"][j])
        return _mlstm_core(q, k, v, o, gates, p["mlstm_b_gate"][j], p["mlstm_head_gain"][j]), p["mlstm_w_out"][j]
    if kind == 1:
        q, k, v, kmean = _qkv_proj(h, gain, p["moba_w_qkv"][j], positions)
        return _moba_attn(q, k, v, kmean), p["moba_w_out"][j]
    if kind == 2:
        return _conv_mixer_pre(h, gain, p["conv_w_in"][j], p["conv_w"][j]), p["conv_w_out"][j]
    q, k, v = _qkv_proj(h, gain, p["sb_w_qkv"][j])
    return _sb_attn(q, k, v), p["sb_w_out"][j]


def kernel(x, positions, norm_gains, mlstm_w_in, mlstm_b_gate, mlstm_head_gain, mlstm_w_out, moba_w_qkv, moba_w_out, conv_w_in, conv_w, conv_w_out, sb_w_qkv, sb_w_out, ffn_w_gate, ffn_w_up, ffn_w_down, final_gain):
    B, S, D = x.shape
    assert D == D_MODEL
    depth = norm_gains.shape[0]
    p = dict(mlstm_w_in=mlstm_w_in, mlstm_b_gate=mlstm_b_gate, mlstm_head_gain=mlstm_head_gain,
             mlstm_w_out=mlstm_w_out, moba_w_qkv=moba_w_qkv, moba_w_out=moba_w_out,
             conv_w_in=conv_w_in, conv_w=conv_w, conv_w_out=conv_w_out,
             sb_w_qkv=sb_w_qkv, sb_w_out=sb_w_out)
    outs = []
    for b in range(B):
        h = x[b]
        for layer in range(depth):
            kind, j = layer % 4, layer // 4
            mix, w_out = _mixer(kind, j, h, norm_gains[layer, 0], positions[b], p)
            h = _post(h, mix, w_out, norm_gains[layer, 1], layer, ffn_w_gate, ffn_w_up, ffn_w_down,
                      final_gain if layer == depth - 1 else None)
        outs.append(h)
    return jnp.stack(outs)
```
